```python
import jax, jax.numpy as jnp
from jax import lax
import numpy as np

D_MODEL = 1024
BATCH = 8
SEQ = 4096
DEPTH = 2
DEC_BATCH = 8
DEC_SEQ = 8192
PAST_LEN = 128

HEAD_DIM = 64
A_KV_HEADS = 2
A_GROUP = 4
A_HEADS = A_KV_HEADS * A_GROUP
A_WINDOW = 128
B_PAIRS = ((128, 1), (512, 4), (2048, 16))
B_KV_HEADS = 3
B_GROUP = 2
B_HEADS = B_KV_HEADS * B_GROUP
C_HEADS = 4
C_HEAD_DIM = 128
C_CHUNK = 128
C_CONV = 3
A_Q = A_HEADS * HEAD_DIM
A_KV = A_KV_HEADS * HEAD_DIM
B_Q = B_HEADS * HEAD_DIM
B_KV = B_KV_HEADS * HEAD_DIM
C_W = C_HEADS * C_HEAD_DIM
C_GATES = 4 * C_HEADS
N_BRANCH = 3
COL_SIZES = (A_Q, A_KV, A_KV, B_Q, B_KV, B_KV, C_W, C_W, C_W, C_W, C_GATES, N_BRANCH * D_MODEL)
N_IN = A_Q + 2 * A_KV + B_Q + 2 * B_KV + 4 * C_W + C_GATES + N_BRANCH * D_MODEL
C_GATE_OFF = A_Q + 2 * A_KV + B_Q + 2 * B_KV + 4 * C_W
N_EXPERTS = 256
TOP_K = 8
N_EXPERT_GROUPS = 8
TOPK_GROUPS = 4
EXPERTS_PER_GROUP = N_EXPERTS // N_EXPERT_GROUPS
D_EXPERT = 256
ROUTED_SCALE = 2.5
MOE_BLOCK = 256
LN_EPS = 1e-5
ALPHA = (2 * DEPTH) ** 0.25
BETA = (8 * DEPTH) ** -0.25
NEG = -1e30

kernel_name = 'hybrid_bidir_gated_encoder'


def _split(a, sizes):
    out, idx = [], 0
    for s in sizes:
        out.append(a[..., idx:idx + s])
        idx += s
    return out


def alibi_slopes(n):
    return jnp.exp2(-8.0 * jnp.arange(1, n + 1, dtype=jnp.float32) / n)


def layer_norm(x, g, b):
    xf = x.astype(jnp.float32)
    mu = xf.mean(-1, keepdims=True)
    var = jnp.square(xf - mu).mean(-1, keepdims=True)
    return ((xf - mu) * lax.rsqrt(var + LN_EPS) * g.astype(jnp.float32) + b.astype(jnp.float32)).astype(x.dtype)


def _three_blocks(a):
    return jnp.concatenate([a[:, :-2], a[:, 1:-1], a[:, 2:]], axis=2)


def banded_attention(q, k, v, kvalid, slopes, dist_scale, band, sink=None):
    n, L, hk, g, dh = q.shape
    nb = L // band
    qb = q.reshape(n, nb, band, hk, g, dh)
    pad = ((0, 0), (band, band), (0, 0), (0, 0))
    kb = _three_blocks(jnp.pad(k, pad).reshape(n, nb + 2, band, hk, dh))
    vb = _three_blocks(jnp.pad(v, pad).reshape(n, nb + 2, band, hk, dh))
    mb = _three_blocks(jnp.pad(jnp.broadcast_to(kvalid, (n, L)), ((0, 0), (band, band))).reshape(n, nb + 2, band))
    rel = jnp.abs(jnp.arange(band)[:, None] - jnp.arange(3 * band)[None, :] + band)
    mask = (rel <= band)[None, None] & mb[:, :, None, :]
    scores = jnp.einsum('nbqhgd,nbkhd->nbhgqk', qb, kb, preferred_element_type=jnp.float32) * (dh ** -0.5)
    bias = -slopes.astype(jnp.float32)[:, :, None, None] * (rel * dist_scale).astype(jnp.float32)
    logits = jnp.where(mask[:, :, None, None], scores + bias, NEG)
    m = logits.max(-1)
    if sink is not None:
        sk = sink.astype(jnp.float32)[:, :, None]
        m = jnp.maximum(m, sk)
    p = jnp.exp(logits - m[..., None])
    denom = p.sum(-1)
    if sink is not None:
        denom = denom + jnp.exp(sk - m)
    out = jnp.einsum('nbhgqk,nbkhd->nbhgqd', p, vb.astype(jnp.float32)) / denom[..., None]
    lse = m + jnp.log(denom)
    out = out.transpose(0, 1, 4, 2, 3, 5).reshape(n, L, hk, g, dh).astype(q.dtype)
    lse = lse.transpose(0, 1, 4, 2, 3).reshape(n, L, hk, g)
    return out, lse


def dilated_group(q, k, v, slopes, dil, band):
    n, S = q.shape[:2]
    span = dil * band
    Sp = -(-S // span) * span
    L = Sp // dil

    def to_sub(a):
        a = jnp.pad(a, [(0, 0), (0, Sp - S)] + [(0, 0)] * (a.ndim - 2))
        a = jnp.moveaxis(a.reshape((n, L, dil) + a.shape[2:]), 2, 1)
        return a.reshape((n * dil, L) + a.shape[3:])

    def from_sub(a):
        a = jnp.moveaxis(a.reshape((n, dil, L) + a.shape[2:]), 1, 2)
        return a.reshape((n, Sp) + a.shape[3:])[:, :S]

    pos = jnp.arange(L)[None, :] * dil + jnp.arange(dil)[:, None]
    kvalid = jnp.broadcast_to((pos < S)[None], (n, dil, L)).reshape(n * dil, L)
    out, lse = banded_attention(to_sub(q), to_sub(k), to_sub(v), kvalid, slopes, dil, band)
    return from_sub(out), from_sub(lse)


def centred_depthwise_conv(x, w):
    return lax.conv_general_dilated(x, w[:, None, :].astype(x.dtype), window_strides=(1,), padding='SAME',
                                    dimension_numbers=('NWC', 'WIO', 'NWC'), feature_group_count=x.shape[-1])


def mlstm_scan(q, k, v, li, lf):
    n, H, S, dk = q.shape
    dv = v.shape[-1]
    nc = S // C_CHUNK
    chunks = lambda a: jnp.moveaxis(a.reshape((n, H, nc, C_CHUNK) + a.shape[3:]), 2, 0)
    tril = jnp.tril(jnp.ones((C_CHUNK, C_CHUNK), bool))

    def step(carry, inp):
        C, nv, m = carry
        qc, kc, vc, lic, lfc = inp
        b = jnp.cumsum(lfc, axis=-1)
        D = jnp.where(tril, b[..., :, None] - b[..., None, :] + lic[..., None, :], NEG)
        m_inter = b + m[..., None]
        mt = jnp.maximum(D.max(-1), m_inter)
        W = jnp.exp(D - mt[..., None]) * jnp.einsum('nhtd,nhsd->nhts', qc, kc)
        inter = jnp.exp(m_inter - mt)
        num = jnp.einsum('nhts,nhse->nhte', W, vc) + inter[..., None] * jnp.einsum('nhtd,nhde->nhte', qc, C)
        den = W.sum(-1) + inter * jnp.einsum('nhtd,nhd->nht', qc, nv)
        h = num / jnp.maximum(jnp.abs(den), jnp.exp(-mt))[..., None]
        g = b[..., -1:] - b + lic
        m_new = jnp.maximum(b[..., -1] + m, g.max(-1))
        wk = jnp.exp(g - m_new[..., None])
        decay = jnp.exp(b[..., -1] + m - m_new)
        C = decay[..., None, None] * C + jnp.einsum('nhs,nhsd,nhse->nhde', wk, kc, vc)
        nv = decay[..., None] * nv + jnp.einsum('nhs,nhsd->nhd', wk, kc)
        return (C, nv, m_new), h

    init = (jnp.zeros((n, H, dk, dv), jnp.float32), jnp.zeros((n, H, dk), jnp.float32), jnp.zeros((n, H), jnp.float32))
    _, hs = lax.scan(step, init, (chunks(q), chunks(k), chunks(v), chunks(li), chunks(lf)))
    return jnp.moveaxis(hs, 0, 2).reshape(n, H, S, dv)


def token_mixer(x, w_in, b_in, sink, conv_w, w_br_a, w_br_b, w_br_c, w_out):
    n, S, _ = x.shape
    proj = x @ w_in + b_in.astype(x.dtype)
    qa, ka, va, qb, kb, vb, qc, kc, vc, oc, gc, gates = _split(proj, COL_SIZES)

    out_a, _ = banded_attention(qa.reshape(n, S, A_KV_HEADS, A_GROUP, HEAD_DIM),
                                ka.reshape(n, S, A_KV_HEADS, HEAD_DIM), va.reshape(n, S, A_KV_HEADS, HEAD_DIM),
                                jnp.ones((1, S), bool), alibi_slopes(A_HEADS).reshape(A_KV_HEADS, A_GROUP),
                                1, A_WINDOW, sink)
    ya = out_a.reshape(n, S, A_Q) @ w_br_a

    qb = qb.reshape(n, S, B_KV_HEADS, B_GROUP, HEAD_DIM)
    kb = kb.reshape(n, S, B_KV_HEADS, HEAD_DIM)
    vb = vb.reshape(n, S, B_KV_HEADS, HEAD_DIM)
    slopes_b = alibi_slopes(B_HEADS).reshape(B_KV_HEADS, B_GROUP)
    outs, lses = [], []
    for gi, (win, dil) in enumerate(B_PAIRS):
        o, l = dilated_group(qb[:, :, gi:gi + 1], kb[:, :, gi:gi + 1], vb[:, :, gi:gi + 1],
                             slopes_b[gi:gi + 1], dil, win // (2 * dil))
        outs.append(o[:, :, 0])
        lses.append(l[:, :, 0])
    wts = jax.nn.softmax(jnp.stack(lses, 2), axis=2)
    out_b = jnp.stack(outs, 2) * wts[..., None].astype(x.dtype)
    yb = out_b.reshape(n, S, B_Q) @ w_br_b

    qk = jax.nn.silu(centred_depthwise_conv(jnp.concatenate([qc, kc], -1), conv_w))
    heads = lambda a: a.reshape(n, S, C_HEADS, C_HEAD_DIM).transpose(0, 2, 1, 3).astype(jnp.float32)
    qh = heads(qk[..., :C_W])
    kh = heads(qk[..., C_W:]) * (C_HEAD_DIM ** -0.5)
    vh = heads(vc)
    li_f, f_f, li_b, f_b = gc.astype(jnp.float32).reshape(n, S, 4, C_HEADS).transpose(2, 0, 3, 1)
    h_f = mlstm_scan(qh, kh, vh, li_f, jax.nn.log_sigmoid(f_f))
    fl = lambda a: jnp.flip(a, axis=2)
    h_b = fl(mlstm_scan(fl(qh), fl(kh), fl(vh), jnp.flip(li_b, -1), jnp.flip(jax.nn.log_sigmoid(f_b), -1)))
    h_c = (h_f + h_b).transpose(0, 2, 1, 3).reshape(n, S, C_W).astype(x.dtype)
    yc = (jax.nn.sigmoid(oc) * h_c) @ w_br_c

    g = jax.nn.sigmoid(gates.reshape(n, S, N_BRANCH, D_MODEL))
    merged = g[:, :, 0] * ya + g[:, :, 1] * yb + g[:, :, 2] * yc
    return merged @ w_out


def moe_ffn(x, w_router, router_bias, w_exp_gu, w_exp_down, w_sh_gu, w_sh_down):
    n, S, D = x.shape
    T = n * S
    xf = x.reshape(T, D)
    scores = jax.nn.sigmoid(jnp.einsum('td,de->te', xf, w_router, preferred_element_type=jnp.float32))
    biased = scores + router_bias.astype(jnp.float32)
    grp_score = lax.top_k(biased.reshape(T, N_EXPERT_GROUPS, EXPERTS_PER_GROUP), 2)[0].sum(-1)
    _, grp_idx = lax.top_k(grp_score, TOPK_GROUPS)
    grp_mask = jax.nn.one_hot(grp_idx, N_EXPERT_GROUPS, dtype=jnp.float32).sum(-2) > 0
    exp_mask = jnp.repeat(grp_mask, EXPERTS_PER_GROUP, axis=-1)
    _, top_idx = lax.top_k(jnp.where(exp_mask, biased, -jnp.inf), TOP_K)
    top_s = jnp.take_along_axis(scores, top_idx, -1)
    top_w = top_s / top_s.sum(-1, keepdims=True) * ROUTED_SCALE

    A = T * TOP_K
    flat_e = top_idx.reshape(-1)
    order = jnp.argsort(flat_e, stable=True)
    sorted_e = flat_e[order]
    tok = (order // TOP_K).astype(jnp.int32)
    wts = top_w.reshape(-1)[order]
    counts = jnp.bincount(flat_e, length=N_EXPERTS)
    padded = (counts + MOE_BLOCK - 1) // MOE_BLOCK * MOE_BLOCK
    pad_end = jnp.cumsum(padded)
    pad_start = pad_end - padded
    start = jnp.cumsum(counts) - counts
    dest = pad_start[sorted_e] + jnp.arange(A, dtype=jnp.int32) - start[sorted_e]
    n_blocks = -(-A // MOE_BLOCK) + N_EXPERTS
    P = n_blocks * MOE_BLOCK
    buf_tok = jnp.full((P,), T, jnp.int32).at[dest].set(tok)
    buf_w = jnp.zeros((P,), jnp.float32).at[dest].set(wts)
    blk_e = jnp.minimum(jnp.searchsorted(pad_end, jnp.arange(n_blocks, dtype=jnp.int32) * MOE_BLOCK, side='right'),
                        N_EXPERTS - 1)
    x_pad = jnp.concatenate([xf, jnp.zeros((1, D), xf.dtype)], 0)

    def expert_block(acc, blk):
        e, idx, wt = blk
        gu = x_pad[idx] @ w_exp_gu[e]
        y = (jax.nn.silu(gu[:, :D_EXPERT]) * gu[:, D_EXPERT:]) @ w_exp_down[e]
        return acc.at[idx].add(y * wt[:, None].astype(y.dtype)), None

    acc, _ = lax.scan(expert_block, jnp.zeros((T + 1, D), xf.dtype),
                      (blk_e, buf_tok.reshape(n_blocks, MOE_BLOCK), buf_w.reshape(n_blocks, MOE_BLOCK)))
    gu_s = xf @ w_sh_gu
    shared = (jax.nn.silu(gu_s[:, :D_EXPERT]) * gu_s[:, D_EXPERT:]) @ w_sh_down
    return (acc[:T] + shared).reshape(n, S, D)


def trunk(x, w_in, b_in, sink, conv_w, w_br_a, w_br_b, w_br_c, w_out, ln1_g, ln1_b,
          w_router, router_bias, w_exp_gu, w_exp_down, w_sh_gu, w_sh_down, ln2_g, ln2_b):
    for l in range(DEPTH):
        h = token_mixer(x, w_in[l], b_in[l], sink[l], conv_w[l], w_br_a[l], w_br_b[l], w_br_c[l], w_out[l])
        x = layer_norm(ALPHA * x + h, ln1_g[l], ln1_b[l])
        f = moe_ffn(x, w_router[l], router_bias[l], w_exp_gu[l], w_exp_down[l], w_sh_gu[l], w_sh_down[l])
        x = layer_norm(ALPHA * x + f, ln2_g[l], ln2_b[l])
    return x


def setup_inputs(seed: int = 0) -> dict:
    key = jax.random.key(seed)
    ks = jax.random.split(key, 22)
    nrm = lambda k, shape, scale: jax.random.normal(k, shape, jnp.float32) * scale
    x_prompt = nrm(ks[0], (BATCH, SEQ, D_MODEL), 1.0)
    x_sample = nrm(ks[1], (DEC_BATCH, DEC_SEQ, D_MODEL), 1.0)
    w_in = nrm(ks[2], (DEPTH, D_MODEL, N_IN), D_MODEL ** -0.5)
    fbias = jnp.linspace(3.0, 6.0, C_HEADS, dtype=jnp.float32)
    b_in = nrm(ks[3], (DEPTH, N_IN), 0.02)
    b_in = b_in.at[:, C_GATE_OFF + C_HEADS:C_GATE_OFF + 2 * C_HEADS].add(fbias)
    b_in = b_in.at[:, C_GATE_OFF + 3 * C_HEADS:C_GATE_OFF + 4 * C_HEADS].add(fbias)
    sink = nrm(ks[4], (DEPTH, A_KV_HEADS, A_GROUP), 0.5)
    conv_w = nrm(ks[5], (DEPTH, C_CONV, 2 * C_W), C_CONV ** -0.5)
    w_br_a = nrm(ks[6], (DEPTH, A_Q, D_MODEL), A_Q ** -0.5)
    w_br_b = nrm(ks[7], (DEPTH, B_Q, D_MODEL), B_Q ** -0.5)
    w_br_c = nrm(ks[8], (DEPTH, C_W, D_MODEL), C_W ** -0.5)
    w_out = nrm(ks[9], (DEPTH, D_MODEL, D_MODEL), BETA * D_MODEL ** -0.5)
    ln1_g = 1.0 + nrm(ks[10], (DEPTH, D_MODEL), 0.02)
    ln1_b = nrm(ks[11], (DEPTH, D_MODEL), 0.02)
    w_router = nrm(ks[12], (DEPTH, D_MODEL, N_EXPERTS), D_MODEL ** -0.5)
    router_bias = nrm(ks[13], (DEPTH, N_EXPERTS), 0.01)
    w_exp_gu = nrm(ks[14], (DEPTH, N_EXPERTS, D_MODEL, 2 * D_EXPERT), D_MODEL ** -0.5)
    w_exp_down = nrm(ks[15], (DEPTH, N_EXPERTS, D_EXPERT, D_MODEL), BETA * D_EXPERT ** -0.5)
    w_sh_gu = nrm(ks[16], (DEPTH, D_MODEL, 2 * D_EXPERT), D_MODEL ** -0.5)
    w_sh_down = nrm(ks[17], (DEPTH, D_EXPERT, D_MODEL), BETA * D_EXPERT ** -0.5)
    ln2_g = 1.0 + nrm(ks[18], (DEPTH, D_MODEL), 0.02)
    ln2_b = nrm(ks[19], (DEPTH, D_MODEL), 0.02)
    return {'x_prompt': x_prompt, 'x_sample': x_sample, 'w_in': w_in, 'b_in': b_in, 'sink': sink,
            'conv_w': conv_w, 'w_br_a': w_br_a, 'w_br_b': w_br_b, 'w_br_c': w_br_c, 'w_out': w_out,
            'ln1_g': ln1_g, 'ln1_b': ln1_b, 'w_router': w_router, 'router_bias': router_bias,
            'w_exp_gu': w_exp_gu, 'w_exp_down': w_exp_down, 'w_sh_gu': w_sh_gu, 'w_sh_down': w_sh_down,
            'ln2_g': ln2_g, 'ln2_b': ln2_b}


def reference(x_prompt, x_sample, w_in, b_in, sink, conv_w, w_br_a, w_br_b, w_br_c, w_out, ln1_g, ln1_b,
              w_router, router_bias, w_exp_gu, w_exp_down, w_sh_gu, w_sh_down, ln2_g, ln2_b):
    y_prompt = trunk(x_prompt, w_in, b_in, sink, conv_w, w_br_a, w_br_b, w_br_c, w_out, ln1_g, ln1_b,
                     w_router, router_bias, w_exp_gu, w_exp_down, w_sh_gu, w_sh_down, ln2_g, ln2_b)
    y_sample = trunk(x_sample, w_in, b_in, sink, conv_w, w_br_a, w_br_b, w_br_c, w_out, ln1_g, ln1_b,
                     w_router, router_bias, w_exp_gu, w_exp_down, w_sh_gu, w_sh_down, ln2_g, ln2_b)
    return (y_prompt, y_sample)
```

```python
import functools

import numpy as np
import jax
import jax.numpy as jnp
from jax import lax
from jax.experimental import pallas as pl
from jax.experimental.pallas import tpu as pltpu

F32, BF16, I32 = jnp.float32, jnp.bfloat16, jnp.int32

LANES = 128
D = 1024
HD = 64
A_HK, A_G, A_BAND = 2, 4, 128
B_G, B_BAND = 2, 64
B_DILS = (1, 4, 16)
C_H, C_HD, CHUNK = 4, 128, 128
NE, TOPK, NGRP, TOPG, DE = 256, 8, 8, 4, 256
EPG = NE // NGRP
ROUTED_SCALE = 2.5
LN_EPS = 1e-5
NEG = -1e30
QB = 128

U_QA, U_KA, U_VA = 0, 4, 5
U_QB = (6, 8, 10)
U_KVB = (7, 9, 11)
U_QC, U_KC, U_VC, U_OC, U_GC, U_GATES = 12, 16, 20, 24, 28, 32
N_UNITS = 56
NPAD = N_UNITS * LANES


def _edge_flags(seq_lens, rows):
    first, last = [], []
    for s in seq_lens:
        nb = s // rows
        first += [1] + [0] * (nb - 1)
        last += [0] * (nb - 1) + [1]
    return np.array([first, last], np.int32)


def _mm(a, b):
    return jnp.dot(a.astype(BF16), b.astype(BF16), preferred_element_type=F32)


def _mm_nt(a, b):
    return lax.dot_general(a.astype(BF16), b.astype(BF16), (((1,), (1,)), ((), ())), preferred_element_type=F32)


def _mm_tn(a, b):
    return lax.dot_general(a.astype(BF16), b.astype(BF16), (((0,), (0,)), ((), ())), preferred_element_type=F32)


def _split3(x):
    hi = x.astype(BF16)
    r1 = x - hi.astype(F32)
    mid = r1.astype(BF16)
    lo = (r1 - mid.astype(F32)).astype(BF16)
    return hi, mid, lo


def _sigmoid(x):
    return 1.0 / (1.0 + jnp.exp(-x))


def _log_sigmoid(x):
    return jnp.minimum(x, 0.0) - jnp.log(1.0 + jnp.exp(-jnp.abs(x)))


def _layer_norm(x, g, b):
    mu = jnp.mean(x, axis=-1, keepdims=True)
    xc = x - mu
    var = jnp.mean(xc * xc, axis=-1, keepdims=True)
    return xc * lax.rsqrt(var + LN_EPS) * g + b


def _proj_kernel(x_ref, w_ref, b_ref, o_ref, xb_ref):
    @pl.when(pl.program_id(1) == 0)
    def _():
        xb_ref[...] = x_ref[...].astype(BF16)

    o_ref[...] = jnp.dot(xb_ref[...], w_ref[...], preferred_element_type=F32) + b_ref[...]


def _proj(x, w, b, tm, tn):
    T, N = x.shape[0], w.shape[1]
    return pl.pallas_call(
        _proj_kernel,
        grid=(T // tm, N // tn),
        in_specs=[pl.BlockSpec((tm, D), lambda i, j: (i, 0)),
                  pl.BlockSpec((D, tn), lambda i, j: (0, j)),
                  pl.BlockSpec((1, tn), lambda i, j: (0, j))],
        out_specs=pl.BlockSpec((tm, tn), lambda i, j: (i, j)),
        out_shape=jax.ShapeDtypeStruct((T, N), F32),
        scratch_shapes=[pltpu.VMEM((tm, D), BF16)],
        compiler_params=pltpu.CompilerParams(dimension_semantics=("parallel", "arbitrary")),
        name="proj",
    )(x, w, b)


def _attn_kernel(flags_ref, *refs, hk, g, band, dist, slopes, packed, use_sink, blk_axis):
    if use_sink:
        sink_ref, refs = refs[0], refs[1:]
    if packed:
        q_ref, kp_ref, ko_ref, kn_ref, o_ref, lse_ref = refs
        vp_ref, vo_ref, vn_ref = kp_ref, ko_ref, kn_ref
    else:
        q_ref, kp_ref, ko_ref, kn_ref, vp_ref, vo_ref, vn_ref, o_ref = refs
    b = pl.program_id(blk_axis)
    lo = jnp.where(flags_ref[0, b] == 0, 0, QB)
    hi = jnp.where(flags_ref[1, b] == 0, 3 * QB, 2 * QB)
    rows = g * QB
    row = lax.broadcasted_iota(I32, (rows, 3 * QB), 0)
    col = lax.broadcasted_iota(I32, (rows, 3 * QB), 1)
    rel = jnp.abs(col - QB - (row & (QB - 1)))
    mask = (rel <= band) & (col >= lo) & (col < hi)
    relf = rel.astype(F32) * float(dist)
    row_g = jnp.right_shift(lax.broadcasted_iota(I32, (rows, 1), 0), QB.bit_length() - 1)
    outs, lses = [], []
    for h in range(hk):
        qs = jnp.concatenate([q_ref[:, (h * g + gg) * HD:(h * g + gg + 1) * HD] for gg in range(g)], axis=0)
        qs = qs * (HD ** -0.5)
        ko = h * HD
        vo = HD if packed else h * HD
        kh = jnp.concatenate([r[:, ko:ko + HD] for r in (kp_ref, ko_ref, kn_ref)], axis=0)
        vh = jnp.concatenate([r[:, vo:vo + HD] for r in (vp_ref, vo_ref, vn_ref)], axis=0)
        s = _mm_nt(qs, kh)
        slope_col = jnp.zeros((rows, 1), F32)
        for gg in range(g):
            slope_col = jnp.where(row_g == gg, float(slopes[h * g + gg]), slope_col)
        logits = jnp.where(mask, s - slope_col * relf, NEG)
        m = jnp.max(logits, axis=-1, keepdims=True)
        if use_sink:
            sink_col = jnp.zeros((rows, 1), F32)
            for gg in range(g):
                sink_col = jnp.where(row_g == gg, sink_ref[0, h * g + gg], sink_col)
            m = jnp.maximum(m, sink_col)
        p = jnp.exp(logits - m)
        den = jnp.sum(p, axis=-1, keepdims=True)
        if use_sink:
            den = den + jnp.exp(sink_col - m)
        o = _mm(p, vh) / den
        for gg in range(g):
            outs.append(o[gg * QB:(gg + 1) * QB])
        if packed:
            l = m + jnp.log(den)
            for gg in range(g):
                lses.append(jnp.broadcast_to(l[gg * QB:(gg + 1) * QB], (QB, HD)))
    o_ref[...] = jnp.concatenate(outs, axis=1).astype(o_ref.dtype)
    if packed:
        lse_ref[...] = jnp.concatenate(lses, axis=1)


def _attn_a(proj, flags, sink, slopes):
    T = proj.shape[0]
    nb = T // QB
    prev = lambda b, f: jnp.maximum(b - 1, 0)
    nxt = lambda b, f: jnp.minimum(b + 1, nb - 1)
    kspecs = lambda u: [pl.BlockSpec((QB, LANES), lambda b, f: (prev(b, f), u)),
                        pl.BlockSpec((QB, LANES), lambda b, f: (b, u)),
                        pl.BlockSpec((QB, LANES), lambda b, f: (nxt(b, f), u))]
    kern = functools.partial(_attn_kernel, hk=A_HK, g=A_G, band=A_BAND, dist=1, slopes=slopes, packed=False,
                             use_sink=True, blk_axis=0)
    return pl.pallas_call(
        kern,
        grid_spec=pltpu.PrefetchScalarGridSpec(
            num_scalar_prefetch=1, grid=(nb,),
            in_specs=[pl.BlockSpec(memory_space=pltpu.SMEM),
                      pl.BlockSpec((QB, A_HK * A_G * HD), lambda b, f: (b, U_QA // 4))]
            + kspecs(U_KA) + kspecs(U_VA),
            out_specs=pl.BlockSpec((QB, A_HK * A_G * HD), lambda b, f: (b, 0))),
        out_shape=jax.ShapeDtypeStruct((T, A_HK * A_G * HD), BF16),
        compiler_params=pltpu.CompilerParams(dimension_semantics=("parallel",)),
        name="attn_a",
    )(flags, sink, proj, proj, proj, proj, proj, proj, proj)


def _attn_b(proj, flags, gi, slopes):
    dil = B_DILS[gi]
    T = proj.shape[0]
    L = T // dil
    nb = L // QB
    pv = proj.reshape(L, dil * NPAD)
    uq, ukv = U_QB[gi], U_KVB[gi]
    kern = functools.partial(_attn_kernel, hk=1, g=B_G, band=B_BAND, dist=dil, slopes=slopes, packed=True,
                             use_sink=False, blk_axis=1)
    out, lse = pl.pallas_call(
        kern,
        grid_spec=pltpu.PrefetchScalarGridSpec(
            num_scalar_prefetch=1, grid=(dil, nb),
            in_specs=[pl.BlockSpec((QB, LANES), lambda r, b, f: (b, r * N_UNITS + uq)),
                      pl.BlockSpec((QB, LANES), lambda r, b, f: (jnp.maximum(b - 1, 0), r * N_UNITS + ukv)),
                      pl.BlockSpec((QB, LANES), lambda r, b, f: (b, r * N_UNITS + ukv)),
                      pl.BlockSpec((QB, LANES), lambda r, b, f: (jnp.minimum(b + 1, nb - 1), r * N_UNITS + ukv))],
            out_specs=[pl.BlockSpec((QB, LANES), lambda r, b, f: (b, r)),
                       pl.BlockSpec((QB, LANES), lambda r, b, f: (b, r))]),
        out_shape=[jax.ShapeDtypeStruct((L, dil * LANES), F32), jax.ShapeDtypeStruct((L, dil * LANES), F32)],
        compiler_params=pltpu.CompilerParams(dimension_semantics=("parallel", "parallel")),
        name=f"attn_b{gi}",
    )(flags, pv, pv, pv, pv)
    return out.reshape(T, LANES), lse.reshape(T, LANES)


def _conv_silu(x, xp8, xn8, w, keep_prev, keep_next):
    rowi = lax.broadcasted_iota(I32, (CHUNK, 1), 0)
    x_prev = jnp.where(rowi == 0, xp8[7:8, :] * keep_prev, pltpu.roll(x, 1, 0))
    x_next = jnp.where(rowi == CHUNK - 1, xn8[0:1, :] * keep_next, pltpu.roll(x, CHUNK - 1, 0))
    y = w[0:1, :] * x_prev + w[1:2, :] * x + w[2:3, :] * x_next
    return y * _sigmoid(y)


def _mlstm_dir(rev, first, last, q_ref, qp_ref, qn_ref, k_ref, kp_ref, kn_ref, v_ref, g_ref, gt_ref, cw_ref,
               c_ref, n_ref, m_ref, h_ref):
    sd = 1 if rev else 0
    keep_prev = (1 - first).astype(F32)
    keep_next = (1 - last).astype(F32)

    @pl.when((last if rev else first) == 1)
    def _():
        c_ref[sd] = jnp.zeros(c_ref.shape[1:], F32)
        n_ref[sd] = jnp.zeros(n_ref.shape[1:], F32)
        m_ref[sd] = jnp.zeros(m_ref.shape[1:], F32)

    W = C_H * C_HD
    q = _conv_silu(q_ref[...], qp_ref[...], qn_ref[...], cw_ref[:, :W], keep_prev, keep_next)
    k = _conv_silu(k_ref[...], kp_ref[...], kn_ref[...], cw_ref[:, W:], keep_prev, keep_next) * (C_HD ** -0.5)
    v = v_ref[...]
    G = g_ref[...]
    GT = gt_ref[...]
    ti = lax.broadcasted_iota(I32, (CHUNK, CHUNK), 0)
    si = lax.broadcasted_iota(I32, (CHUNK, CHUNK), 1)
    tri = (si >= ti) if rev else (si <= ti)
    L = jnp.where(tri, 1.0, 0.0).astype(BF16)
    Lt = jnp.where((ti >= si) if rev else (ti <= si), 1.0, 0.0).astype(BF16)
    cum_col = sum(jnp.dot(L, p, preferred_element_type=F32) for p in _split3(_log_sigmoid(G)))
    cum_row = sum(jnp.dot(p, Lt, preferred_element_type=F32) for p in _split3(_log_sigmoid(GT)))
    e = 0 if rev else CHUNK - 1
    hs = []
    for h in range(C_H):
        ci = (2 * C_H if rev else 0) + h
        cf = ci + C_H
        cum_c, cum_r = cum_col[:, cf:cf + 1], cum_row[cf:cf + 1, :]
        li_c, li_r = G[:, ci:ci + 1], GT[ci:ci + 1, :]
        qh, kh, vh = (a[:, h * C_HD:(h + 1) * C_HD] for a in (q, k, v))
        m11 = m_ref[sd, h][:, :1]
        Dm = jnp.where(tri, cum_c - cum_r + li_r, NEG)
        m_inter = cum_c + m11
        mt = jnp.maximum(jnp.max(Dm, axis=-1, keepdims=True), m_inter)
        Wm = jnp.exp(Dm - mt) * _mm_nt(qh, kh)
        inter = jnp.exp(m_inter - mt)
        num = _mm(Wm, vh) + inter * _mm(qh, c_ref[sd, h])
        den = jnp.sum(Wm, axis=-1, keepdims=True) + inter * jnp.sum(qh * n_ref[sd, h], axis=-1, keepdims=True)
        hs.append(num / jnp.maximum(jnp.abs(den), jnp.exp(-mt)))
        total = cum_c[e:e + 1, :]
        gcol = total - cum_c + li_c
        m_new = jnp.maximum(total + m11, jnp.max(gcol, axis=0, keepdims=True))
        kw = kh * jnp.exp(gcol - m_new)
        decay = jnp.exp(total + m11 - m_new)
        c_ref[sd, h] = decay * c_ref[sd, h] + _mm_tn(kw, vh)
        n_ref[sd, h] = decay * n_ref[sd, h] + jnp.sum(kw, axis=0, keepdims=True)
        m_ref[sd, h] = jnp.broadcast_to(m_new, (1, LANES))
    h_ref[...] = jnp.concatenate(hs, axis=1)


def _mlstm_kernel(flags_ref, *refs, nb):
    fwd, bwd, (cw_ref, hf_ref, hb_ref, c_ref, n_ref, m_ref) = refs[:9], refs[9:18], refs[18:]
    j = pl.program_id(0)
    jb = nb - 1 - j
    _mlstm_dir(False, flags_ref[0, j], flags_ref[1, j], *fwd, cw_ref, c_ref, n_ref, m_ref, hf_ref)
    _mlstm_dir(True, flags_ref[0, jb], flags_ref[1, jb], *bwd, cw_ref, c_ref, n_ref, m_ref, hb_ref)


def _mlstm(proj, gct, conv_w, flags):
    T = proj.shape[0]
    nb = T // CHUNK
    W = C_H * C_HD
    r8 = CHUNK // 8

    def specs(blk):
        halo_p = lambda j, f: jnp.maximum(blk(j) * r8 - 1, 0)
        halo_n = lambda j, f: jnp.minimum(blk(j) * r8 + r8, nb * r8 - 1)
        out = []
        for u in (U_QC, U_KC):
            out += [pl.BlockSpec((CHUNK, W), lambda j, f, u=u: (blk(j), u // 4)),
                    pl.BlockSpec((8, W), lambda j, f, u=u: (halo_p(j, f), u // 4)),
                    pl.BlockSpec((8, W), lambda j, f, u=u: (halo_n(j, f), u // 4))]
        out += [pl.BlockSpec((CHUNK, W), lambda j, f: (blk(j), U_VC // 4)),
                pl.BlockSpec((CHUNK, LANES), lambda j, f: (blk(j), U_GC)),
                pl.BlockSpec((16, CHUNK), lambda j, f: (0, blk(j)))]
        return out

    fwd_blk = lambda j: j
    bwd_blk = lambda j: nb - 1 - j
    args = [proj] * 8 + [gct]
    return pl.pallas_call(
        functools.partial(_mlstm_kernel, nb=nb),
        grid_spec=pltpu.PrefetchScalarGridSpec(
            num_scalar_prefetch=1, grid=(nb,),
            in_specs=specs(fwd_blk) + specs(bwd_blk) + [pl.BlockSpec((3, 2 * W), lambda j, f: (0, 0))],
            out_specs=[pl.BlockSpec((CHUNK, W), lambda j, f: (j, 0)),
                       pl.BlockSpec((CHUNK, W), lambda j, f: (nb - 1 - j, 0))],
            scratch_shapes=[pltpu.VMEM((2, C_H, C_HD, C_HD), F32),
                            pltpu.VMEM((2, C_H, 1, C_HD), F32),
                            pltpu.VMEM((2, C_H, 1, LANES), F32)]),
        out_shape=[jax.ShapeDtypeStruct((T, W), F32), jax.ShapeDtypeStruct((T, W), F32)],
        compiler_params=pltpu.CompilerParams(dimension_semantics=("arbitrary",)),
        name="mlstm",
    )(flags, *args, *args, conv_w)


def _merge_kernel(x_ref, oa_ref, ob0_ref, ob1_ref, ob2_ref, l0_ref, l1_ref, l2_ref, hf_ref, hb_ref, oc_ref,
                  ga_ref, gb_ref, gc_ref, wa_ref, wb_ref, wc_ref, wo_ref, lg_ref, lb_ref, o_ref, *, alpha):
    ya = jnp.dot(oa_ref[...], wa_ref[...], preferred_element_type=F32)
    l0, l1, l2 = l0_ref[...], l1_ref[...], l2_ref[...]
    lm = jnp.maximum(jnp.maximum(l0, l1), l2)
    e0, e1, e2 = jnp.exp(l0 - lm), jnp.exp(l1 - lm), jnp.exp(l2 - lm)
    es = e0 + e1 + e2
    outb = jnp.concatenate([ob0_ref[...] * (e0 / es), ob1_ref[...] * (e1 / es), ob2_ref[...] * (e2 / es)], axis=1)
    yb = _mm(outb, wb_ref[...])
    yc = _mm(_sigmoid(oc_ref[...]) * (hf_ref[...] + hb_ref[...]), wc_ref[...])
    merged = _sigmoid(ga_ref[...]) * ya + _sigmoid(gb_ref[...]) * yb + _sigmoid(gc_ref[...]) * yc
    mix = _mm(merged, wo_ref[...])
    o_ref[...] = _layer_norm(alpha * x_ref[...] + mix, lg_ref[...], lb_ref[...])


def _merge(x, proj, oa, obs, lses, hf, hb, wa, wb, wc, wo, lg, lb, alpha, tm):
    T = x.shape[0]
    row = lambda w: pl.BlockSpec((tm, w), lambda i: (i, 0))
    full = lambda a: pl.BlockSpec(a.shape, lambda i: (0,) * a.ndim)
    return pl.pallas_call(
        functools.partial(_merge_kernel, alpha=alpha),
        grid=(T // tm,),
        in_specs=[row(D), row(oa.shape[1])] + [row(LANES)] * 6 + [row(hf.shape[1])] * 2
        + [pl.BlockSpec((tm, 512), lambda i: (i, U_OC // 4))]
        + [pl.BlockSpec((tm, D), lambda i, c=c: (i, U_GATES // 8 + c)) for c in range(3)]
        + [full(a) for a in (wa, wb, wc, wo, lg, lb)],
        out_specs=row(D),
        out_shape=jax.ShapeDtypeStruct((T, D), F32),
        compiler_params=pltpu.CompilerParams(dimension_semantics=("parallel",)),
        name="merge",
    )(x, oa, *obs, *lses, hf, hb, proj, proj, proj, proj, wa, wb, wc, wo, lg, lb)


def _stack_rows(rows, iota):
    out = jnp.zeros(iota.shape, F32)
    for k, r in enumerate(rows):
        out = jnp.where(iota == float(k), r, out)
    return out


def _route_kernel(x_ref, wrt_ref, bias_ref, idx_ref, w_ref, rank_ref, cnt_ref, run_ref, *, tm):
    @pl.when(pl.program_id(0) == 0)
    def _():
        run_ref[...] = jnp.zeros(run_ref.shape, F32)

    x = x_ref[...]
    xh, xm, xl = _split3(x)
    wh, wm, wl = _split3(wrt_ref[...])
    nt = lambda a, b: lax.dot_general(a, b, (((1,), (1,)), ((), ())), preferred_element_type=F32)
    logits = nt(wh, xh) + (nt(wh, xm) + nt(wm, xh)) + (nt(wh, xl) + nt(wl, xh) + nt(wm, xm))
    scores = _sigmoid(logits)
    biased = scores + bias_ref[...]
    ninf = -jnp.inf
    iota_g = lax.broadcasted_iota(I32, (EPG, tm), 0).astype(F32)
    gs = []
    for gi in range(NGRP):
        v = biased[gi * EPG:(gi + 1) * EPG]
        t1 = jnp.max(v, axis=0, keepdims=True)
        i1 = jnp.min(jnp.where(v == t1, iota_g, float(EPG)), axis=0, keepdims=True)
        t2 = jnp.max(jnp.where(iota_g == i1, ninf, v), axis=0, keepdims=True)
        gs.append(t1 + t2)
    iota8 = lax.broadcasted_iota(I32, (NGRP, tm), 0).astype(F32)
    gs = _stack_rows(gs, iota8)
    gsel = jnp.zeros((NGRP, tm), F32)
    for _ in range(TOPG):
        mx = jnp.max(gs, axis=0, keepdims=True)
        ix = jnp.min(jnp.where(gs == mx, iota8, float(NGRP)), axis=0, keepdims=True)
        hit = iota8 == ix
        gsel = jnp.where(hit, 1.0, gsel)
        gs = jnp.where(hit, ninf, gs)
    v = jnp.concatenate([jnp.where(gsel[gi:gi + 1] > 0.0, biased[gi * EPG:(gi + 1) * EPG], ninf)
                         for gi in range(NGRP)], axis=0)
    iota_e = lax.broadcasted_iota(I32, (NE, tm), 0).astype(F32)
    sel = jnp.zeros((NE, tm), F32)
    idxs, svals = [], []
    for _ in range(TOPK):
        mx = jnp.max(v, axis=0, keepdims=True)
        ix = jnp.min(jnp.where(v == mx, iota_e, float(NE)), axis=0, keepdims=True)
        hit = iota_e == ix
        idxs.append(ix)
        svals.append(jnp.sum(jnp.where(hit, scores, 0.0), axis=0, keepdims=True))
        sel = jnp.where(hit, 1.0, sel)
        v = jnp.where(hit, ninf, v)
    top_s = _stack_rows(svals, iota8)
    idx_ref[...] = _stack_rows(idxs, iota8).astype(I32)
    w_ref[...] = top_s / jnp.sum(top_s, axis=0, keepdims=True) * ROUTED_SCALE
    ti = lax.broadcasted_iota(I32, (tm, tm), 0)
    si = lax.broadcasted_iota(I32, (tm, tm), 1)
    before = jnp.where(ti < si, 1.0, 0.0).astype(BF16)
    selb = sel.astype(BF16)
    cnt = run_ref[...] + jnp.dot(selb, before, preferred_element_type=F32)
    rank_ref[...] = _stack_rows(
        [jnp.sum(jnp.where(iota_e == ix, cnt, 0.0), axis=0, keepdims=True) for ix in idxs], iota8).astype(I32)
    run = run_ref[...] + jnp.dot(selb, jnp.ones((tm, tm), BF16), preferred_element_type=F32)
    run_ref[...] = run
    cnt_ref[...] = run.astype(I32)


def _route(x1, wrt, bias_col, tm):
    T = x1.shape[0]
    slot = lambda dt: jax.ShapeDtypeStruct((TOPK, T), dt)
    return pl.pallas_call(
        functools.partial(_route_kernel, tm=tm),
        grid=(T // tm,),
        in_specs=[pl.BlockSpec((tm, D), lambda i: (i, 0)),
                  pl.BlockSpec((NE, D), lambda i: (0, 0)),
                  pl.BlockSpec((NE, 1), lambda i: (0, 0))],
        out_specs=[pl.BlockSpec((TOPK, tm), lambda i: (0, i))] * 3 + [pl.BlockSpec((NE, tm), lambda i: (0, 0))],
        out_shape=[slot(I32), slot(F32), slot(I32), jax.ShapeDtypeStruct((NE, tm), I32)],
        scratch_shapes=[pltpu.VMEM((NE, tm), F32)],
        compiler_params=pltpu.CompilerParams(dimension_semantics=("arbitrary",)),
        name="route",
    )(x1, wrt, bias_col)


def _dispatch_kernel(dest_hbm, x_ref, zeros_hbm, xs_hbm, dest_smem, sem_idx, sem, *, tm):
    del zeros_hbm
    i = pl.program_id(0)
    cp = pltpu.make_async_copy(dest_hbm.at[i], dest_smem, sem_idx)
    cp.start()
    cp.wait()

    def body(r, c):
        for k in range(TOPK):
            d = dest_smem[k * tm + r]
            pltpu.make_async_copy(x_ref.at[pl.ds(r, 1)], xs_hbm.at[pl.ds(d, 1)], sem).start()
        return c

    lax.fori_loop(0, tm, body, 0)
    for k in range(TOPK):
        pltpu.make_async_copy(x_ref, xs_hbm.at[pl.ds(0, tm)], sem).wait()


def _dispatch(dest_tiles, x1, P, tm):
    T = x1.shape[0]
    zeros = jnp.zeros((P, D), F32)
    return pl.pallas_call(
        functools.partial(_dispatch_kernel, tm=tm),
        grid=(T // tm,),
        in_specs=[pl.BlockSpec(memory_space=pl.ANY),
                  pl.BlockSpec((tm, D), lambda i: (i, 0)),
                  pl.BlockSpec(memory_space=pl.ANY)],
        out_specs=pl.BlockSpec(memory_space=pl.ANY),
        out_shape=jax.ShapeDtypeStruct((P, D), F32),
        scratch_shapes=[pltpu.SMEM((TOPK * tm,), I32), pltpu.SemaphoreType.DMA, pltpu.SemaphoreType.DMA],
        input_output_aliases={2: 0},
        compiler_params=pltpu.CompilerParams(dimension_semantics=("arbitrary",)),
        name="dispatch",
    )(dest_tiles, x1, zeros)


def _expert_kernel(blk_e_ref, n_used_ref, x_ref, wgu_ref, wdn_ref, y_ref):
    del blk_e_ref
    b = pl.program_id(0)

    @pl.when(b < n_used_ref[0])
    def _():
        gu = _mm(x_ref[...], wgu_ref[0])
        gate, up = gu[:, :DE], gu[:, DE:]
        y_ref[...] = _mm(gate * _sigmoid(gate) * up, wdn_ref[0])

    @pl.when(b >= n_used_ref[0])
    def _():
        y_ref[...] = jnp.zeros(y_ref.shape, F32)


def _experts(blk_e, n_used, xs, wgu, wdn, bm):
    P = xs.shape[0]
    return pl.pallas_call(
        _expert_kernel,
        grid_spec=pltpu.PrefetchScalarGridSpec(
            num_scalar_prefetch=2, grid=(P // bm,),
            in_specs=[pl.BlockSpec((bm, D), lambda b, e, n: (b, 0)),
                      pl.BlockSpec((1, D, 2 * DE), lambda b, e, n: (e[b], 0, 0)),
                      pl.BlockSpec((1, DE, D), lambda b, e, n: (e[b], 0, 0))],
            out_specs=pl.BlockSpec((bm, D), lambda b, e, n: (b, 0))),
        out_shape=jax.ShapeDtypeStruct((P, D), F32),
        compiler_params=pltpu.CompilerParams(dimension_semantics=("parallel",)),
        name="experts",
    )(blk_e, n_used, xs, wgu, wdn)


def _combine_kernel(dest_hbm, y_hbm, x_ref, w_ref, wsg_ref, wsd_ref, lg_ref, lb_ref, o_ref, dest_smem, rows_ref,
                    sem_idx, sem, *, tm, alpha):
    i = pl.program_id(0)
    cp = pltpu.make_async_copy(dest_hbm.at[i], dest_smem, sem_idx)
    cp.start()
    cp.wait()

    def body(r, c):
        for k in range(TOPK):
            d = dest_smem[k * tm + r]
            pltpu.make_async_copy(y_hbm.at[pl.ds(d, 1)], rows_ref.at[k, pl.ds(r, 1)], sem).start()
        return c

    lax.fori_loop(0, tm, body, 0)
    x = x_ref[...]
    gu = _mm(x, wsg_ref[...])
    gate, up = gu[:, :DE], gu[:, DE:]
    acc = _mm(gate * _sigmoid(gate) * up, wsd_ref[...])
    for k in range(TOPK):
        pltpu.make_async_copy(y_hbm.at[pl.ds(0, tm)], rows_ref.at[k], sem).wait()
    w = w_ref[...]
    routed = rows_ref[0] * w[:, 0:1]
    for k in range(1, TOPK):
        routed = routed + rows_ref[k] * w[:, k:k + 1]
    o_ref[...] = _layer_norm(alpha * x + (routed + acc), lg_ref[...], lb_ref[...])


def _combine(dest_tiles, ys, x1, w_tok, wsg, wsd, lg, lb, alpha, tm):
    T = x1.shape[0]
    full = lambda a: pl.BlockSpec(a.shape, lambda i: (0,) * a.ndim)
    return pl.pallas_call(
        functools.partial(_combine_kernel, tm=tm, alpha=alpha),
        grid=(T // tm,),
        in_specs=[pl.BlockSpec(memory_space=pl.ANY), pl.BlockSpec(memory_space=pl.ANY),
                  pl.BlockSpec((tm, D), lambda i: (i, 0)), pl.BlockSpec((tm, TOPK), lambda i: (i, 0))]
        + [full(a) for a in (wsg, wsd, lg, lb)],
        out_specs=pl.BlockSpec((tm, D), lambda i: (i, 0)),
        out_shape=jax.ShapeDtypeStruct((T, D), F32),
        scratch_shapes=[pltpu.SMEM((TOPK * tm,), I32), pltpu.VMEM((TOPK, tm, D), F32),
                        pltpu.SemaphoreType.DMA, pltpu.SemaphoreType.DMA],
        compiler_params=pltpu.CompilerParams(dimension_semantics=("arbitrary",)),
        name="combine",
    )(dest_tiles, ys, x1, w_tok, wsg, wsd, lg, lb)


def _regroup_w_in(w, b):
    offs = np.cumsum([0, 512, 128, 128, 384, 192, 192, 512, 512, 512, 512, 16, 3 * D])
    qa, ka, va, qb, kb, vb, qc, kc, vc, oc, gc, gates = (slice(offs[i], offs[i + 1]) for i in range(12))
    n_in = int(offs[-1])
    cols = np.full((NPAD,), n_in, np.int32)

    def place(unit, start, stop):
        cols[unit * LANES:unit * LANES + (stop - start)] = np.arange(start, stop)

    place(U_QA, qa.start, qa.stop)
    place(U_KA, ka.start, ka.stop)
    place(U_VA, va.start, va.stop)
    for gi in range(3):
        place(U_QB[gi], qb.start + gi * 128, qb.start + (gi + 1) * 128)
        cols[U_KVB[gi] * LANES:U_KVB[gi] * LANES + HD] = np.arange(kb.start + gi * HD, kb.start + (gi + 1) * HD)
        cols[U_KVB[gi] * LANES + HD:(U_KVB[gi] + 1) * LANES] = np.arange(vb.start + gi * HD, vb.start + (gi + 1) * HD)
    for unit, s in ((U_QC, qc), (U_KC, kc), (U_VC, vc), (U_OC, oc), (U_GC, gc), (U_GATES, gates)):
        place(unit, s.start, s.stop)
    wr = jnp.take(jnp.pad(w, ((0, 0), (0, 1))), cols, axis=1)
    br = jnp.take(jnp.pad(b, (0, 1)), cols, axis=0)
    return wr.astype(BF16), br.reshape(1, NPAD)


def _trunk(x, seq_lens, depth, params, *, bm, tm_proj, tn_proj, tm_merge, tm_route, tm_disp, tm_comb):
    (w_in, b_in, sink, conv_w, w_br_a, w_br_b, w_br_c, w_out, ln1_g, ln1_b, w_router, router_bias, w_exp_gu,
     w_exp_down, w_sh_gu, w_sh_down, ln2_g, ln2_b) = params
    T = x.shape[0]
    alpha = float((2 * depth) ** 0.25)
    slopes_a = tuple(float(2.0 ** (-8.0 * i / 8)) for i in range(1, 9))
    slopes_b = tuple(float(2.0 ** (-8.0 * i / 6)) for i in range(1, 7))
    flags1 = jnp.asarray(_edge_flags(seq_lens, QB))
    flags_b = [jnp.asarray(_edge_flags(seq_lens, QB * d)) for d in B_DILS]
    A = T * TOPK
    n_blocks = A // bm + NE
    P = n_blocks * bm
    row2 = lambda a: a.reshape(1, -1)
    for l in range(depth):
        w_all, b_all = _regroup_w_in(w_in[l], b_in[l])
        proj = _proj(x, w_all, b_all, tm_proj, tn_proj)
        oa = _attn_a(proj, flags1, sink[l].reshape(1, A_HK * A_G), slopes_a)
        obs, lses = [], []
        for gi in range(3):
            o, s = _attn_b(proj, flags_b[gi], gi, slopes_b[gi * B_G:(gi + 1) * B_G])
            obs.append(o)
            lses.append(s)
        gct = proj[:, U_GC * LANES:U_GC * LANES + 16].T
        hf, hb = _mlstm(proj, gct, conv_w[l], flags1)
        x1 = _merge(x, proj, oa, obs, lses, hf, hb, w_br_a[l].astype(BF16), w_br_b[l].astype(BF16),
                    w_br_c[l].astype(BF16), w_out[l].astype(BF16), row2(ln1_g[l]), row2(ln1_b[l]), alpha, tm_merge)
        top_idx, top_w, rank, counts = _route(x1, w_router[l].T, router_bias[l].reshape(NE, 1), tm_route)
        counts = counts[:, 0]
        padded = (counts + bm - 1) // bm * bm
        pad_end = jnp.cumsum(padded)
        pad_start = pad_end - padded
        dest = jnp.take(pad_start, top_idx, axis=0) + rank
        blk_e = jnp.minimum(jnp.searchsorted(pad_end, jnp.arange(n_blocks, dtype=I32) * bm, side='right'),
                            NE - 1).astype(I32)
        n_used = (pad_end[-1:] // bm).astype(I32)
        tiles = lambda tm: dest.reshape(TOPK, T // tm, tm).transpose(1, 0, 2).reshape(T // tm, TOPK * tm)
        xs = _dispatch(tiles(tm_disp), x1, P, tm_disp)
        ys = _experts(blk_e, n_used, xs, w_exp_gu[l].astype(BF16), w_exp_down[l].astype(BF16), bm)
        x = _combine(tiles(tm_comb), ys, x1, top_w.T, w_sh_gu[l].astype(BF16), w_sh_down[l].astype(BF16),
                     row2(ln2_g[l]), row2(ln2_b[l]), alpha, tm_comb)
    return x


def kernel(x_prompt, x_sample, w_in, b_in, sink, conv_w, w_br_a, w_br_b, w_br_c, w_out, ln1_g, ln1_b, w_router,
           router_bias, w_exp_gu, w_exp_down, w_sh_gu, w_sh_down, ln2_g, ln2_b):
    depth = w_in.shape[0]
    n1, s1, _ = x_prompt.shape
    n2, s2, _ = x_sample.shape
    span = QB * max(B_DILS)
    assert s1 % span == 0 and s2 % span == 0, "sequence lengths must be multiples of the widest dilated block"
    x = jnp.concatenate([x_prompt.reshape(n1 * s1, D), x_sample.reshape(n2 * s2, D)], axis=0)
    params = (w_in, b_in, sink, conv_w, w_br_a, w_br_b, w_br_c, w_out, ln1_g, ln1_b, w_router, router_bias,
              w_exp_gu, w_exp_down, w_sh_gu, w_sh_down, ln2_g, ln2_b)
    y = _trunk(x, [s1] * n1 + [s2] * n2, depth, params, bm=512, tm_proj=1024, tn_proj=512, tm_merge=256,
               tm_route=256, tm_disp=256, tm_comb=128)
    return y[:n1 * s1].reshape(n1, s1, D), y[n1 * s1:].reshape(n2, s2, D)
```

```python
import functools

import numpy as np
import jax
import jax.numpy as jnp
from jax import lax
from jax.experimental import pallas as pl
from jax.experimental.pallas import tpu as pltpu

F32, BF16, I32 = jnp.float32, jnp.bfloat16, jnp.int32

LANES = 128
D = 1024
HD = 64
A_HK, A_G, A_BAND = 2, 4, 128
B_G, B_BAND = 2, 64
B_DILS = (1, 4, 16)
C_H, C_HD, CHUNK = 4, 128, 128
NE, TOPK, NGRP, TOPG, DE = 256, 8, 8, 4, 256
EPG = NE // NGRP
ROUTED_SCALE = 2.5
LN_EPS = 1e-5
NEG = -1e30
QB = 128
TILE_ROWS = D // LANES

U_QA, U_KA, U_VA = 0, 4, 5
U_QB = (6, 8, 10)
U_KVB = (7, 9, 11)
U_QC, U_KC, U_VC, U_OC, U_GC, U_GATES = 12, 16, 20, 24, 28, 32
N_UNITS = 56
NPAD = N_UNITS * LANES


def _edge_flags(seq_lens, rows):
    first, last = [], []
    for s in seq_lens:
        nb = s // rows
        first += [1] + [0] * (nb - 1)
        last += [0] * (nb - 1) + [1]
    return np.array([first, last], np.int32)


def _mm(a, b):
    return jnp.dot(a.astype(BF16), b.astype(BF16), preferred_element_type=F32)


def _mm_nt(a, b):
    return lax.dot_general(a.astype(BF16), b.astype(BF16), (((1,), (1,)), ((), ())), preferred_element_type=F32)


def _mm_tn(a, b):
    return lax.dot_general(a.astype(BF16), b.astype(BF16), (((0,), (0,)), ((), ())), preferred_element_type=F32)


def _split3(x):
    hi = x.astype(BF16)
    r1 = x - hi.astype(F32)
    mid = r1.astype(BF16)
    lo = (r1 - mid.astype(F32)).astype(BF16)
    return hi, mid, lo


def _sigmoid(x):
    return 1.0 / (1.0 + jnp.exp(-x))


def _log_sigmoid(x):
    return jnp.minimum(x, 0.0) - jnp.log(1.0 + jnp.exp(-jnp.abs(x)))


def _to_token_tiles(ref, x):
    n = x.shape[0]
    for j in range(D // LANES):
        ref[pl.ds(j, n, stride=D // LANES), :] = x[:, j * LANES:(j + 1) * LANES]


def _from_token_tiles(ref):
    n = ref.shape[0] // (D // LANES)
    return jnp.concatenate([ref[pl.ds(j, n, stride=D // LANES), :] for j in range(D // LANES)], axis=1)


def _layer_norm(x, g, b):
    mu = jnp.mean(x, axis=-1, keepdims=True)
    xc = x - mu
    var = jnp.mean(xc * xc, axis=-1, keepdims=True)
    return xc * lax.rsqrt(var + LN_EPS) * g + b


def _proj_kernel(x_ref, w_ref, b_ref, o_ref, xb_ref):
    @pl.when(pl.program_id(1) == 0)
    def _():
        xb_ref[...] = x_ref[...].astype(BF16)

    o_ref[...] = jnp.dot(xb_ref[...], w_ref[...], preferred_element_type=F32) + b_ref[...]


def _proj(x, w, b, tm, tn):
    T, N = x.shape[0], w.shape[1]
    return pl.pallas_call(
        _proj_kernel,
        grid=(T // tm, N // tn),
        in_specs=[pl.BlockSpec((tm, D), lambda i, j: (i, 0)),
                  pl.BlockSpec((D, tn), lambda i, j: (0, j)),
                  pl.BlockSpec((1, tn), lambda i, j: (0, j))],
        out_specs=pl.BlockSpec((tm, tn), lambda i, j: (i, j)),
        out_shape=jax.ShapeDtypeStruct((T, N), F32),
        scratch_shapes=[pltpu.VMEM((tm, D), BF16)],
        compiler_params=pltpu.CompilerParams(dimension_semantics=("parallel", "arbitrary")),
        name="proj",
    )(x, w, b)


def _attn_tile(q, ks, vs, lo, hi, sinks, *, hk, g, band, dist, slopes, packed):
    rows = g * QB
    row = lax.broadcasted_iota(I32, (rows, 3 * QB), 0)
    col = lax.broadcasted_iota(I32, (rows, 3 * QB), 1)
    rel = jnp.abs(col - QB - (row & (QB - 1)))
    mask = (rel <= band) & (col >= lo) & (col < hi)
    relf = rel.astype(F32) * float(dist)
    row_g = jnp.right_shift(lax.broadcasted_iota(I32, (rows, 1), 0), QB.bit_length() - 1)
    outs, lses = [], []
    for h in range(hk):
        qs = jnp.concatenate([q[:, (h * g + gg) * HD:(h * g + gg + 1) * HD] for gg in range(g)], axis=0)
        qs = qs * (HD ** -0.5)
        ko = h * HD
        vo = HD if packed else h * HD
        kh = jnp.concatenate([a[:, ko:ko + HD] for a in ks], axis=0)
        vh = jnp.concatenate([a[:, vo:vo + HD] for a in vs], axis=0)
        s = _mm_nt(qs, kh)
        slope_col = jnp.zeros((rows, 1), F32)
        for gg in range(g):
            slope_col = jnp.where(row_g == gg, float(slopes[h * g + gg]), slope_col)
        logits = jnp.where(mask, s - slope_col * relf, NEG)
        m = jnp.max(logits, axis=-1, keepdims=True)
        if sinks is not None:
            sink_col = jnp.zeros((rows, 1), F32)
            for gg in range(g):
                sink_col = jnp.where(row_g == gg, sinks[h * g + gg], sink_col)
            m = jnp.maximum(m, sink_col)
        p = jnp.exp(logits - m)
        den = jnp.sum(p, axis=-1, keepdims=True)
        if sinks is not None:
            den = den + jnp.exp(sink_col - m)
        o = _mm(p, vh) / den
        for gg in range(g):
            outs.append(o[gg * QB:(gg + 1) * QB])
        if packed:
            l = m + jnp.log(den)
            for gg in range(g):
                lses.append(jnp.broadcast_to(l[gg * QB:(gg + 1) * QB], (QB, HD)))
    return outs, lses


def _attn_a_kernel(flags_ref, sink_ref, q_ref, kp_ref, ko_ref, kn_ref, vp_ref, vo_ref, vn_ref, o_ref, *, slopes):
    b = pl.program_id(0)
    lo = jnp.where(flags_ref[0, b] == 0, 0, QB)
    hi = jnp.where(flags_ref[1, b] == 0, 3 * QB, 2 * QB)
    sinks = [sink_ref[0, i] for i in range(A_HK * A_G)]
    outs, _ = _attn_tile(q_ref[...], [kp_ref[...], ko_ref[...], kn_ref[...]], [vp_ref[...], vo_ref[...], vn_ref[...]],
                         lo, hi, sinks, hk=A_HK, g=A_G, band=A_BAND, dist=1, slopes=slopes, packed=False)
    o_ref[...] = jnp.concatenate(outs, axis=1).astype(o_ref.dtype)


def _attn_a(proj, flags, sink, slopes):
    T = proj.shape[0]
    nb = T // QB
    prev = lambda b, f: jnp.maximum(b - 1, 0)
    nxt = lambda b, f: jnp.minimum(b + 1, nb - 1)
    kspecs = lambda u: [pl.BlockSpec((QB, LANES), lambda b, f: (prev(b, f), u)),
                        pl.BlockSpec((QB, LANES), lambda b, f: (b, u)),
                        pl.BlockSpec((QB, LANES), lambda b, f: (nxt(b, f), u))]
    return pl.pallas_call(
        functools.partial(_attn_a_kernel, slopes=slopes),
        grid_spec=pltpu.PrefetchScalarGridSpec(
            num_scalar_prefetch=1, grid=(nb,),
            in_specs=[pl.BlockSpec(memory_space=pltpu.SMEM),
                      pl.BlockSpec((QB, A_HK * A_G * HD), lambda b, f: (b, U_QA // 4))]
            + kspecs(U_KA) + kspecs(U_VA),
            out_specs=pl.BlockSpec((QB, A_HK * A_G * HD), lambda b, f: (b, 0))),
        out_shape=jax.ShapeDtypeStruct((T, A_HK * A_G * HD), BF16),
        compiler_params=pltpu.CompilerParams(dimension_semantics=("parallel",)),
        name="attn_a",
    )(flags, sink, proj, proj, proj, proj, proj, proj, proj)


def _attn_b_kernel(flags_ref, q_ref, kp_ref, ko_ref, kn_ref, o_ref, lse_ref, *, dil, slopes):
    b = pl.program_id(0)
    lo = jnp.where(flags_ref[0, b] == 0, 0, QB)
    hi = jnp.where(flags_ref[1, b] == 0, 3 * QB, 2 * QB)

    def one(r):
        sub = lambda ref: ref[...] if dil == 1 else ref[pl.ds(r, QB, stride=dil), :]
        kv = [sub(kp_ref), sub(ko_ref), sub(kn_ref)]
        outs, lses = _attn_tile(sub(q_ref), kv, kv, lo, hi, None, hk=1, g=B_G, band=B_BAND, dist=dil, slopes=slopes,
                                packed=True)
        o, l = jnp.concatenate(outs, axis=1), jnp.concatenate(lses, axis=1)
        if dil == 1:
            o_ref[...] = o
            lse_ref[...] = l
        else:
            o_ref[pl.ds(r, QB, stride=dil), :] = o
            lse_ref[pl.ds(r, QB, stride=dil), :] = l

    if dil == 1:
        one(0)
    else:
        def body(r, c):
            one(r)
            return c
        lax.fori_loop(0, dil, body, 0)


def _attn_b(proj, flags, gi, slopes):
    dil = B_DILS[gi]
    T = proj.shape[0]
    rows = QB * dil
    nb = T // rows
    uq, ukv = U_QB[gi], U_KVB[gi]
    return pl.pallas_call(
        functools.partial(_attn_b_kernel, dil=dil, slopes=slopes),
        grid_spec=pltpu.PrefetchScalarGridSpec(
            num_scalar_prefetch=1, grid=(nb,),
            in_specs=[pl.BlockSpec((rows, LANES), lambda b, f: (b, uq)),
                      pl.BlockSpec((rows, LANES), lambda b, f: (jnp.maximum(b - 1, 0), ukv)),
                      pl.BlockSpec((rows, LANES), lambda b, f: (b, ukv)),
                      pl.BlockSpec((rows, LANES), lambda b, f: (jnp.minimum(b + 1, nb - 1), ukv))],
            out_specs=[pl.BlockSpec((rows, LANES), lambda b, f: (b, 0)),
                       pl.BlockSpec((rows, LANES), lambda b, f: (b, 0))]),
        out_shape=[jax.ShapeDtypeStruct((T, LANES), F32), jax.ShapeDtypeStruct((T, LANES), F32)],
        compiler_params=pltpu.CompilerParams(dimension_semantics=("parallel",)),
        name=f"attn_b{gi}",
    )(flags, proj, proj, proj, proj)


def _conv_silu(x, xp8, xn8, w, keep_prev, keep_next):
    rowi = lax.broadcasted_iota(I32, (CHUNK, 1), 0)
    x_prev = jnp.where(rowi == 0, xp8[7:8, :] * keep_prev, pltpu.roll(x, 1, 0))
    x_next = jnp.where(rowi == CHUNK - 1, xn8[0:1, :] * keep_next, pltpu.roll(x, CHUNK - 1, 0))
    y = w[0:1, :] * x_prev + w[1:2, :] * x + w[2:3, :] * x_next
    return y * _sigmoid(y)


def _mlstm_dir(rev, first, last, q_ref, qp_ref, qn_ref, k_ref, kp_ref, kn_ref, v_ref, g_ref, gt_ref, cw_ref,
               c_ref, n_ref, m_ref, h_ref):
    sd = 1 if rev else 0
    keep_prev = (1 - first).astype(F32)
    keep_next = (1 - last).astype(F32)

    @pl.when((last if rev else first) == 1)
    def _():
        c_ref[sd] = jnp.zeros(c_ref.shape[1:], F32)
        n_ref[sd] = jnp.zeros(n_ref.shape[1:], F32)
        m_ref[sd] = jnp.zeros(m_ref.shape[1:], F32)

    W = C_H * C_HD
    q = _conv_silu(q_ref[...], qp_ref[...], qn_ref[...], cw_ref[:, :W], keep_prev, keep_next)
    k = _conv_silu(k_ref[...], kp_ref[...], kn_ref[...], cw_ref[:, W:], keep_prev, keep_next) * (C_HD ** -0.5)
    v = v_ref[...]
    G = g_ref[...]
    GT = gt_ref[...]
    ti = lax.broadcasted_iota(I32, (CHUNK, CHUNK), 0)
    si = lax.broadcasted_iota(I32, (CHUNK, CHUNK), 1)
    tri = (si >= ti) if rev else (si <= ti)
    L = jnp.where(tri, 1.0, 0.0).astype(BF16)
    Lt = jnp.where((ti >= si) if rev else (ti <= si), 1.0, 0.0).astype(BF16)
    cum_col = sum(jnp.dot(L, p, preferred_element_type=F32) for p in _split3(_log_sigmoid(G)))
    cum_row = sum(jnp.dot(p, Lt, preferred_element_type=F32) for p in _split3(_log_sigmoid(GT)))
    e = 0 if rev else CHUNK - 1
    hs = []
    for h in range(C_H):
        ci = (2 * C_H if rev else 0) + h
        cf = ci + C_H
        cum_c, cum_r = cum_col[:, cf:cf + 1], cum_row[cf:cf + 1, :]
        li_c, li_r = G[:, ci:ci + 1], GT[ci:ci + 1, :]
        qh, kh, vh = (a[:, h * C_HD:(h + 1) * C_HD] for a in (q, k, v))
        m11 = m_ref[sd, h][:, :1]
        Dm = jnp.where(tri, cum_c - cum_r + li_r, NEG)
        m_inter = cum_c + m11
        mt = jnp.maximum(jnp.max(Dm, axis=-1, keepdims=True), m_inter)
        Wm = jnp.exp(Dm - mt) * _mm_nt(qh, kh)
        inter = jnp.exp(m_inter - mt)
        num = _mm(Wm, vh) + inter * _mm(qh, c_ref[sd, h])
        den = jnp.sum(Wm, axis=-1, keepdims=True) + inter * jnp.sum(qh * n_ref[sd, h], axis=-1, keepdims=True)
        hs.append(num / jnp.maximum(jnp.abs(den), jnp.exp(-mt)))
        total = cum_c[e:e + 1, :]
        gcol = total - cum_c + li_c
        m_new = jnp.maximum(total + m11, jnp.max(gcol, axis=0, keepdims=True))
        kw = kh * jnp.exp(gcol - m_new)
        decay = jnp.exp(total + m11 - m_new)
        c_ref[sd, h] = decay * c_ref[sd, h] + _mm_tn(kw, vh)
        n_ref[sd, h] = decay * n_ref[sd, h] + jnp.sum(kw, axis=0, keepdims=True)
        m_ref[sd, h] = jnp.broadcast_to(m_new, (1, LANES))
    h_ref[...] = jnp.concatenate(hs, axis=1)


def _mlstm_kernel(flags_ref, *refs, nb):
    fwd, bwd, (cw_ref, hf_ref, hb_ref, c_ref, n_ref, m_ref) = refs[:9], refs[9:18], refs[18:]
    j = pl.program_id(0)
    jb = nb - 1 - j
    _mlstm_dir(False, flags_ref[0, j], flags_ref[1, j], *fwd, cw_ref, c_ref, n_ref, m_ref, hf_ref)
    _mlstm_dir(True, flags_ref[0, jb], flags_ref[1, jb], *bwd, cw_ref, c_ref, n_ref, m_ref, hb_ref)


def _mlstm(proj, gct, conv_w, flags):
    T = proj.shape[0]
    nb = T // CHUNK
    W = C_H * C_HD
    r8 = CHUNK // 8

    def specs(blk):
        halo_p = lambda j, f: jnp.maximum(blk(j) * r8 - 1, 0)
        halo_n = lambda j, f: jnp.minimum(blk(j) * r8 + r8, nb * r8 - 1)
        out = []
        for u in (U_QC, U_KC):
            out += [pl.BlockSpec((CHUNK, W), lambda j, f, u=u: (blk(j), u // 4)),
                    pl.BlockSpec((8, W), lambda j, f, u=u: (halo_p(j, f), u // 4)),
                    pl.BlockSpec((8, W), lambda j, f, u=u: (halo_n(j, f), u // 4))]
        out += [pl.BlockSpec((CHUNK, W), lambda j, f: (blk(j), U_VC // 4)),
                pl.BlockSpec((CHUNK, LANES), lambda j, f: (blk(j), U_GC)),
                pl.BlockSpec((16, CHUNK), lambda j, f: (0, blk(j)))]
        return out

    fwd_blk = lambda j: j
    bwd_blk = lambda j: nb - 1 - j
    args = [proj] * 8 + [gct]
    return pl.pallas_call(
        functools.partial(_mlstm_kernel, nb=nb),
        grid_spec=pltpu.PrefetchScalarGridSpec(
            num_scalar_prefetch=1, grid=(nb,),
            in_specs=specs(fwd_blk) + specs(bwd_blk) + [pl.BlockSpec((3, 2 * W), lambda j, f: (0, 0))],
            out_specs=[pl.BlockSpec((CHUNK, W), lambda j, f: (j, 0)),
                       pl.BlockSpec((CHUNK, W), lambda j, f: (nb - 1 - j, 0))],
            scratch_shapes=[pltpu.VMEM((2, C_H, C_HD, C_HD), F32),
                            pltpu.VMEM((2, C_H, 1, C_HD), F32),
                            pltpu.VMEM((2, C_H, 1, LANES), F32)]),
        out_shape=[jax.ShapeDtypeStruct((T, W), F32), jax.ShapeDtypeStruct((T, W), F32)],
        compiler_params=pltpu.CompilerParams(dimension_semantics=("arbitrary",)),
        name="mlstm",
    )(flags, *args, *args, conv_w)


def _merge_kernel(x_ref, oa_ref, ob0_ref, ob1_ref, ob2_ref, l0_ref, l1_ref, l2_ref, hf_ref, hb_ref, oc_ref,
                  ga_ref, gb_ref, gc_ref, wa_ref, wb_ref, wc_ref, wo_ref, lg_ref, lb_ref, o_ref, ot_ref, *, alpha):
    ya = jnp.dot(oa_ref[...], wa_ref[...], preferred_element_type=F32)
    l0, l1, l2 = l0_ref[...], l1_ref[...], l2_ref[...]
    lm = jnp.maximum(jnp.maximum(l0, l1), l2)
    e0, e1, e2 = jnp.exp(l0 - lm), jnp.exp(l1 - lm), jnp.exp(l2 - lm)
    es = e0 + e1 + e2
    outb = jnp.concatenate([ob0_ref[...] * (e0 / es), ob1_ref[...] * (e1 / es), ob2_ref[...] * (e2 / es)], axis=1)
    yb = _mm(outb, wb_ref[...])
    yc = _mm(_sigmoid(oc_ref[...]) * (hf_ref[...] + hb_ref[...]), wc_ref[...])
    merged = _sigmoid(ga_ref[...]) * ya + _sigmoid(gb_ref[...]) * yb + _sigmoid(gc_ref[...]) * yc
    mix = _mm(merged, wo_ref[...])
    x1 = _layer_norm(alpha * x_ref[...] + mix, lg_ref[...], lb_ref[...])
    o_ref[...] = x1
    _to_token_tiles(ot_ref, x1)


def _merge(x, proj, oa, obs, lses, hf, hb, wa, wb, wc, wo, lg, lb, alpha, tm):
    T = x.shape[0]
    row = lambda w: pl.BlockSpec((tm, w), lambda i: (i, 0))
    full = lambda a: pl.BlockSpec(a.shape, lambda i: (0,) * a.ndim)
    return pl.pallas_call(
        functools.partial(_merge_kernel, alpha=alpha),
        grid=(T // tm,),
        in_specs=[row(D), row(oa.shape[1])] + [row(LANES)] * 6 + [row(hf.shape[1])] * 2
        + [pl.BlockSpec((tm, 512), lambda i: (i, U_OC // 4))]
        + [pl.BlockSpec((tm, D), lambda i, c=c: (i, U_GATES // 8 + c)) for c in range(3)]
        + [full(a) for a in (wa, wb, wc, wo, lg, lb)],
        out_specs=[row(D), pl.BlockSpec((tm * TILE_ROWS, LANES), lambda i: (i, 0))],
        out_shape=[jax.ShapeDtypeStruct((T, D), F32), jax.ShapeDtypeStruct((T * TILE_ROWS, LANES), F32)],
        compiler_params=pltpu.CompilerParams(dimension_semantics=("parallel",)),
        name="merge",
    )(x, oa, *obs, *lses, hf, hb, proj, proj, proj, proj, wa, wb, wc, wo, lg, lb)


def _stack_rows(rows, iota):
    out = jnp.zeros(iota.shape, F32)
    for k, r in enumerate(rows):
        out = jnp.where(iota == float(k), r, out)
    return out


def _route_kernel(x_ref, wrt_ref, bias_ref, idx_ref, w_ref, rank_ref, cnt_ref, run_ref, *, tm):
    @pl.when(pl.program_id(0) == 0)
    def _():
        run_ref[...] = jnp.zeros(run_ref.shape, F32)

    x = x_ref[...]
    xh, xm, xl = _split3(x)
    wh, wm, wl = _split3(wrt_ref[...])
    nt = lambda a, b: lax.dot_general(a, b, (((1,), (1,)), ((), ())), preferred_element_type=F32)
    logits = nt(wh, xh) + (nt(wh, xm) + nt(wm, xh)) + (nt(wh, xl) + nt(wl, xh) + nt(wm, xm))
    scores = _sigmoid(logits)
    biased = scores + bias_ref[...]
    ninf = -jnp.inf
    iota_g = lax.broadcasted_iota(I32, (EPG, tm), 0).astype(F32)
    gs = []
    for gi in range(NGRP):
        v = biased[gi * EPG:(gi + 1) * EPG]
        t1 = jnp.max(v, axis=0, keepdims=True)
        i1 = jnp.min(jnp.where(v == t1, iota_g, float(EPG)), axis=0, keepdims=True)
        t2 = jnp.max(jnp.where(iota_g == i1, ninf, v), axis=0, keepdims=True)
        gs.append(t1 + t2)
    iota8 = lax.broadcasted_iota(I32, (NGRP, tm), 0).astype(F32)
    gs = _stack_rows(gs, iota8)
    gsel = jnp.zeros((NGRP, tm), F32)
    for _ in range(TOPG):
        mx = jnp.max(gs, axis=0, keepdims=True)
        ix = jnp.min(jnp.where(gs == mx, iota8, float(NGRP)), axis=0, keepdims=True)
        hit = iota8 == ix
        gsel = jnp.where(hit, 1.0, gsel)
        gs = jnp.where(hit, ninf, gs)
    v = jnp.concatenate([jnp.where(gsel[gi:gi + 1] > 0.0, biased[gi * EPG:(gi + 1) * EPG], ninf)
                         for gi in range(NGRP)], axis=0)
    iota_e = lax.broadcasted_iota(I32, (NE, tm), 0).astype(F32)
    sel = jnp.zeros((NE, tm), F32)
    idxs, svals = [], []
    for _ in range(TOPK):
        mx = jnp.max(v, axis=0, keepdims=True)
        ix = jnp.min(jnp.where(v == mx, iota_e, float(NE)), axis=0, keepdims=True)
        hit = iota_e == ix
        idxs.append(ix)
        svals.append(jnp.sum(jnp.where(hit, scores, 0.0), axis=0, keepdims=True))
        sel = jnp.where(hit, 1.0, sel)
        v = jnp.where(hit, ninf, v)
    top_s = _stack_rows(svals, iota8)
    idx_ref[...] = _stack_rows(idxs, iota8).astype(I32)
    w_ref[...] = top_s / jnp.sum(top_s, axis=0, keepdims=True) * ROUTED_SCALE
    ti = lax.broadcasted_iota(I32, (tm, tm), 0)
    si = lax.broadcasted_iota(I32, (tm, tm), 1)
    before = jnp.where(ti < si, 1.0, 0.0).astype(BF16)
    selb = sel.astype(BF16)
    cnt = run_ref[...] + jnp.dot(selb, before, preferred_element_type=F32)
    rank_ref[...] = _stack_rows(
        [jnp.sum(jnp.where(iota_e == ix, cnt, 0.0), axis=0, keepdims=True) for ix in idxs], iota8).astype(I32)
    run = run_ref[...] + jnp.dot(selb, jnp.ones((tm, tm), BF16), preferred_element_type=F32)
    run_ref[...] = run
    cnt_ref[...] = run.astype(I32)


def _route(x1, wrt, bias_col, tm):
    T = x1.shape[0]
    slot = lambda dt: jax.ShapeDtypeStruct((TOPK, T), dt)
    return pl.pallas_call(
        functools.partial(_route_kernel, tm=tm),
        grid=(T // tm,),
        in_specs=[pl.BlockSpec((tm, D), lambda i: (i, 0)),
                  pl.BlockSpec((NE, D), lambda i: (0, 0)),
                  pl.BlockSpec((NE, 1), lambda i: (0, 0))],
        out_specs=[pl.BlockSpec((TOPK, tm), lambda i: (0, i))] * 3 + [pl.BlockSpec((NE, tm), lambda i: (0, 0))],
        out_shape=[slot(I32), slot(F32), slot(I32), jax.ShapeDtypeStruct((NE, tm), I32)],
        scratch_shapes=[pltpu.VMEM((NE, tm), F32)],
        compiler_params=pltpu.CompilerParams(dimension_semantics=("arbitrary",)),
        name="route",
    )(x1, wrt, bias_col)


def _tile_rows(ref, slot, n=1):
    return ref.at[pl.ds(pl.multiple_of(slot * TILE_ROWS, TILE_ROWS), n * TILE_ROWS)]


def _dispatch_kernel(cnt_ref, start_ref, dest_hbm, x_ref, xs_hbm, dest_smem, zero_ref, sem_idx, sem, sem_z, *, tm, bm):
    i = pl.program_id(0)
    pad_sizes = [1 << k for k in range(bm.bit_length() - 1)]
    half = bm // 2

    def pad_copies(e, fn):
        n = (-cnt_ref[e]) & (bm - 1)
        at = start_ref[e] + cnt_ref[e]
        for sz in pad_sizes:
            @pl.when((n & sz) != 0)
            def _(at=at, sz=sz):
                fn(pltpu.make_async_copy(_tile_rows(zero_ref, 0, sz), _tile_rows(xs_hbm, at, sz), sem_z))
            at = at + (n & sz)

    @pl.when(i == 0)
    def _():
        zero_ref[...] = jnp.zeros(zero_ref.shape, F32)

        def start(e, c):
            pad_copies(e, lambda cp: cp.start())
            return c

        def wait(e, c):
            pad_copies(e, lambda cp: cp.wait())
            return c

        lax.fori_loop(0, NE, start, 0)
        lax.fori_loop(0, NE, wait, 0)

        used = start_ref[NE - 1] + cnt_ref[NE - 1] + ((-cnt_ref[NE - 1]) & (bm - 1))
        tail = lambda c: pltpu.make_async_copy(zero_ref, _tile_rows(xs_hbm, used + c * half, half), sem_z)

        def tail_start(c, z):
            tail(c).start()
            return z

        def tail_wait(c, z):
            tail(c).wait()
            return z

        n_tail = (xs_hbm.shape[0] // TILE_ROWS - used) // half
        lax.fori_loop(0, n_tail, tail_start, 0)
        lax.fori_loop(0, n_tail, tail_wait, 0)

    cp = pltpu.make_async_copy(dest_hbm.at[i], dest_smem, sem_idx)
    cp.start()
    cp.wait()

    def body(r, c):
        for k in range(TOPK):
            pltpu.make_async_copy(_tile_rows(x_ref, r), _tile_rows(xs_hbm, dest_smem[k * tm + r]), sem).start()
        return c

    lax.fori_loop(0, tm, body, 0)
    for k in range(TOPK):
        pltpu.make_async_copy(x_ref, _tile_rows(xs_hbm, 0, tm), sem).wait()


def _dispatch(counts, pad_start, dest_tiles, x1t, P, tm, bm):
    T = x1t.shape[0] // TILE_ROWS
    return pl.pallas_call(
        functools.partial(_dispatch_kernel, tm=tm, bm=bm),
        grid_spec=pltpu.PrefetchScalarGridSpec(
            num_scalar_prefetch=2, grid=(T // tm,),
            in_specs=[pl.BlockSpec(memory_space=pl.ANY),
                      pl.BlockSpec((tm * TILE_ROWS, LANES), lambda i, c, s: (i, 0))],
            out_specs=pl.BlockSpec(memory_space=pl.ANY),
            scratch_shapes=[pltpu.SMEM((TOPK * tm,), I32), pltpu.VMEM((bm // 2 * TILE_ROWS, LANES), F32),
                            pltpu.SemaphoreType.DMA, pltpu.SemaphoreType.DMA, pltpu.SemaphoreType.DMA]),
        out_shape=jax.ShapeDtypeStruct((P * TILE_ROWS, LANES), F32),
        compiler_params=pltpu.CompilerParams(dimension_semantics=("arbitrary",)),
        name="dispatch",
    )(counts, pad_start, dest_tiles, x1t)


def _expert_kernel(blk_e_ref, n_used_ref, x_ref, wgu_ref, wdn_ref, y_ref):
    del blk_e_ref
    b = pl.program_id(0)

    @pl.when(b < n_used_ref[0])
    def _():
        gu = _mm(_from_token_tiles(x_ref), wgu_ref[0])
        gate, up = gu[:, :DE], gu[:, DE:]
        _to_token_tiles(y_ref, _mm(gate * _sigmoid(gate) * up, wdn_ref[0]))

    @pl.when(b >= n_used_ref[0])
    def _():
        y_ref[...] = jnp.zeros(y_ref.shape, F32)


def _experts(blk_e, n_used, xs, wgu, wdn, bm):
    n_blocks = xs.shape[0] // (bm * TILE_ROWS)
    live = lambda b, n: jnp.minimum(b, n[0] - 1)
    return pl.pallas_call(
        _expert_kernel,
        grid_spec=pltpu.PrefetchScalarGridSpec(
            num_scalar_prefetch=2, grid=(n_blocks,),
            in_specs=[pl.BlockSpec((bm * TILE_ROWS, LANES), lambda b, e, n: (live(b, n), 0)),
                      pl.BlockSpec((1, D, 2 * DE), lambda b, e, n: (e[live(b, n)], 0, 0)),
                      pl.BlockSpec((1, DE, D), lambda b, e, n: (e[live(b, n)], 0, 0))],
            out_specs=pl.BlockSpec((bm * TILE_ROWS, LANES), lambda b, e, n: (b, 0))),
        out_shape=jax.ShapeDtypeStruct(xs.shape, F32),
        compiler_params=pltpu.CompilerParams(dimension_semantics=("parallel",)),
        name="experts",
    )(blk_e, n_used, xs, wgu, wdn)


def _combine_kernel(dest_hbm, y_hbm, x_ref, w_ref, wsg_ref, wsd_ref, lg_ref, lb_ref, o_ref, dest_smem, rows_ref,
                    sem_idx, sem, *, tm, alpha):
    i = pl.program_id(0)
    cp = pltpu.make_async_copy(dest_hbm.at[i], dest_smem, sem_idx)
    cp.start()
    cp.wait()

    def body(r, c):
        for k in range(TOPK):
            pltpu.make_async_copy(_tile_rows(y_hbm, dest_smem[k * tm + r]), _tile_rows(rows_ref.at[k], r), sem).start()
        return c

    lax.fori_loop(0, tm, body, 0)
    x = x_ref[...]
    gu = _mm(x, wsg_ref[...])
    gate, up = gu[:, :DE], gu[:, DE:]
    shared = _mm(gate * _sigmoid(gate) * up, wsd_ref[...])
    for k in range(TOPK):
        pltpu.make_async_copy(_tile_rows(y_hbm, 0, tm), rows_ref.at[k], sem).wait()
    w = w_ref[...]
    routed = _from_token_tiles(rows_ref.at[0]) * w[:, 0:1]
    for k in range(1, TOPK):
        routed = routed + _from_token_tiles(rows_ref.at[k]) * w[:, k:k + 1]
    o_ref[...] = _layer_norm(alpha * x + (routed + shared), lg_ref[...], lb_ref[...])


def _combine(dest_tiles, ys, x1, w_tok, wsg, wsd, lg, lb, alpha, tm):
    T = x1.shape[0]
    full = lambda a: pl.BlockSpec(a.shape, lambda i: (0,) * a.ndim)
    return pl.pallas_call(
        functools.partial(_combine_kernel, tm=tm, alpha=alpha),
        grid=(T // tm,),
        in_specs=[pl.BlockSpec(memory_space=pl.ANY), pl.BlockSpec(memory_space=pl.ANY),
                  pl.BlockSpec((tm, D), lambda i: (i, 0)), pl.BlockSpec((tm, TOPK), lambda i: (i, 0))]
        + [full(a) for a in (wsg, wsd, lg, lb)],
        out_specs=pl.BlockSpec((tm, D), lambda i: (i, 0)),
        out_shape=jax.ShapeDtypeStruct((T, D), F32),
        scratch_shapes=[pltpu.SMEM((TOPK * tm,), I32), pltpu.VMEM((TOPK, tm * TILE_ROWS, LANES), F32),
                        pltpu.SemaphoreType.DMA, pltpu.SemaphoreType.DMA],
        compiler_params=pltpu.CompilerParams(dimension_semantics=("arbitrary",)),
        name="combine",
    )(dest_tiles, ys, x1, w_tok, wsg, wsd, lg, lb)


def _regroup_w_in(w, b):
    offs = np.cumsum([0, 512, 128, 128, 384, 192, 192, 512, 512, 512, 512, 16, 3 * D])
    qa, ka, va, qb, kb, vb, qc, kc, vc, oc, gc, gates = (int(o) for o in offs[:-1])
    pieces = [(qa, ka), (ka, va), (va, qb)]
    for gi in range(3):
        pieces += [(qb + gi * 128, qb + (gi + 1) * 128), (kb + gi * HD, kb + (gi + 1) * HD),
                   (vb + gi * HD, vb + (gi + 1) * HD)]
    pieces += [(qc, kc), (kc, vc), (vc, oc), (oc, gc), (gc, gates), None, (gates, int(offs[-1]))]
    gc_pad = (U_GATES - U_GC) * LANES - (gates - gc)

    def build(a):
        parts = [jnp.zeros(a.shape[:-1] + (gc_pad,), a.dtype) if p is None else a[..., p[0]:p[1]] for p in pieces]
        return jnp.concatenate(parts, axis=-1)

    return build(w).astype(BF16), build(b).reshape(1, NPAD)


def _trunk(x, seq_lens, depth, params, *, bm, tm_proj, tn_proj, tm_merge, tm_route, tm_disp, tm_comb):
    (w_in, b_in, sink, conv_w, w_br_a, w_br_b, w_br_c, w_out, ln1_g, ln1_b, w_router, router_bias, w_exp_gu,
     w_exp_down, w_sh_gu, w_sh_down, ln2_g, ln2_b) = params
    T = x.shape[0]
    alpha = float((2 * depth) ** 0.25)
    slopes_a = tuple(float(2.0 ** (-8.0 * i / 8)) for i in range(1, 9))
    slopes_b = tuple(float(2.0 ** (-8.0 * i / 6)) for i in range(1, 7))
    flags1 = jnp.asarray(_edge_flags(seq_lens, QB))
    flags_b = [jnp.asarray(_edge_flags(seq_lens, QB * d)) for d in B_DILS]
    A = T * TOPK
    n_blocks = A // bm + NE
    P = n_blocks * bm
    row2 = lambda a: a.reshape(1, -1)
    for l in range(depth):
        w_all, b_all = _regroup_w_in(w_in[l], b_in[l])
        proj = _proj(x, w_all, b_all, tm_proj, tn_proj)
        oa = _attn_a(proj, flags1, sink[l].reshape(1, A_HK * A_G), slopes_a)
        obs, lses = [], []
        for gi in range(3):
            o, s = _attn_b(proj, flags_b[gi], gi, slopes_b[gi * B_G:(gi + 1) * B_G])
            obs.append(o)
            lses.append(s)
        gct = proj[:, U_GC * LANES:U_GC * LANES + 16].T
        hf, hb = _mlstm(proj, gct, conv_w[l], flags1)
        x1, x1t = _merge(x, proj, oa, obs, lses, hf, hb, w_br_a[l].astype(BF16), w_br_b[l].astype(BF16),
                    w_br_c[l].astype(BF16), w_out[l].astype(BF16), row2(ln1_g[l]), row2(ln1_b[l]), alpha, tm_merge)
        top_idx, top_w, rank, counts = _route(x1, w_router[l].T, router_bias[l].reshape(NE, 1), tm_route)
        counts = counts[:, 0]
        padded = (counts + bm - 1) // bm * bm
        pad_end = jnp.cumsum(padded)
        pad_start = pad_end - padded
        dest = jnp.take(pad_start, top_idx, axis=0) + rank
        blk_e = jnp.minimum(jnp.searchsorted(pad_end, jnp.arange(n_blocks, dtype=I32) * bm, side='right'),
                            NE - 1).astype(I32)
        n_used = (pad_end[-1:] // bm).astype(I32)
        tiles = lambda tm: dest.reshape(TOPK, T // tm, tm).transpose(1, 0, 2).reshape(T // tm, TOPK * tm)
        xs = _dispatch(counts, pad_start, tiles(tm_disp), x1t, P, tm_disp, bm)
        ys = _experts(blk_e, n_used, xs, w_exp_gu[l].astype(BF16), w_exp_down[l].astype(BF16), bm)
        x = _combine(tiles(tm_comb), ys, x1, top_w.T, w_sh_gu[l].astype(BF16), w_sh_down[l].astype(BF16),
                     row2(ln2_g[l]), row2(ln2_b[l]), alpha, tm_comb)
    return x


def kernel(x_prompt, x_sample, w_in, b_in, sink, conv_w, w_br_a, w_br_b, w_br_c, w_out, ln1_g, ln1_b, w_router,
           router_bias, w_exp_gu, w_exp_down, w_sh_gu, w_sh_down, ln2_g, ln2_b):
    depth = w_in.shape[0]
    n1, s1, _ = x_prompt.shape
    n2, s2, _ = x_sample.shape
    span = QB * max(B_DILS)
    assert s1 % span == 0 and s2 % span == 0, "sequence lengths must be multiples of the widest dilated block"
    x = jnp.concatenate([x_prompt.reshape(n1 * s1, D), x_sample.reshape(n2 * s2, D)], axis=0)
    params = (w_in, b_in, sink, conv_w, w_br_a, w_br_b, w_br_c, w_out, ln1_g, ln1_b, w_router, router_bias,
              w_exp_gu, w_exp_down, w_sh_gu, w_sh_down, ln2_g, ln2_b)
    y = _trunk(x, [s1] * n1 + [s2] * n2, depth, params, bm=512, tm_proj=1024, tn_proj=512, tm_merge=256,
               tm_route=256, tm_disp=512, tm_comb=256)
    return y[:n1 * s1].reshape(n1, s1, D), y[n1 * s1:].reshape(n2, s2, D)
```

```python
import functools

import numpy as np
import jax
import jax.numpy as jnp
from jax import lax
from jax.experimental import pallas as pl
from jax.experimental.pallas import tpu as pltpu

F32, BF16, I32 = jnp.float32, jnp.bfloat16, jnp.int32

LANES = 128
D = 1024
HD = 64
A_HK, A_G, A_BAND = 2, 4, 128
B_G, B_BAND = 2, 64
B_DILS = (1, 4, 16)
C_H, C_HD, CHUNK = 4, 128, 128
NE, TOPK, NGRP, TOPG, DE = 256, 8, 8, 4, 256
EPG = NE // NGRP
ROUTED_SCALE = 2.5
LN_EPS = 1e-5
NEG = -1e30
QB = 128
TILE_ROWS = D // LANES

U_QA, U_KA, U_VA = 0, 4, 5
U_QB = (6, 8, 10)
U_KVB = (7, 9, 11)
U_QC, U_KC, U_VC, U_OC, U_GC, U_GATES = 12, 16, 20, 24, 28, 32
N_UNITS = 56
NPAD = N_UNITS * LANES


def _edge_flags(seq_lens, rows):
    first, last = [], []
    for s in seq_lens:
        nb = s // rows
        first += [1] + [0] * (nb - 1)
        last += [0] * (nb - 1) + [1]
    return np.array([first, last], np.int32)


def _mm(a, b):
    return jnp.dot(a.astype(BF16), b.astype(BF16), preferred_element_type=F32)


def _mm_nt(a, b):
    return lax.dot_general(a.astype(BF16), b.astype(BF16), (((1,), (1,)), ((), ())), preferred_element_type=F32)


def _mm_tn(a, b):
    return lax.dot_general(a.astype(BF16), b.astype(BF16), (((0,), (0,)), ((), ())), preferred_element_type=F32)


def _split3(x):
    hi = x.astype(BF16)
    r1 = x - hi.astype(F32)
    mid = r1.astype(BF16)
    lo = (r1 - mid.astype(F32)).astype(BF16)
    return hi, mid, lo


def _sigmoid(x):
    return 1.0 / (1.0 + jnp.exp(-x))


def _log_sigmoid(x):
    return jnp.minimum(x, 0.0) - jnp.log(1.0 + jnp.exp(-jnp.abs(x)))


def _to_token_tiles(ref, x):
    n = x.shape[0]
    for j in range(D // LANES):
        ref[pl.ds(j, n, stride=D // LANES), :] = x[:, j * LANES:(j + 1) * LANES]


def _from_token_tiles(ref):
    n = ref.shape[0] // (D // LANES)
    return jnp.concatenate([ref[pl.ds(j, n, stride=D // LANES), :] for j in range(D // LANES)], axis=1)


def _layer_norm(x, g, b):
    mu = jnp.mean(x, axis=-1, keepdims=True)
    xc = x - mu
    var = jnp.mean(xc * xc, axis=-1, keepdims=True)
    return xc * lax.rsqrt(var + LN_EPS) * g + b


def _proj_kernel(x_ref, w_ref, b_ref, o_ref, xb_ref):
    @pl.when(pl.program_id(1) == 0)
    def _():
        xb_ref[...] = x_ref[...].astype(BF16)

    o_ref[...] = jnp.dot(xb_ref[...], w_ref[...], preferred_element_type=F32) + b_ref[...]


def _proj(x, w, b, tm, tn):
    T, N = x.shape[0], w.shape[1]
    return pl.pallas_call(
        _proj_kernel,
        grid=(T // tm, N // tn),
        in_specs=[pl.BlockSpec((tm, D), lambda i, j: (i, 0)),
                  pl.BlockSpec((D, tn), lambda i, j: (0, j)),
                  pl.BlockSpec((1, tn), lambda i, j: (0, j))],
        out_specs=pl.BlockSpec((tm, tn), lambda i, j: (i, j)),
        out_shape=jax.ShapeDtypeStruct((T, N), F32),
        scratch_shapes=[pltpu.VMEM((tm, D), BF16)],
        compiler_params=pltpu.CompilerParams(dimension_semantics=("parallel", "arbitrary")),
        name="proj",
    )(x, w, b)


def _attn_tile(q, ks, vs, lo, hi, sinks, *, hk, g, band, dist, slopes, packed):
    rows = g * QB
    row = lax.broadcasted_iota(I32, (rows, 3 * QB), 0)
    col = lax.broadcasted_iota(I32, (rows, 3 * QB), 1)
    rel = jnp.abs(col - QB - (row & (QB - 1)))
    mask = (rel <= band) & (col >= lo) & (col < hi)
    relf = rel.astype(F32) * float(dist)
    row_g = jnp.right_shift(lax.broadcasted_iota(I32, (rows, 1), 0), QB.bit_length() - 1)
    outs, lses = [], []
    for h in range(hk):
        qs = jnp.concatenate([q[:, (h * g + gg) * HD:(h * g + gg + 1) * HD] for gg in range(g)], axis=0)
        qs = qs * (HD ** -0.5)
        ko = h * HD
        vo = HD if packed else h * HD
        kh = jnp.concatenate([a[:, ko:ko + HD] for a in ks], axis=0)
        vh = jnp.concatenate([a[:, vo:vo + HD] for a in vs], axis=0)
        s = _mm_nt(qs, kh)
        slope_col = jnp.zeros((rows, 1), F32)
        for gg in range(g):
            slope_col = jnp.where(row_g == gg, float(slopes[h * g + gg]), slope_col)
        logits = jnp.where(mask, s - slope_col * relf, NEG)
        m = jnp.max(logits, axis=-1, keepdims=True)
        if sinks is not None:
            sink_col = jnp.zeros((rows, 1), F32)
            for gg in range(g):
                sink_col = jnp.where(row_g == gg, sinks[h * g + gg], sink_col)
            m = jnp.maximum(m, sink_col)
        p = jnp.exp(logits - m)
        den = jnp.sum(p, axis=-1, keepdims=True)
        if sinks is not None:
            den = den + jnp.exp(sink_col - m)
        o = _mm(p, vh) / den
        for gg in range(g):
            outs.append(o[gg * QB:(gg + 1) * QB])
        if packed:
            l = m + jnp.log(den)
            for gg in range(g):
                lses.append(jnp.broadcast_to(l[gg * QB:(gg + 1) * QB], (QB, HD)))
    return outs, lses


def _attn_a_kernel(flags_ref, sink_ref, q_ref, kp_ref, ko_ref, kn_ref, vp_ref, vo_ref, vn_ref, o_ref, *, slopes):
    b = pl.program_id(0)
    lo = jnp.where(flags_ref[0, b] == 0, 0, QB)
    hi = jnp.where(flags_ref[1, b] == 0, 3 * QB, 2 * QB)
    sinks = [sink_ref[0, i] for i in range(A_HK * A_G)]
    outs, _ = _attn_tile(q_ref[...], [kp_ref[...], ko_ref[...], kn_ref[...]], [vp_ref[...], vo_ref[...], vn_ref[...]],
                         lo, hi, sinks, hk=A_HK, g=A_G, band=A_BAND, dist=1, slopes=slopes, packed=False)
    o_ref[...] = jnp.concatenate(outs, axis=1).astype(o_ref.dtype)


def _attn_a(proj, flags, sink, slopes):
    T = proj.shape[0]
    nb = T // QB
    prev = lambda b, f: jnp.maximum(b - 1, 0)
    nxt = lambda b, f: jnp.minimum(b + 1, nb - 1)
    kspecs = lambda u: [pl.BlockSpec((QB, LANES), lambda b, f: (prev(b, f), u)),
                        pl.BlockSpec((QB, LANES), lambda b, f: (b, u)),
                        pl.BlockSpec((QB, LANES), lambda b, f: (nxt(b, f), u))]
    return pl.pallas_call(
        functools.partial(_attn_a_kernel, slopes=slopes),
        grid_spec=pltpu.PrefetchScalarGridSpec(
            num_scalar_prefetch=1, grid=(nb,),
            in_specs=[pl.BlockSpec(memory_space=pltpu.SMEM),
                      pl.BlockSpec((QB, A_HK * A_G * HD), lambda b, f: (b, U_QA // 4))]
            + kspecs(U_KA) + kspecs(U_VA),
            out_specs=pl.BlockSpec((QB, A_HK * A_G * HD), lambda b, f: (b, 0))),
        out_shape=jax.ShapeDtypeStruct((T, A_HK * A_G * HD), BF16),
        compiler_params=pltpu.CompilerParams(dimension_semantics=("parallel",)),
        name="attn_a",
    )(flags, sink, proj, proj, proj, proj, proj, proj, proj)


def _attn_b_kernel(flags_ref, q_ref, kp_ref, ko_ref, kn_ref, o_ref, lse_ref, *, dil, slopes):
    b = pl.program_id(0)
    lo = jnp.where(flags_ref[0, b] == 0, 0, QB)
    hi = jnp.where(flags_ref[1, b] == 0, 3 * QB, 2 * QB)

    def one(r):
        sub = lambda ref: ref[...] if dil == 1 else ref[pl.ds(r, QB, stride=dil), :]
        kv = [sub(kp_ref), sub(ko_ref), sub(kn_ref)]
        outs, lses = _attn_tile(sub(q_ref), kv, kv, lo, hi, None, hk=1, g=B_G, band=B_BAND, dist=dil, slopes=slopes,
                                packed=True)
        o, l = jnp.concatenate(outs, axis=1), jnp.concatenate(lses, axis=1)
        if dil == 1:
            o_ref[...] = o
            lse_ref[...] = l
        else:
            o_ref[pl.ds(r, QB, stride=dil), :] = o
            lse_ref[pl.ds(r, QB, stride=dil), :] = l

    if dil == 1:
        one(0)
    else:
        def body(r, c):
            one(r)
            return c
        lax.fori_loop(0, dil, body, 0)


def _attn_b(proj, flags, gi, slopes):
    dil = B_DILS[gi]
    T = proj.shape[0]
    rows = QB * dil
    nb = T // rows
    uq, ukv = U_QB[gi], U_KVB[gi]
    return pl.pallas_call(
        functools.partial(_attn_b_kernel, dil=dil, slopes=slopes),
        grid_spec=pltpu.PrefetchScalarGridSpec(
            num_scalar_prefetch=1, grid=(nb,),
            in_specs=[pl.BlockSpec((rows, LANES), lambda b, f: (b, uq)),
                      pl.BlockSpec((rows, LANES), lambda b, f: (jnp.maximum(b - 1, 0), ukv)),
                      pl.BlockSpec((rows, LANES), lambda b, f: (b, ukv)),
                      pl.BlockSpec((rows, LANES), lambda b, f: (jnp.minimum(b + 1, nb - 1), ukv))],
            out_specs=[pl.BlockSpec((rows, LANES), lambda b, f: (b, 0)),
                       pl.BlockSpec((rows, LANES), lambda b, f: (b, 0))]),
        out_shape=[jax.ShapeDtypeStruct((T, LANES), F32), jax.ShapeDtypeStruct((T, LANES), F32)],
        compiler_params=pltpu.CompilerParams(dimension_semantics=("parallel",)),
        name=f"attn_b{gi}",
    )(flags, proj, proj, proj, proj)


def _conv_silu(x, xp8, xn8, w, keep_prev, keep_next):
    rowi = lax.broadcasted_iota(I32, (CHUNK, 1), 0)
    x_prev = jnp.where(rowi == 0, xp8[7:8, :] * keep_prev, pltpu.roll(x, 1, 0))
    x_next = jnp.where(rowi == CHUNK - 1, xn8[0:1, :] * keep_next, pltpu.roll(x, CHUNK - 1, 0))
    y = w[0:1, :] * x_prev + w[1:2, :] * x + w[2:3, :] * x_next
    return y * _sigmoid(y)


def _mlstm_dir(rev, first, last, q_ref, qp_ref, qn_ref, k_ref, kp_ref, kn_ref, v_ref, g_ref, gt_ref, cw_ref,
               c_ref, n_ref, m_ref, h_ref):
    sd = 1 if rev else 0
    keep_prev = (1 - first).astype(F32)
    keep_next = (1 - last).astype(F32)

    @pl.when((last if rev else first) == 1)
    def _():
        c_ref[sd] = jnp.zeros(c_ref.shape[1:], F32)
        n_ref[sd] = jnp.zeros(n_ref.shape[1:], F32)
        m_ref[sd] = jnp.zeros(m_ref.shape[1:], F32)

    W = C_H * C_HD
    q = _conv_silu(q_ref[...], qp_ref[...], qn_ref[...], cw_ref[:, :W], keep_prev, keep_next)
    k = _conv_silu(k_ref[...], kp_ref[...], kn_ref[...], cw_ref[:, W:], keep_prev, keep_next) * (C_HD ** -0.5)
    v = v_ref[...]
    G = g_ref[...]
    GT = gt_ref[...]
    ti = lax.broadcasted_iota(I32, (CHUNK, CHUNK), 0)
    si = lax.broadcasted_iota(I32, (CHUNK, CHUNK), 1)
    tri = (si >= ti) if rev else (si <= ti)
    L = jnp.where(tri, 1.0, 0.0).astype(BF16)
    Lt = jnp.where((ti >= si) if rev else (ti <= si), 1.0, 0.0).astype(BF16)
    cum_col = sum(jnp.dot(L, p, preferred_element_type=F32) for p in _split3(_log_sigmoid(G)))
    cum_row = sum(jnp.dot(p, Lt, preferred_element_type=F32) for p in _split3(_log_sigmoid(GT)))
    e = 0 if rev else CHUNK - 1
    hs = []
    for h in range(C_H):
        ci = (2 * C_H if rev else 0) + h
        cf = ci + C_H
        cum_c, cum_r = cum_col[:, cf:cf + 1], cum_row[cf:cf + 1, :]
        li_c, li_r = G[:, ci:ci + 1], GT[ci:ci + 1, :]
        qh, kh, vh = (a[:, h * C_HD:(h + 1) * C_HD] for a in (q, k, v))
        m11 = m_ref[sd, h][:, :1]
        Dm = jnp.where(tri, cum_c - cum_r + li_r, NEG)
        m_inter = cum_c + m11
        mt = jnp.maximum(jnp.max(Dm, axis=-1, keepdims=True), m_inter)
        Wm = jnp.exp(Dm - mt) * _mm_nt(qh, kh)
        inter = jnp.exp(m_inter - mt)
        num = _mm(Wm, vh) + inter * _mm(qh, c_ref[sd, h])
        den = jnp.sum(Wm, axis=-1, keepdims=True) + inter * jnp.sum(qh * n_ref[sd, h], axis=-1, keepdims=True)
        hs.append(num / jnp.maximum(jnp.abs(den), jnp.exp(-mt)))
        total = cum_c[e:e + 1, :]
        gcol = total - cum_c + li_c
        m_new = jnp.maximum(total + m11, jnp.max(gcol, axis=0, keepdims=True))
        kw = kh * jnp.exp(gcol - m_new)
        decay = jnp.exp(total + m11 - m_new)
        c_ref[sd, h] = decay * c_ref[sd, h] + _mm_tn(kw, vh)
        n_ref[sd, h] = decay * n_ref[sd, h] + jnp.sum(kw, axis=0, keepdims=True)
        m_ref[sd, h] = jnp.broadcast_to(m_new, (1, LANES))
    h_ref[...] = jnp.concatenate(hs, axis=1)


def _mlstm_kernel(flags_ref, *refs, nb):
    fwd, bwd, (cw_ref, hf_ref, hb_ref, c_ref, n_ref, m_ref) = refs[:9], refs[9:18], refs[18:]
    j = pl.program_id(0)
    jb = nb - 1 - j
    _mlstm_dir(False, flags_ref[0, j], flags_ref[1, j], *fwd, cw_ref, c_ref, n_ref, m_ref, hf_ref)
    _mlstm_dir(True, flags_ref[0, jb], flags_ref[1, jb], *bwd, cw_ref, c_ref, n_ref, m_ref, hb_ref)


def _mlstm(proj, gct, conv_w, flags):
    T = proj.shape[0]
    nb = T // CHUNK
    W = C_H * C_HD
    r8 = CHUNK // 8

    def specs(blk):
        halo_p = lambda j, f: jnp.maximum(blk(j) * r8 - 1, 0)
        halo_n = lambda j, f: jnp.minimum(blk(j) * r8 + r8, nb * r8 - 1)
        out = []
        for u in (U_QC, U_KC):
            out += [pl.BlockSpec((CHUNK, W), lambda j, f, u=u: (blk(j), u // 4)),
                    pl.BlockSpec((8, W), lambda j, f, u=u: (halo_p(j, f), u // 4)),
                    pl.BlockSpec((8, W), lambda j, f, u=u: (halo_n(j, f), u // 4))]
        out += [pl.BlockSpec((CHUNK, W), lambda j, f: (blk(j), U_VC // 4)),
                pl.BlockSpec((CHUNK, LANES), lambda j, f: (blk(j), U_GC)),
                pl.BlockSpec((16, CHUNK), lambda j, f: (0, blk(j)))]
        return out

    fwd_blk = lambda j: j
    bwd_blk = lambda j: nb - 1 - j
    args = [proj] * 8 + [gct]
    return pl.pallas_call(
        functools.partial(_mlstm_kernel, nb=nb),
        grid_spec=pltpu.PrefetchScalarGridSpec(
            num_scalar_prefetch=1, grid=(nb,),
            in_specs=specs(fwd_blk) + specs(bwd_blk) + [pl.BlockSpec((3, 2 * W), lambda j, f: (0, 0))],
            out_specs=[pl.BlockSpec((CHUNK, W), lambda j, f: (j, 0)),
                       pl.BlockSpec((CHUNK, W), lambda j, f: (nb - 1 - j, 0))],
            scratch_shapes=[pltpu.VMEM((2, C_H, C_HD, C_HD), F32),
                            pltpu.VMEM((2, C_H, 1, C_HD), F32),
                            pltpu.VMEM((2, C_H, 1, LANES), F32)]),
        out_shape=[jax.ShapeDtypeStruct((T, W), F32), jax.ShapeDtypeStruct((T, W), F32)],
        compiler_params=pltpu.CompilerParams(dimension_semantics=("arbitrary",)),
        name="mlstm",
    )(flags, *args, *args, conv_w)


def _merge_kernel(x_ref, oa_ref, ob0_ref, ob1_ref, ob2_ref, l0_ref, l1_ref, l2_ref, hf_ref, hb_ref, oc_ref,
                  ga_ref, gb_ref, gc_ref, wa_ref, wb_ref, wc_ref, wo_ref, lg_ref, lb_ref, o_ref, ot_ref, *, alpha):
    ya = jnp.dot(oa_ref[...], wa_ref[...], preferred_element_type=F32)
    l0, l1, l2 = l0_ref[...], l1_ref[...], l2_ref[...]
    lm = jnp.maximum(jnp.maximum(l0, l1), l2)
    e0, e1, e2 = jnp.exp(l0 - lm), jnp.exp(l1 - lm), jnp.exp(l2 - lm)
    es = e0 + e1 + e2
    outb = jnp.concatenate([ob0_ref[...] * (e0 / es), ob1_ref[...] * (e1 / es), ob2_ref[...] * (e2 / es)], axis=1)
    yb = _mm(outb, wb_ref[...])
    yc = _mm(_sigmoid(oc_ref[...]) * (hf_ref[...] + hb_ref[...]), wc_ref[...])
    merged = _sigmoid(ga_ref[...]) * ya + _sigmoid(gb_ref[...]) * yb + _sigmoid(gc_ref[...]) * yc
    mix = _mm(merged, wo_ref[...])
    x1 = _layer_norm(alpha * x_ref[...] + mix, lg_ref[...], lb_ref[...])
    o_ref[...] = x1
    _to_token_tiles(ot_ref, x1)


def _merge(x, proj, oa, obs, lses, hf, hb, wa, wb, wc, wo, lg, lb, alpha, tm):
    T = x.shape[0]
    row = lambda w: pl.BlockSpec((tm, w), lambda i: (i, 0))
    full = lambda a: pl.BlockSpec(a.shape, lambda i: (0,) * a.ndim)
    return pl.pallas_call(
        functools.partial(_merge_kernel, alpha=alpha),
        grid=(T // tm,),
        in_specs=[row(D), row(oa.shape[1])] + [row(LANES)] * 6 + [row(hf.shape[1])] * 2
        + [pl.BlockSpec((tm, 512), lambda i: (i, U_OC // 4))]
        + [pl.BlockSpec((tm, D), lambda i, c=c: (i, U_GATES // 8 + c)) for c in range(3)]
        + [full(a) for a in (wa, wb, wc, wo, lg, lb)],
        out_specs=[row(D), pl.BlockSpec((tm * TILE_ROWS, LANES), lambda i: (i, 0))],
        out_shape=[jax.ShapeDtypeStruct((T, D), F32), jax.ShapeDtypeStruct((T * TILE_ROWS, LANES), F32)],
        compiler_params=pltpu.CompilerParams(dimension_semantics=("parallel",)),
        name="merge",
    )(x, oa, *obs, *lses, hf, hb, proj, proj, proj, proj, wa, wb, wc, wo, lg, lb)


def _stack_rows(rows, iota):
    out = jnp.zeros(iota.shape, F32)
    for k, r in enumerate(rows):
        out = jnp.where(iota == float(k), r, out)
    return out


def _route_kernel(x_ref, wrt_ref, bias_ref, idx_ref, w_ref, rank_ref, cnt_ref, run_ref, *, tm):
    @pl.when(pl.program_id(0) == 0)
    def _():
        run_ref[...] = jnp.zeros(run_ref.shape, F32)

    x = x_ref[...]
    xh, xm, xl = _split3(x)
    wh, wm, wl = _split3(wrt_ref[...])
    nt = lambda a, b: lax.dot_general(a, b, (((1,), (1,)), ((), ())), preferred_element_type=F32)
    logits = nt(wh, xh) + (nt(wh, xm) + nt(wm, xh)) + (nt(wh, xl) + nt(wl, xh) + nt(wm, xm))
    scores = _sigmoid(logits)
    biased = scores + bias_ref[...]
    ninf = -jnp.inf
    iota_g = lax.broadcasted_iota(I32, (EPG, tm), 0).astype(F32)
    gs = []
    for gi in range(NGRP):
        v = biased[gi * EPG:(gi + 1) * EPG]
        t1 = jnp.max(v, axis=0, keepdims=True)
        i1 = jnp.min(jnp.where(v == t1, iota_g, float(EPG)), axis=0, keepdims=True)
        t2 = jnp.max(jnp.where(iota_g == i1, ninf, v), axis=0, keepdims=True)
        gs.append(t1 + t2)
    iota8 = lax.broadcasted_iota(I32, (NGRP, tm), 0).astype(F32)
    gs = _stack_rows(gs, iota8)
    gsel = jnp.zeros((NGRP, tm), F32)
    for _ in range(TOPG):
        mx = jnp.max(gs, axis=0, keepdims=True)
        ix = jnp.min(jnp.where(gs == mx, iota8, float(NGRP)), axis=0, keepdims=True)
        hit = iota8 == ix
        gsel = jnp.where(hit, 1.0, gsel)
        gs = jnp.where(hit, ninf, gs)
    v = jnp.concatenate([jnp.where(gsel[gi:gi + 1] > 0.0, biased[gi * EPG:(gi + 1) * EPG], ninf)
                         for gi in range(NGRP)], axis=0)
    iota_e = lax.broadcasted_iota(I32, (NE, tm), 0).astype(F32)
    sel = jnp.zeros((NE, tm), F32)
    idxs, svals = [], []
    for _ in range(TOPK):
        mx = jnp.max(v, axis=0, keepdims=True)
        ix = jnp.min(jnp.where(v == mx, iota_e, float(NE)), axis=0, keepdims=True)
        hit = iota_e == ix
        idxs.append(ix)
        svals.append(jnp.sum(jnp.where(hit, scores, 0.0), axis=0, keepdims=True))
        sel = jnp.where(hit, 1.0, sel)
        v = jnp.where(hit, ninf, v)
    top_s = _stack_rows(svals, iota8)
    idx_ref[...] = _stack_rows(idxs, iota8).astype(I32)
    w_ref[...] = top_s / jnp.sum(top_s, axis=0, keepdims=True) * ROUTED_SCALE
    ti = lax.broadcasted_iota(I32, (tm, tm), 0)
    si = lax.broadcasted_iota(I32, (tm, tm), 1)
    before = jnp.where(ti < si, 1.0, 0.0).astype(BF16)
    selb = sel.astype(BF16)
    cnt = run_ref[...] + jnp.dot(selb, before, preferred_element_type=F32)
    rank_ref[...] = _stack_rows(
        [jnp.sum(jnp.where(iota_e == ix, cnt, 0.0), axis=0, keepdims=True) for ix in idxs], iota8).astype(I32)
    run = run_ref[...] + jnp.dot(selb, jnp.ones((tm, tm), BF16), preferred_element_type=F32)
    run_ref[...] = run
    cnt_ref[...] = run.astype(I32)


def _route(x1, wrt, bias_col, tm):
    T = x1.shape[0]
    slot = lambda dt: jax.ShapeDtypeStruct((TOPK, T), dt)
    return pl.pallas_call(
        functools.partial(_route_kernel, tm=tm),
        grid=(T // tm,),
        in_specs=[pl.BlockSpec((tm, D), lambda i: (i, 0)),
                  pl.BlockSpec((NE, D), lambda i: (0, 0)),
                  pl.BlockSpec((NE, 1), lambda i: (0, 0))],
        out_specs=[pl.BlockSpec((TOPK, tm), lambda i: (0, i))] * 3 + [pl.BlockSpec((NE, tm), lambda i: (0, 0))],
        out_shape=[slot(I32), slot(F32), slot(I32), jax.ShapeDtypeStruct((NE, tm), I32)],
        scratch_shapes=[pltpu.VMEM((NE, tm), F32)],
        compiler_params=pltpu.CompilerParams(dimension_semantics=("arbitrary",)),
        name="route",
    )(x1, wrt, bias_col)


def _dest_kernel(idx_ref, rank_ref, start_ref, o_ref, *, tm):
    iota_e = lax.broadcasted_iota(I32, (NE, tm), 0)
    starts = start_ref[...]
    rows = []
    for k in range(TOPK):
        base = jnp.sum(jnp.where(iota_e == idx_ref[k:k + 1, :], starts, 0.0), axis=0, keepdims=True)
        rows.append(base.astype(I32) + rank_ref[k:k + 1, :])
    o_ref[0] = jnp.concatenate(rows, axis=1)


def _dest_tiles(top_idx, rank, pad_start, tm):
    T = top_idx.shape[1]
    return pl.pallas_call(
        functools.partial(_dest_kernel, tm=tm),
        grid=(T // tm,),
        in_specs=[pl.BlockSpec((TOPK, tm), lambda i: (0, i)), pl.BlockSpec((TOPK, tm), lambda i: (0, i)),
                  pl.BlockSpec((NE, 1), lambda i: (0, 0))],
        out_specs=pl.BlockSpec((1, 1, TOPK * tm), lambda i: (i, 0, 0)),
        out_shape=jax.ShapeDtypeStruct((T // tm, 1, TOPK * tm), I32),
        compiler_params=pltpu.CompilerParams(dimension_semantics=("parallel",)),
        name="dest",
    )(top_idx, rank, pad_start.astype(F32).reshape(NE, 1))


def _tile_rows(ref, slot, n=1):
    return ref.at[pl.ds(pl.multiple_of(slot * TILE_ROWS, TILE_ROWS), n * TILE_ROWS)]


def _dispatch_kernel(cnt_ref, start_ref, dest_hbm, x_ref, xs_hbm, dest_smem, zero_ref, sem_idx, sem, sem_z, *, tm, bm):
    i = pl.program_id(0)
    pad_sizes = [1 << k for k in range(bm.bit_length() - 1)]
    half = bm // 2

    def pad_copies(e, fn):
        n = (-cnt_ref[e]) & (bm - 1)
        at = start_ref[e] + cnt_ref[e]
        for sz in pad_sizes:
            @pl.when((n & sz) != 0)
            def _(at=at, sz=sz):
                fn(pltpu.make_async_copy(_tile_rows(zero_ref, 0, sz), _tile_rows(xs_hbm, at, sz), sem_z))
            at = at + (n & sz)

    @pl.when(i == 0)
    def _():
        zero_ref[...] = jnp.zeros(zero_ref.shape, F32)

        def start(e, c):
            pad_copies(e, lambda cp: cp.start())
            return c

        def wait(e, c):
            pad_copies(e, lambda cp: cp.wait())
            return c

        lax.fori_loop(0, NE, start, 0)
        lax.fori_loop(0, NE, wait, 0)

        used = start_ref[NE - 1] + cnt_ref[NE - 1] + ((-cnt_ref[NE - 1]) & (bm - 1))
        tail = lambda c: pltpu.make_async_copy(zero_ref, _tile_rows(xs_hbm, used + c * half, half), sem_z)

        def tail_start(c, z):
            tail(c).start()
            return z

        def tail_wait(c, z):
            tail(c).wait()
            return z

        n_tail = (xs_hbm.shape[0] // TILE_ROWS - used) // half
        lax.fori_loop(0, n_tail, tail_start, 0)
        lax.fori_loop(0, n_tail, tail_wait, 0)

    cp = pltpu.make_async_copy(dest_hbm.at[i, 0], dest_smem, sem_idx)
    cp.start()
    cp.wait()

    def body(r, c):
        for k in range(TOPK):
            pltpu.make_async_copy(_tile_rows(x_ref, r), _tile_rows(xs_hbm, dest_smem[k * tm + r]), sem).start(
                priority=k % 2)
        return c

    lax.fori_loop(0, tm, body, 0)
    for k in range(TOPK):
        pltpu.make_async_copy(x_ref, _tile_rows(xs_hbm, 0, tm), sem).wait()


def _dispatch(counts, pad_start, dest_tiles, x1t, P, tm, bm):
    T = x1t.shape[0] // TILE_ROWS
    return pl.pallas_call(
        functools.partial(_dispatch_kernel, tm=tm, bm=bm),
        grid_spec=pltpu.PrefetchScalarGridSpec(
            num_scalar_prefetch=2, grid=(T // tm,),
            in_specs=[pl.BlockSpec(memory_space=pl.ANY),
                      pl.BlockSpec((tm * TILE_ROWS, LANES), lambda i, c, s: (i, 0))],
            out_specs=pl.BlockSpec(memory_space=pl.ANY),
            scratch_shapes=[pltpu.SMEM((TOPK * tm,), I32), pltpu.VMEM((bm // 2 * TILE_ROWS, LANES), F32),
                            pltpu.SemaphoreType.DMA, pltpu.SemaphoreType.DMA, pltpu.SemaphoreType.DMA]),
        out_shape=jax.ShapeDtypeStruct((P * TILE_ROWS, LANES), F32),
        compiler_params=pltpu.CompilerParams(dimension_semantics=("arbitrary",)),
        name="dispatch",
    )(counts, pad_start, dest_tiles, x1t)


def _expert_kernel(blk_e_ref, n_used_ref, x_ref, wgu_ref, wdn_ref, y_ref):
    del blk_e_ref
    b = pl.program_id(0)

    @pl.when(b < n_used_ref[0])
    def _():
        gu = _mm(_from_token_tiles(x_ref), wgu_ref[0])
        gate, up = gu[:, :DE], gu[:, DE:]
        _to_token_tiles(y_ref, _mm(gate * _sigmoid(gate) * up, wdn_ref[0]))

    @pl.when(b >= n_used_ref[0])
    def _():
        y_ref[...] = jnp.zeros(y_ref.shape, F32)


def _experts(blk_e, n_used, xs, wgu, wdn, bm):
    n_blocks = xs.shape[0] // (bm * TILE_ROWS)
    live = lambda b, n: jnp.minimum(b, n[0] - 1)
    return pl.pallas_call(
        _expert_kernel,
        grid_spec=pltpu.PrefetchScalarGridSpec(
            num_scalar_prefetch=2, grid=(n_blocks,),
            in_specs=[pl.BlockSpec((bm * TILE_ROWS, LANES), lambda b, e, n: (live(b, n), 0)),
                      pl.BlockSpec((1, D, 2 * DE), lambda b, e, n: (e[live(b, n)], 0, 0)),
                      pl.BlockSpec((1, DE, D), lambda b, e, n: (e[live(b, n)], 0, 0))],
            out_specs=pl.BlockSpec((bm * TILE_ROWS, LANES), lambda b, e, n: (b, 0))),
        out_shape=jax.ShapeDtypeStruct(xs.shape, F32),
        compiler_params=pltpu.CompilerParams(dimension_semantics=("parallel",)),
        name="experts",
    )(blk_e, n_used, xs, wgu, wdn)


def _combine_kernel(dest_hbm, y_hbm, x_ref, w_ref, wsg_ref, wsd_ref, lg_ref, lb_ref, o_ref, dest_smem, rows_ref,
                    sem_idx, sem, *, tm, alpha):
    i = pl.program_id(0)
    cp = pltpu.make_async_copy(dest_hbm.at[i, 0], dest_smem, sem_idx)
    cp.start()
    cp.wait()

    def body(r, c):
        for k in range(TOPK):
            pltpu.make_async_copy(_tile_rows(y_hbm, dest_smem[k * tm + r]), _tile_rows(rows_ref.at[k], r), sem).start(
                priority=k % 2)
        return c

    lax.fori_loop(0, tm, body, 0)
    x = x_ref[...]
    gu = _mm(x, wsg_ref[...])
    gate, up = gu[:, :DE], gu[:, DE:]
    shared = _mm(gate * _sigmoid(gate) * up, wsd_ref[...])
    for k in range(TOPK):
        pltpu.make_async_copy(_tile_rows(y_hbm, 0, tm), rows_ref.at[k], sem).wait()
    w = w_ref[...]
    routed = _from_token_tiles(rows_ref.at[0]) * w[:, 0:1]
    for k in range(1, TOPK):
        routed = routed + _from_token_tiles(rows_ref.at[k]) * w[:, k:k + 1]
    o_ref[...] = _layer_norm(alpha * x + (routed + shared), lg_ref[...], lb_ref[...])


def _combine(dest_tiles, ys, x1, w_tok, wsg, wsd, lg, lb, alpha, tm):
    T = x1.shape[0]
    full = lambda a: pl.BlockSpec(a.shape, lambda i: (0,) * a.ndim)
    return pl.pallas_call(
        functools.partial(_combine_kernel, tm=tm, alpha=alpha),
        grid=(T // tm,),
        in_specs=[pl.BlockSpec(memory_space=pl.ANY), pl.BlockSpec(memory_space=pl.ANY),
                  pl.BlockSpec((tm, D), lambda i: (i, 0)), pl.BlockSpec((tm, TOPK), lambda i: (i, 0))]
        + [full(a) for a in (wsg, wsd, lg, lb)],
        out_specs=pl.BlockSpec((tm, D), lambda i: (i, 0)),
        out_shape=jax.ShapeDtypeStruct((T, D), F32),
        scratch_shapes=[pltpu.SMEM((TOPK * tm,), I32), pltpu.VMEM((TOPK, tm * TILE_ROWS, LANES), F32),
                        pltpu.SemaphoreType.DMA, pltpu.SemaphoreType.DMA],
        compiler_params=pltpu.CompilerParams(dimension_semantics=("arbitrary",)),
        name="combine",
    )(dest_tiles, ys, x1, w_tok, wsg, wsd, lg, lb)


def _regroup_w_in(w, b):
    offs = np.cumsum([0, 512, 128, 128, 384, 192, 192, 512, 512, 512, 512, 16, 3 * D])
    qa, ka, va, qb, kb, vb, qc, kc, vc, oc, gc, gates = (int(o) for o in offs[:-1])
    pieces = [(qa, ka), (ka, va), (va, qb)]
    for gi in range(3):
        pieces += [(qb + gi * 128, qb + (gi + 1) * 128), (kb + gi * HD, kb + (gi + 1) * HD),
                   (vb + gi * HD, vb + (gi + 1) * HD)]
    pieces += [(qc, kc), (kc, vc), (vc, oc), (oc, gc), (gc, gates), None, (gates, int(offs[-1]))]
    gc_pad = (U_GATES - U_GC) * LANES - (gates - gc)

    def build(a):
        parts = [jnp.zeros(a.shape[:-1] + (gc_pad,), a.dtype) if p is None else a[..., p[0]:p[1]] for p in pieces]
        return jnp.concatenate(parts, axis=-1)

    return build(w).astype(BF16), build(b).reshape(1, NPAD)


def _trunk(x, seq_lens, depth, params, *, bm, tm_proj, tn_proj, tm_merge, tm_route, tm_moe):
    (w_in, b_in, sink, conv_w, w_br_a, w_br_b, w_br_c, w_out, ln1_g, ln1_b, w_router, router_bias, w_exp_gu,
     w_exp_down, w_sh_gu, w_sh_down, ln2_g, ln2_b) = params
    T = x.shape[0]
    alpha = float((2 * depth) ** 0.25)
    slopes_a = tuple(float(2.0 ** (-8.0 * i / 8)) for i in range(1, 9))
    slopes_b = tuple(float(2.0 ** (-8.0 * i / 6)) for i in range(1, 7))
    flags1 = jnp.asarray(_edge_flags(seq_lens, QB))
    flags_b = [jnp.asarray(_edge_flags(seq_lens, QB * d)) for d in B_DILS]
    A = T * TOPK
    n_blocks = A // bm + NE
    P = n_blocks * bm
    row2 = lambda a: a.reshape(1, -1)
    for l in range(depth):
        w_all, b_all = _regroup_w_in(w_in[l], b_in[l])
        proj = _proj(x, w_all, b_all, tm_proj, tn_proj)
        oa = _attn_a(proj, flags1, sink[l].reshape(1, A_HK * A_G), slopes_a)
        obs, lses = [], []
        for gi in range(3):
            o, s = _attn_b(proj, flags_b[gi], gi, slopes_b[gi * B_G:(gi + 1) * B_G])
            obs.append(o)
            lses.append(s)
        gct = proj[:, U_GC * LANES:U_GC * LANES + 16].T
        hf, hb = _mlstm(proj, gct, conv_w[l], flags1)
        x1, x1t = _merge(x, proj, oa, obs, lses, hf, hb, w_br_a[l].astype(BF16), w_br_b[l].astype(BF16),
                    w_br_c[l].astype(BF16), w_out[l].astype(BF16), row2(ln1_g[l]), row2(ln1_b[l]), alpha, tm_merge)
        top_idx, top_w, rank, counts = _route(x1, w_router[l].T, router_bias[l].reshape(NE, 1), tm_route)
        counts = counts[:, 0]
        padded = (counts + bm - 1) // bm * bm
        pad_end = jnp.cumsum(padded)
        pad_start = pad_end - padded
        dest = _dest_tiles(top_idx, rank, pad_start, tm_moe)
        blk_e = jnp.minimum(jnp.searchsorted(pad_end, jnp.arange(n_blocks, dtype=I32) * bm, side='right'),
                            NE - 1).astype(I32)
        n_used = (pad_end[-1:] // bm).astype(I32)
        xs = _dispatch(counts, pad_start, dest, x1t, P, tm_moe, bm)
        ys = _experts(blk_e, n_used, xs, w_exp_gu[l].astype(BF16), w_exp_down[l].astype(BF16), bm)
        x = _combine(dest, ys, x1, top_w.T, w_sh_gu[l].astype(BF16), w_sh_down[l].astype(BF16),
                     row2(ln2_g[l]), row2(ln2_b[l]), alpha, tm_moe)
    return x


def kernel(x_prompt, x_sample, w_in, b_in, sink, conv_w, w_br_a, w_br_b, w_br_c, w_out, ln1_g, ln1_b, w_router,
           router_bias, w_exp_gu, w_exp_down, w_sh_gu, w_sh_down, ln2_g, ln2_b):
    depth = w_in.shape[0]
    n1, s1, _ = x_prompt.shape
    n2, s2, _ = x_sample.shape
    span = QB * max(B_DILS)
    assert s1 % span == 0 and s2 % span == 0, "sequence lengths must be multiples of the widest dilated block"
    x = jnp.concatenate([x_prompt.reshape(n1 * s1, D), x_sample.reshape(n2 * s2, D)], axis=0)
    params = (w_in, b_in, sink, conv_w, w_br_a, w_br_b, w_br_c, w_out, ln1_g, ln1_b, w_router, router_bias,
              w_exp_gu, w_exp_down, w_sh_gu, w_sh_down, ln2_g, ln2_b)
    y = _trunk(x, [s1] * n1 + [s2] * n2, depth, params, bm=512, tm_proj=1024, tn_proj=512, tm_merge=256,
               tm_route=256, tm_moe=512)
    return y[:n1 * s1].reshape(n1, s1, D), y[n1 * s1:].reshape(n2, s2, D)
```

```python
import functools

import numpy as np
import jax
import jax.numpy as jnp
from jax import lax
from jax.experimental import pallas as pl
from jax.experimental.pallas import tpu as pltpu

F32, BF16, I32, U32 = jnp.float32, jnp.bfloat16, jnp.int32, jnp.uint32

LANES = 128
D = 1024
HD = 64
A_HK, A_G, A_BAND = 2, 4, 128
B_G, B_BAND = 2, 64
B_DILS = (1, 4, 16)
C_H, C_HD, CHUNK = 4, 128, 128
NE, TOPK, NGRP, TOPG, DE = 256, 8, 8, 4, 256
EPG = NE // NGRP
ROUTED_SCALE = 2.5
LN_EPS = 1e-5
NEG = -1e30
QB = 128
SLOT_ROWS = D // (2 * LANES)

U_QA, U_KA, U_VA = 0, 4, 5
U_QB = (6, 8, 10)
U_KVB = (7, 9, 11)
U_QC, U_KC, U_VC, U_OC, U_GC = 12, 16, 20, 24, 28
N_UNITS = 32
NPAD = N_UNITS * LANES


def _edge_flags(seq_lens, rows):
    first, last = [], []
    for s in seq_lens:
        nb = s // rows
        first += [1] + [0] * (nb - 1)
        last += [0] * (nb - 1) + [1]
    return np.array([first, last], np.int32)


def _mm(a, b):
    return jnp.dot(a.astype(BF16), b.astype(BF16), preferred_element_type=F32)


def _mm_nt(a, b):
    return lax.dot_general(a.astype(BF16), b.astype(BF16), (((1,), (1,)), ((), ())), preferred_element_type=F32)


def _mm_tn(a, b):
    return lax.dot_general(a.astype(BF16), b.astype(BF16), (((0,), (0,)), ((), ())), preferred_element_type=F32)


def _split3(x):
    hi = x.astype(BF16)
    r1 = x - hi.astype(F32)
    mid = r1.astype(BF16)
    lo = (r1 - mid.astype(F32)).astype(BF16)
    return hi, mid, lo


def _sigmoid(x):
    return 1.0 / (1.0 + jnp.exp(-x))


def _log_sigmoid(x):
    return jnp.minimum(x, 0.0) - jnp.log(1.0 + jnp.exp(-jnp.abs(x)))


def _pack_tokens(ref, x):
    n = x.shape[0]
    u = lax.bitcast_convert_type(x, U32)
    u = u + (jnp.uint32(0x7FFF) + ((u >> 16) & jnp.uint32(1)))
    for j in range(SLOT_ROWS):
        lo = u[:, j * LANES:(j + 1) * LANES] >> 16
        hi = u[:, D // 2 + j * LANES:D // 2 + (j + 1) * LANES] & jnp.uint32(0xFFFF0000)
        ref[pl.ds(j, n, stride=SLOT_ROWS), :] = lo | hi


def _unpack_tokens(ref):
    n = ref.shape[0] // SLOT_ROWS
    words = [ref[pl.ds(j, n, stride=SLOT_ROWS), :] for j in range(SLOT_ROWS)]
    lo = [lax.bitcast_convert_type(w << 16, F32) for w in words]
    hi = [lax.bitcast_convert_type(w & jnp.uint32(0xFFFF0000), F32) for w in words]
    return jnp.concatenate(lo + hi, axis=1)


def _layer_norm(x, g, b):
    mu = jnp.mean(x, axis=-1, keepdims=True)
    xc = x - mu
    var = jnp.mean(xc * xc, axis=-1, keepdims=True)
    return xc * lax.rsqrt(var + LN_EPS) * g + b


def _proj_kernel(x_ref, w_ref, b_ref, o_ref, xb_ref):
    @pl.when(pl.program_id(1) == 0)
    def _():
        xb_ref[...] = x_ref[...].astype(BF16)

    o_ref[...] = jnp.dot(xb_ref[...], w_ref[...], preferred_element_type=F32) + b_ref[...]


def _proj(x, w, b, tm, tn):
    T, N = x.shape[0], w.shape[1]
    return pl.pallas_call(
        _proj_kernel,
        grid=(T // tm, N // tn),
        in_specs=[pl.BlockSpec((tm, D), lambda i, j: (i, 0)),
                  pl.BlockSpec((D, tn), lambda i, j: (0, j)),
                  pl.BlockSpec((1, tn), lambda i, j: (0, j))],
        out_specs=pl.BlockSpec((tm, tn), lambda i, j: (i, j)),
        out_shape=jax.ShapeDtypeStruct((T, N), F32),
        scratch_shapes=[pltpu.VMEM((tm, D), BF16)],
        compiler_params=pltpu.CompilerParams(dimension_semantics=("parallel", "arbitrary")),
        name="proj",
    )(x, w, b)


def _attn_tile(q, kwin, vwin, q0, k0, lo, hi, sinks, *, hk, g, band, dist, slopes, packed):
    qb, W = q.shape[0], kwin.shape[0]
    rows = g * qb
    row = lax.broadcasted_iota(I32, (rows, W), 0)
    col = lax.broadcasted_iota(I32, (rows, W), 1) + k0
    rel = jnp.abs(col - q0 - (row & (qb - 1)))
    mask = (rel <= band) & (col >= lo) & (col < hi)
    relf = rel.astype(F32) * float(dist)
    row_g = jnp.right_shift(lax.broadcasted_iota(I32, (rows, 1), 0), qb.bit_length() - 1)
    outs, lses = [], []
    for h in range(hk):
        qs = jnp.concatenate([q[:, (h * g + gg) * HD:(h * g + gg + 1) * HD] for gg in range(g)], axis=0)
        qs = qs * (HD ** -0.5)
        ko = h * HD
        vo = HD if packed else h * HD
        kh = kwin[:, ko:ko + HD]
        vh = vwin[:, vo:vo + HD]
        s = _mm_nt(qs, kh)
        slope_col = jnp.zeros((rows, 1), F32)
        for gg in range(g):
            slope_col = jnp.where(row_g == gg, float(slopes[h * g + gg]), slope_col)
        logits = jnp.where(mask, s - slope_col * relf, NEG)
        m = jnp.max(logits, axis=-1, keepdims=True)
        if sinks is not None:
            sink_col = jnp.zeros((rows, 1), F32)
            for gg in range(g):
                sink_col = jnp.where(row_g == gg, sinks[h * g + gg], sink_col)
            m = jnp.maximum(m, sink_col)
        p = jnp.exp(logits - m)
        den = jnp.sum(p, axis=-1, keepdims=True)
        if sinks is not None:
            den = den + jnp.exp(sink_col - m)
        o = _mm(p, vh) / den
        for gg in range(g):
            outs.append(o[gg * qb:(gg + 1) * qb])
        if packed:
            l = m + jnp.log(den)
            for gg in range(g):
                lses.append(jnp.broadcast_to(l[gg * qb:(gg + 1) * qb], (qb, HD)))
    return outs, lses


def _attn_a_kernel(flags_ref, sink_ref, q_ref, kp_ref, ko_ref, kn_ref, vp_ref, vo_ref, vn_ref, o_ref, *, slopes):
    b = pl.program_id(0)
    lo = jnp.where(flags_ref[0, b] == 0, 0, QB)
    hi = jnp.where(flags_ref[1, b] == 0, 3 * QB, 2 * QB)
    sinks = [sink_ref[0, i] for i in range(A_HK * A_G)]
    kwin = jnp.concatenate([kp_ref[...], ko_ref[...], kn_ref[...]], axis=0)
    vwin = jnp.concatenate([vp_ref[...], vo_ref[...], vn_ref[...]], axis=0)
    outs, _ = _attn_tile(q_ref[...], kwin, vwin, QB, 0, lo, hi, sinks, hk=A_HK, g=A_G, band=A_BAND, dist=1,
                         slopes=slopes, packed=False)
    o_ref[...] = jnp.concatenate(outs, axis=1).astype(o_ref.dtype)


def _attn_a(proj, flags, sink, slopes):
    T = proj.shape[0]
    nb = T // QB
    prev = lambda b, f: jnp.maximum(b - 1, 0)
    nxt = lambda b, f: jnp.minimum(b + 1, nb - 1)
    kspecs = lambda u: [pl.BlockSpec((QB, LANES), lambda b, f: (prev(b, f), u)),
                        pl.BlockSpec((QB, LANES), lambda b, f: (b, u)),
                        pl.BlockSpec((QB, LANES), lambda b, f: (nxt(b, f), u))]
    return pl.pallas_call(
        functools.partial(_attn_a_kernel, slopes=slopes),
        grid_spec=pltpu.PrefetchScalarGridSpec(
            num_scalar_prefetch=1, grid=(nb,),
            in_specs=[pl.BlockSpec(memory_space=pltpu.SMEM),
                      pl.BlockSpec((QB, A_HK * A_G * HD), lambda b, f: (b, U_QA // 4))]
            + kspecs(U_KA) + kspecs(U_VA),
            out_specs=pl.BlockSpec((QB, A_HK * A_G * HD), lambda b, f: (b, 0))),
        out_shape=jax.ShapeDtypeStruct((T, A_HK * A_G * HD), BF16),
        compiler_params=pltpu.CompilerParams(dimension_semantics=("parallel",)),
        name="attn_a",
    )(flags, sink, proj, proj, proj, proj, proj, proj, proj)


def _attn_b_kernel(flags_ref, q_ref, kp_ref, ko_ref, kn_ref, o_ref, lse_ref, *, dil, slopes):
    b = pl.program_id(0)
    lo = jnp.where(flags_ref[0, b] == 0, 0, QB)
    hi = jnp.where(flags_ref[1, b] == 0, 3 * QB, 2 * QB)

    def one(r):
        sub = lambda ref: ref[...] if dil == 1 else ref[pl.ds(r, QB, stride=dil), :]
        kv = jnp.concatenate([sub(kp_ref), sub(ko_ref), sub(kn_ref)], axis=0)
        outs, lses = _attn_tile(sub(q_ref), kv, kv, QB, 0, lo, hi, None, hk=1, g=B_G, band=B_BAND, dist=dil,
                                slopes=slopes, packed=True)
        o, l = jnp.concatenate(outs, axis=1), jnp.concatenate(lses, axis=1)
        if dil == 1:
            o_ref[...] = o
            lse_ref[...] = l
        else:
            o_ref[pl.ds(r, QB, stride=dil), :] = o
            lse_ref[pl.ds(r, QB, stride=dil), :] = l

    if dil == 1:
        one(0)
    else:
        def body(r, c):
            one(r)
            return c
        lax.fori_loop(0, dil, body, 0, unroll=2)


def _attn_b(proj, flags, gi, slopes):
    dil = B_DILS[gi]
    T = proj.shape[0]
    rows = QB * dil
    nb = T // rows
    uq, ukv = U_QB[gi], U_KVB[gi]
    return pl.pallas_call(
        functools.partial(_attn_b_kernel, dil=dil, slopes=slopes),
        grid_spec=pltpu.PrefetchScalarGridSpec(
            num_scalar_prefetch=1, grid=(nb,),
            in_specs=[pl.BlockSpec((rows, LANES), lambda b, f: (b, uq)),
                      pl.BlockSpec((rows, LANES), lambda b, f: (jnp.maximum(b - 1, 0), ukv)),
                      pl.BlockSpec((rows, LANES), lambda b, f: (b, ukv)),
                      pl.BlockSpec((rows, LANES), lambda b, f: (jnp.minimum(b + 1, nb - 1), ukv))],
            out_specs=[pl.BlockSpec((rows, LANES), lambda b, f: (b, 0)),
                       pl.BlockSpec((rows, LANES), lambda b, f: (b, 0))]),
        out_shape=[jax.ShapeDtypeStruct((T, LANES), F32), jax.ShapeDtypeStruct((T, LANES), F32)],
        compiler_params=pltpu.CompilerParams(dimension_semantics=("parallel",)),
        name=f"attn_b{gi}",
    )(flags, proj, proj, proj, proj)


def _conv_silu(x, xp8, xn8, w, keep_prev, keep_next):
    rowi = lax.broadcasted_iota(I32, (CHUNK, 1), 0)
    x_prev = jnp.where(rowi == 0, xp8[7:8, :] * keep_prev, pltpu.roll(x, 1, 0))
    x_next = jnp.where(rowi == CHUNK - 1, xn8[0:1, :] * keep_next, pltpu.roll(x, CHUNK - 1, 0))
    y = w[0:1, :] * x_prev + w[1:2, :] * x + w[2:3, :] * x_next
    return y * _sigmoid(y)


def _mlstm_dir(rev, first, last, q_ref, qp_ref, qn_ref, k_ref, kp_ref, kn_ref, v_ref, g_ref, gt_ref, cw_ref,
               c_ref, n_ref, m_ref, h_ref):
    sd = 1 if rev else 0
    keep_prev = (1 - first).astype(F32)
    keep_next = (1 - last).astype(F32)

    @pl.when((last if rev else first) == 1)
    def _():
        c_ref[sd] = jnp.zeros(c_ref.shape[1:], F32)
        n_ref[sd] = jnp.zeros(n_ref.shape[1:], F32)
        m_ref[sd] = jnp.zeros(m_ref.shape[1:], F32)

    W = C_H * C_HD
    q = _conv_silu(q_ref[...], qp_ref[...], qn_ref[...], cw_ref[:, :W], keep_prev, keep_next)
    k = _conv_silu(k_ref[...], kp_ref[...], kn_ref[...], cw_ref[:, W:], keep_prev, keep_next) * (C_HD ** -0.5)
    v = v_ref[...]
    G = g_ref[...]
    GT = gt_ref[...]
    ti = lax.broadcasted_iota(I32, (CHUNK, CHUNK), 0)
    si = lax.broadcasted_iota(I32, (CHUNK, CHUNK), 1)
    tri = (si >= ti) if rev else (si <= ti)
    L = jnp.where(tri, 1.0, 0.0).astype(BF16)
    Lt = jnp.where((ti >= si) if rev else (ti <= si), 1.0, 0.0).astype(BF16)
    cum_col = sum(jnp.dot(L, p, preferred_element_type=F32) for p in _split3(_log_sigmoid(G)))
    cum_row = sum(jnp.dot(p, Lt, preferred_element_type=F32) for p in _split3(_log_sigmoid(GT)))
    e = 0 if rev else CHUNK - 1
    hs = []
    for h in range(C_H):
        ci = (2 * C_H if rev else 0) + h
        cf = ci + C_H
        cum_c, cum_r = cum_col[:, cf:cf + 1], cum_row[cf:cf + 1, :]
        li_c, li_r = G[:, ci:ci + 1], GT[ci:ci + 1, :]
        qh, kh, vh = (a[:, h * C_HD:(h + 1) * C_HD] for a in (q, k, v))
        m11 = m_ref[sd, h][:, :1]
        Dm = jnp.where(tri, cum_c - cum_r + li_r, NEG)
        m_inter = cum_c + m11
        mt = jnp.maximum(jnp.max(Dm, axis=-1, keepdims=True), m_inter)
        Wm = jnp.exp(Dm - mt) * _mm_nt(qh, kh)
        inter = jnp.exp(m_inter - mt)
        num = _mm(Wm, vh) + inter * _mm(qh, c_ref[sd, h])
        den = jnp.sum(Wm, axis=-1, keepdims=True) + inter * jnp.sum(qh * n_ref[sd, h], axis=-1, keepdims=True)
        hs.append(num / jnp.maximum(jnp.abs(den), jnp.exp(-mt)))
        total = cum_c[e:e + 1, :]
        gcol = total - cum_c + li_c
        m_new = jnp.maximum(total + m11, jnp.max(gcol, axis=0, keepdims=True))
        kw = kh * jnp.exp(gcol - m_new)
        decay = jnp.exp(total + m11 - m_new)
        c_ref[sd, h] = decay * c_ref[sd, h] + _mm_tn(kw, vh)
        n_ref[sd, h] = decay * n_ref[sd, h] + jnp.sum(kw, axis=0, keepdims=True)
        m_ref[sd, h] = jnp.broadcast_to(m_new, (1, LANES))
    h_ref[...] = jnp.concatenate(hs, axis=1)


def _mlstm_kernel(flags_ref, *refs, nb):
    fwd, bwd, (cw_ref, hf_ref, hb_ref, c_ref, n_ref, m_ref) = refs[:9], refs[9:18], refs[18:]
    j = pl.program_id(0)
    jb = nb - 1 - j
    _mlstm_dir(False, flags_ref[0, j], flags_ref[1, j], *fwd, cw_ref, c_ref, n_ref, m_ref, hf_ref)
    _mlstm_dir(True, flags_ref[0, jb], flags_ref[1, jb], *bwd, cw_ref, c_ref, n_ref, m_ref, hb_ref)


def _mlstm(proj, gct, conv_w, flags):
    T = proj.shape[0]
    nb = T // CHUNK
    W = C_H * C_HD
    r8 = CHUNK // 8

    def specs(blk):
        halo_p = lambda j, f: jnp.maximum(blk(j) * r8 - 1, 0)
        halo_n = lambda j, f: jnp.minimum(blk(j) * r8 + r8, nb * r8 - 1)
        out = []
        for u in (U_QC, U_KC):
            out += [pl.BlockSpec((CHUNK, W), lambda j, f, u=u: (blk(j), u // 4)),
                    pl.BlockSpec((8, W), lambda j, f, u=u: (halo_p(j, f), u // 4)),
                    pl.BlockSpec((8, W), lambda j, f, u=u: (halo_n(j, f), u // 4))]
        out += [pl.BlockSpec((CHUNK, W), lambda j, f: (blk(j), U_VC // 4)),
                pl.BlockSpec((CHUNK, LANES), lambda j, f: (blk(j), U_GC)),
                pl.BlockSpec((16, CHUNK), lambda j, f: (0, blk(j)))]
        return out

    fwd_blk = lambda j: j
    bwd_blk = lambda j: nb - 1 - j
    args = [proj] * 8 + [gct]
    return pl.pallas_call(
        functools.partial(_mlstm_kernel, nb=nb),
        grid_spec=pltpu.PrefetchScalarGridSpec(
            num_scalar_prefetch=1, grid=(nb,),
            in_specs=specs(fwd_blk) + specs(bwd_blk) + [pl.BlockSpec((3, 2 * W), lambda j, f: (0, 0))],
            out_specs=[pl.BlockSpec((CHUNK, W), lambda j, f: (j, 0)),
                       pl.BlockSpec((CHUNK, W), lambda j, f: (nb - 1 - j, 0))],
            scratch_shapes=[pltpu.VMEM((2, C_H, C_HD, C_HD), F32),
                            pltpu.VMEM((2, C_H, 1, C_HD), F32),
                            pltpu.VMEM((2, C_H, 1, LANES), F32)]),
        out_shape=[jax.ShapeDtypeStruct((T, W), F32), jax.ShapeDtypeStruct((T, W), F32)],
        compiler_params=pltpu.CompilerParams(dimension_semantics=("arbitrary",)),
        name="mlstm",
    )(flags, *args, *args, conv_w)


def _merge_kernel(x_ref, oa_ref, ob0_ref, ob1_ref, ob2_ref, l0_ref, l1_ref, l2_ref, hf_ref, hb_ref, oc_ref,
                  wg_ref, bg_ref, wa_ref, wb_ref, wc_ref, wo_ref, lg_ref, lb_ref, o_ref, ot_ref, *, alpha):
    ya = jnp.dot(oa_ref[...], wa_ref[...], preferred_element_type=F32)
    l0, l1, l2 = l0_ref[...], l1_ref[...], l2_ref[...]
    lm = jnp.maximum(jnp.maximum(l0, l1), l2)
    e0, e1, e2 = jnp.exp(l0 - lm), jnp.exp(l1 - lm), jnp.exp(l2 - lm)
    es = e0 + e1 + e2
    outb = jnp.concatenate([ob0_ref[...] * (e0 / es), ob1_ref[...] * (e1 / es), ob2_ref[...] * (e2 / es)], axis=1)
    yb = _mm(outb, wb_ref[...])
    yc = _mm(_sigmoid(oc_ref[...]) * (hf_ref[...] + hb_ref[...]), wc_ref[...])
    x = x_ref[...]
    gates = _sigmoid(_mm(x, wg_ref[...]) + bg_ref[...])
    merged = gates[:, 0:D] * ya + gates[:, D:2 * D] * yb + gates[:, 2 * D:3 * D] * yc
    mix = _mm(merged, wo_ref[...])
    x1 = _layer_norm(alpha * x + mix, lg_ref[...], lb_ref[...])
    o_ref[...] = x1
    _pack_tokens(ot_ref, x1)


def _merge(x, proj, oa, obs, lses, hf, hb, wg, bg, wa, wb, wc, wo, lg, lb, alpha, tm):
    T = x.shape[0]
    row = lambda w: pl.BlockSpec((tm, w), lambda i: (i, 0))
    full = lambda a: pl.BlockSpec(a.shape, lambda i: (0,) * a.ndim)
    return pl.pallas_call(
        functools.partial(_merge_kernel, alpha=alpha),
        grid=(T // tm,),
        in_specs=[row(D), row(oa.shape[1])] + [row(LANES)] * 6 + [row(hf.shape[1])] * 2
        + [pl.BlockSpec((tm, 512), lambda i: (i, U_OC // 4))]
        + [full(a) for a in (wg, bg, wa, wb, wc, wo, lg, lb)],
        out_specs=[row(D), pl.BlockSpec((tm * SLOT_ROWS, LANES), lambda i: (i, 0))],
        out_shape=[jax.ShapeDtypeStruct((T, D), F32), jax.ShapeDtypeStruct((T * SLOT_ROWS, LANES), U32)],
        compiler_params=pltpu.CompilerParams(dimension_semantics=("parallel",)),
        name="merge",
    )(x, oa, *obs, *lses, hf, hb, proj, wg, bg, wa, wb, wc, wo, lg, lb)


def _stack_rows(rows, iota):
    out = jnp.zeros(iota.shape, F32)
    for k, r in enumerate(rows):
        out = jnp.where(iota == float(k), r, out)
    return out


def _route_kernel(x_ref, wrt_ref, bias_ref, idx_ref, w_ref, rank_ref, cnt_ref, run_ref, *, tm):
    @pl.when(pl.program_id(0) == 0)
    def _():
        run_ref[...] = jnp.zeros(run_ref.shape, F32)

    x = x_ref[...]
    xh, xm, xl = _split3(x)
    wh, wm, wl = _split3(wrt_ref[...])
    nt = lambda a, b: lax.dot_general(a, b, (((1,), (1,)), ((), ())), preferred_element_type=F32)
    logits = nt(wh, xh) + (nt(wh, xm) + nt(wm, xh)) + (nt(wh, xl) + nt(wl, xh) + nt(wm, xm))
    scores = _sigmoid(logits)
    biased = scores + bias_ref[...]
    ninf = -jnp.inf
    iota_g = lax.broadcasted_iota(I32, (EPG, tm), 0).astype(F32)
    gs = []
    for gi in range(NGRP):
        v = biased[gi * EPG:(gi + 1) * EPG]
        t1 = jnp.max(v, axis=0, keepdims=True)
        i1 = jnp.min(jnp.where(v == t1, iota_g, float(EPG)), axis=0, keepdims=True)
        t2 = jnp.max(jnp.where(iota_g == i1, ninf, v), axis=0, keepdims=True)
        gs.append(t1 + t2)
    iota8 = lax.broadcasted_iota(I32, (NGRP, tm), 0).astype(F32)
    gs = _stack_rows(gs, iota8)
    gsel = jnp.zeros((NGRP, tm), F32)
    for _ in range(TOPG):
        mx = jnp.max(gs, axis=0, keepdims=True)
        ix = jnp.min(jnp.where(gs == mx, iota8, float(NGRP)), axis=0, keepdims=True)
        hit = iota8 == ix
        gsel = jnp.where(hit, 1.0, gsel)
        gs = jnp.where(hit, ninf, gs)
    v = jnp.concatenate([jnp.where(gsel[gi:gi + 1] > 0.0, biased[gi * EPG:(gi + 1) * EPG], ninf)
                         for gi in range(NGRP)], axis=0)
    iota_e = lax.broadcasted_iota(I32, (NE, tm), 0).astype(F32)
    sel = jnp.zeros((NE, tm), F32)
    idxs, svals = [], []
    for _ in range(TOPK):
        mx = jnp.max(v, axis=0, keepdims=True)
        ix = jnp.min(jnp.where(v == mx, iota_e, float(NE)), axis=0, keepdims=True)
        hit = iota_e == ix
        idxs.append(ix)
        svals.append(jnp.sum(jnp.where(hit, scores, 0.0), axis=0, keepdims=True))
        sel = jnp.where(hit, 1.0, sel)
        v = jnp.where(hit, ninf, v)
    top_s = _stack_rows(svals, iota8)
    idx_ref[...] = _stack_rows(idxs, iota8).astype(I32)
    w_ref[...] = top_s / jnp.sum(top_s, axis=0, keepdims=True) * ROUTED_SCALE
    ti = lax.broadcasted_iota(I32, (tm, tm), 0)
    si = lax.broadcasted_iota(I32, (tm, tm), 1)
    before = jnp.where(ti < si, 1.0, 0.0).astype(BF16)
    selb = sel.astype(BF16)
    cnt = run_ref[...] + jnp.dot(selb, before, preferred_element_type=F32)
    rank_ref[...] = _stack_rows(
        [jnp.sum(jnp.where(iota_e == ix, cnt, 0.0), axis=0, keepdims=True) for ix in idxs], iota8).astype(I32)
    run = run_ref[...] + jnp.dot(selb, jnp.ones((tm, tm), BF16), preferred_element_type=F32)
    run_ref[...] = run
    cnt_ref[...] = run.astype(I32)


def _route(x1, wrt, bias_col, tm):
    T = x1.shape[0]
    slot = lambda dt: jax.ShapeDtypeStruct((TOPK, T), dt)
    return pl.pallas_call(
        functools.partial(_route_kernel, tm=tm),
        grid=(T // tm,),
        in_specs=[pl.BlockSpec((tm, D), lambda i: (i, 0)),
                  pl.BlockSpec((NE, D), lambda i: (0, 0)),
                  pl.BlockSpec((NE, 1), lambda i: (0, 0))],
        out_specs=[pl.BlockSpec((TOPK, tm), lambda i: (0, i))] * 3 + [pl.BlockSpec((NE, tm), lambda i: (0, 0))],
        out_shape=[slot(I32), slot(F32), slot(I32), jax.ShapeDtypeStruct((NE, tm), I32)],
        scratch_shapes=[pltpu.VMEM((NE, tm), F32)],
        compiler_params=pltpu.CompilerParams(dimension_semantics=("arbitrary",)),
        name="route",
    )(x1, wrt, bias_col)


def _dest_kernel(idx_ref, rank_ref, start_ref, o_ref, *, tm):
    iota_e = lax.broadcasted_iota(I32, (NE, tm), 0)
    starts = start_ref[...]
    rows = []
    for k in range(TOPK):
        base = jnp.sum(jnp.where(iota_e == idx_ref[k:k + 1, :], starts, 0.0), axis=0, keepdims=True)
        rows.append(base.astype(I32) + rank_ref[k:k + 1, :])
    o_ref[0] = jnp.concatenate(rows, axis=1)


def _dest_tiles(top_idx, rank, pad_start, tm):
    T = top_idx.shape[1]
    return pl.pallas_call(
        functools.partial(_dest_kernel, tm=tm),
        grid=(T // tm,),
        in_specs=[pl.BlockSpec((TOPK, tm), lambda i: (0, i)), pl.BlockSpec((TOPK, tm), lambda i: (0, i)),
                  pl.BlockSpec((NE, 1), lambda i: (0, 0))],
        out_specs=pl.BlockSpec((1, 1, TOPK * tm), lambda i: (i, 0, 0)),
        out_shape=jax.ShapeDtypeStruct((T // tm, 1, TOPK * tm), I32),
        compiler_params=pltpu.CompilerParams(dimension_semantics=("parallel",)),
        name="dest",
    )(top_idx, rank, pad_start.astype(F32).reshape(NE, 1))


def _tile_rows(ref, slot, n=1):
    return ref.at[pl.ds(pl.multiple_of(slot * SLOT_ROWS, SLOT_ROWS), n * SLOT_ROWS)]


def _dispatch_kernel(cnt_ref, start_ref, dest_hbm, x_ref, xs_hbm, dest_smem, zero_ref, sem_idx, sem, sem_z, *, tm, bm):
    i = pl.program_id(0)
    pad_sizes = [1 << k for k in range(bm.bit_length() - 1)]
    half = bm // 2

    def pad_copies(e, fn):
        n = (-cnt_ref[e]) & (bm - 1)
        at = start_ref[e] + cnt_ref[e]
        for sz in pad_sizes:
            @pl.when((n & sz) != 0)
            def _(at=at, sz=sz):
                fn(pltpu.make_async_copy(_tile_rows(zero_ref, 0, sz), _tile_rows(xs_hbm, at, sz), sem_z))
            at = at + (n & sz)

    @pl.when(i == 0)
    def _():
        zero_ref[...] = jnp.zeros(zero_ref.shape, U32)

        def start(e, c):
            pad_copies(e, lambda cp: cp.start())
            return c

        def wait(e, c):
            pad_copies(e, lambda cp: cp.wait())
            return c

        lax.fori_loop(0, NE, start, 0)
        lax.fori_loop(0, NE, wait, 0)

        used = start_ref[NE - 1] + cnt_ref[NE - 1] + ((-cnt_ref[NE - 1]) & (bm - 1))
        tail = lambda c: pltpu.make_async_copy(zero_ref, _tile_rows(xs_hbm, used + c * half, half), sem_z)

        def tail_start(c, z):
            tail(c).start()
            return z

        def tail_wait(c, z):
            tail(c).wait()
            return z

        n_tail = (xs_hbm.shape[0] // SLOT_ROWS - used) // half
        lax.fori_loop(0, n_tail, tail_start, 0)
        lax.fori_loop(0, n_tail, tail_wait, 0)

    cp = pltpu.make_async_copy(dest_hbm.at[i, 0], dest_smem, sem_idx)
    cp.start()
    cp.wait()

    def body(r, c):
        for k in range(TOPK):
            pltpu.make_async_copy(_tile_rows(x_ref, r), _tile_rows(xs_hbm, dest_smem[k * tm + r]), sem).start(
                priority=k % 2)
        return c

    lax.fori_loop(0, tm, body, 0)
    for k in range(TOPK):
        pltpu.make_async_copy(x_ref, _tile_rows(xs_hbm, 0, tm), sem).wait()


def _dispatch(counts, pad_start, dest_tiles, x1t, P, tm, bm):
    T = x1t.shape[0] // SLOT_ROWS
    return pl.pallas_call(
        functools.partial(_dispatch_kernel, tm=tm, bm=bm),
        grid_spec=pltpu.PrefetchScalarGridSpec(
            num_scalar_prefetch=2, grid=(T // tm,),
            in_specs=[pl.BlockSpec(memory_space=pl.ANY),
                      pl.BlockSpec((tm * SLOT_ROWS, LANES), lambda i, c, s: (i, 0))],
            out_specs=pl.BlockSpec(memory_space=pl.ANY),
            scratch_shapes=[pltpu.SMEM((TOPK * tm,), I32), pltpu.VMEM((bm // 2 * SLOT_ROWS, LANES), U32),
                            pltpu.SemaphoreType.DMA, pltpu.SemaphoreType.DMA, pltpu.SemaphoreType.DMA]),
        out_shape=jax.ShapeDtypeStruct((P * SLOT_ROWS, LANES), U32),
        compiler_params=pltpu.CompilerParams(dimension_semantics=("arbitrary",)),
        name="dispatch",
    )(counts, pad_start, dest_tiles, x1t)


def _expert_kernel(blk_e_ref, n_used_ref, x_ref, wgu_ref, wdn_ref, y_ref):
    del blk_e_ref
    b = pl.program_id(0)

    @pl.when(b < n_used_ref[0])
    def _():
        gu = _mm(_unpack_tokens(x_ref), wgu_ref[0])
        gate, up = gu[:, :DE], gu[:, DE:]
        _pack_tokens(y_ref, _mm(gate * _sigmoid(gate) * up, wdn_ref[0]))

    @pl.when(b >= n_used_ref[0])
    def _():
        y_ref[...] = jnp.zeros(y_ref.shape, U32)


def _experts(blk_e, n_used, xs, wgu, wdn, bm):
    n_blocks = xs.shape[0] // (bm * SLOT_ROWS)
    live = lambda b, n: jnp.minimum(b, n[0] - 1)
    return pl.pallas_call(
        _expert_kernel,
        grid_spec=pltpu.PrefetchScalarGridSpec(
            num_scalar_prefetch=2, grid=(n_blocks,),
            in_specs=[pl.BlockSpec((bm * SLOT_ROWS, LANES), lambda b, e, n: (live(b, n), 0)),
                      pl.BlockSpec((1, D, 2 * DE), lambda b, e, n: (e[live(b, n)], 0, 0)),
                      pl.BlockSpec((1, DE, D), lambda b, e, n: (e[live(b, n)], 0, 0))],
            out_specs=pl.BlockSpec((bm * SLOT_ROWS, LANES), lambda b, e, n: (b, 0))),
        out_shape=jax.ShapeDtypeStruct(xs.shape, U32),
        compiler_params=pltpu.CompilerParams(dimension_semantics=("parallel",)),
        name="experts",
    )(blk_e, n_used, xs, wgu, wdn)


def _combine_kernel(dest_hbm, y_hbm, x_ref, w_ref, wsg_ref, wsd_ref, lg_ref, lb_ref, o_ref, dest_smem, rows_ref,
                    sem_idx, sem, *, tm, alpha, tile0):
    i = pl.program_id(0) + tile0
    cp = pltpu.make_async_copy(dest_hbm.at[i, 0], dest_smem, sem_idx)
    cp.start()
    cp.wait()

    def body(r, c):
        for k in range(TOPK):
            pltpu.make_async_copy(_tile_rows(y_hbm, dest_smem[k * tm + r]), _tile_rows(rows_ref.at[k], r), sem).start(
                priority=k % 2)
        return c

    lax.fori_loop(0, tm, body, 0)
    x = x_ref[...]
    gu = _mm(x, wsg_ref[...])
    gate, up = gu[:, :DE], gu[:, DE:]
    shared = _mm(gate * _sigmoid(gate) * up, wsd_ref[...])
    for k in range(TOPK):
        pltpu.make_async_copy(_tile_rows(y_hbm, 0, tm), rows_ref.at[k], sem).wait()
    w = w_ref[...]
    routed = _unpack_tokens(rows_ref.at[0]) * w[:, 0:1]
    for k in range(1, TOPK):
        routed = routed + _unpack_tokens(rows_ref.at[k]) * w[:, k:k + 1]
    o_ref[...] = _layer_norm(alpha * x + (routed + shared), lg_ref[...], lb_ref[...])


def _combine(dest_tiles, ys, x1, w_tok, wsg, wsd, lg, lb, alpha, tm, tile0=0, n_tiles=None):
    n_tiles = x1.shape[0] // tm if n_tiles is None else n_tiles
    full = lambda a: pl.BlockSpec(a.shape, lambda i: (0,) * a.ndim)
    return pl.pallas_call(
        functools.partial(_combine_kernel, tm=tm, alpha=alpha, tile0=tile0),
        grid=(n_tiles,),
        in_specs=[pl.BlockSpec(memory_space=pl.ANY), pl.BlockSpec(memory_space=pl.ANY),
                  pl.BlockSpec((tm, D), lambda i: (i + tile0, 0)), pl.BlockSpec((tm, TOPK), lambda i: (i + tile0, 0))]
        + [full(a) for a in (wsg, wsd, lg, lb)],
        out_specs=pl.BlockSpec((tm, D), lambda i: (i, 0)),
        out_shape=jax.ShapeDtypeStruct((n_tiles * tm, D), F32),
        scratch_shapes=[pltpu.SMEM((TOPK * tm,), I32), pltpu.VMEM((TOPK, tm * SLOT_ROWS, LANES), U32),
                        pltpu.SemaphoreType.DMA, pltpu.SemaphoreType.DMA],
        compiler_params=pltpu.CompilerParams(dimension_semantics=("arbitrary",)),
        name="combine",
    )(dest_tiles, ys, x1, w_tok, wsg, wsd, lg, lb)


def _regroup_w_in(w, b):
    offs = np.cumsum([0, 512, 128, 128, 384, 192, 192, 512, 512, 512, 512, 16, 3 * D])
    qa, ka, va, qb, kb, vb, qc, kc, vc, oc, gc, gates = (int(o) for o in offs[:-1])
    pieces = [(qa, ka), (ka, va), (va, qb)]
    for gi in range(3):
        pieces += [(qb + gi * 128, qb + (gi + 1) * 128), (kb + gi * HD, kb + (gi + 1) * HD),
                   (vb + gi * HD, vb + (gi + 1) * HD)]
    pieces += [(qc, kc), (kc, vc), (vc, oc), (oc, gc), (gc, gates), None]
    gc_pad = (N_UNITS - U_GC) * LANES - (gates - gc)

    def build(a):
        parts = [jnp.zeros(a.shape[:-1] + (gc_pad,), a.dtype) if p is None else a[..., p[0]:p[1]] for p in pieces]
        return jnp.concatenate(parts, axis=-1)

    return (build(w).astype(BF16), build(b).reshape(1, NPAD), w[:, gates:].astype(BF16), b[gates:].reshape(1, 3 * D))


def _trunk(x, seq_lens, depth, params, *, bm, tm_proj, tn_proj, tm_merge, tm_route, tm_moe, split=None):
    (w_in, b_in, sink, conv_w, w_br_a, w_br_b, w_br_c, w_out, ln1_g, ln1_b, w_router, router_bias, w_exp_gu,
     w_exp_down, w_sh_gu, w_sh_down, ln2_g, ln2_b) = params
    T = x.shape[0]
    alpha = float((2 * depth) ** 0.25)
    slopes_a = tuple(float(2.0 ** (-8.0 * i / 8)) for i in range(1, 9))
    slopes_b = tuple(float(2.0 ** (-8.0 * i / 6)) for i in range(1, 7))
    flags1 = jnp.asarray(_edge_flags(seq_lens, QB))
    flags_b = [jnp.asarray(_edge_flags(seq_lens, QB * d)) for d in B_DILS]
    A = T * TOPK
    n_blocks = A // bm + NE
    P = n_blocks * bm
    row2 = lambda a: a.reshape(1, -1)
    for l in range(depth):
        w_all, b_all, w_gates, b_gates = _regroup_w_in(w_in[l], b_in[l])
        proj = _proj(x, w_all, b_all, tm_proj, tn_proj)
        oa = _attn_a(proj, flags1, sink[l].reshape(1, A_HK * A_G), slopes_a)
        obs, lses = [], []
        for gi in range(3):
            o, s = _attn_b(proj, flags_b[gi], gi, slopes_b[gi * B_G:(gi + 1) * B_G])
            obs.append(o)
            lses.append(s)
        gct = proj[:, U_GC * LANES:U_GC * LANES + 16].T
        hf, hb = _mlstm(proj, gct, conv_w[l], flags1)
        x1, x1t = _merge(x, proj, oa, obs, lses, hf, hb, w_gates, b_gates, w_br_a[l].astype(BF16), w_br_b[l].astype(BF16),
                    w_br_c[l].astype(BF16), w_out[l].astype(BF16), row2(ln1_g[l]), row2(ln1_b[l]), alpha, tm_merge)
        top_idx, top_w, rank, counts = _route(x1, w_router[l].T, router_bias[l].reshape(NE, 1), tm_route)
        counts = counts[:, 0]
        padded = (counts + bm - 1) // bm * bm
        pad_end = jnp.cumsum(padded)
        pad_start = pad_end - padded
        dest = _dest_tiles(top_idx, rank, pad_start, tm_moe)
        blk_e = jnp.minimum(jnp.searchsorted(pad_end, jnp.arange(n_blocks, dtype=I32) * bm, side='right'),
                            NE - 1).astype(I32)
        n_used = (pad_end[-1:] // bm).astype(I32)
        xs = _dispatch(counts, pad_start, dest, x1t, P, tm_moe, bm)
        ys = _experts(blk_e, n_used, xs, w_exp_gu[l].astype(BF16), w_exp_down[l].astype(BF16), bm)
        comb = functools.partial(_combine, dest, ys, x1, top_w.T, w_sh_gu[l].astype(BF16), w_sh_down[l].astype(BF16),
                                 row2(ln2_g[l]), row2(ln2_b[l]), alpha, tm_moe)
        if l + 1 < depth or split is None:
            x = comb()
        else:
            return comb(0, split // tm_moe), comb(split // tm_moe, (T - split) // tm_moe)
    return x


def kernel(x_prompt, x_sample, w_in, b_in, sink, conv_w, w_br_a, w_br_b, w_br_c, w_out, ln1_g, ln1_b, w_router,
           router_bias, w_exp_gu, w_exp_down, w_sh_gu, w_sh_down, ln2_g, ln2_b):
    depth = w_in.shape[0]
    n1, s1, _ = x_prompt.shape
    n2, s2, _ = x_sample.shape
    span = QB * max(B_DILS)
    assert s1 % span == 0 and s2 % span == 0, "sequence lengths must be multiples of the widest dilated block"
    x = jnp.concatenate([x_prompt.reshape(n1 * s1, D), x_sample.reshape(n2 * s2, D)], axis=0)
    params = (w_in, b_in, sink, conv_w, w_br_a, w_br_b, w_br_c, w_out, ln1_g, ln1_b, w_router, router_bias,
              w_exp_gu, w_exp_down, w_sh_gu, w_sh_down, ln2_g, ln2_b)
    y1, y2 = _trunk(x, [s1] * n1 + [s2] * n2, depth, params, bm=512, tm_proj=1024, tn_proj=512, tm_merge=256,
                    tm_route=256, tm_moe=512, split=n1 * s1)
    return y1.reshape(n1, s1, D), y2.reshape(n2, s2, D)
```

```python
import functools

import numpy as np
import jax
import jax.numpy as jnp
from jax import lax
from jax.experimental import pallas as pl
from jax.experimental.pallas import tpu as pltpu

F32, BF16, I32, U32 = jnp.float32, jnp.bfloat16, jnp.int32, jnp.uint32

LANES = 128
D = 1024
HD = 64
A_HK, A_G, A_BAND = 2, 4, 128
B_G, B_BAND = 2, 64
B_DILS = (1, 4, 16)
C_H, C_HD, CHUNK = 4, 128, 128
NE, TOPK, NGRP, TOPG, DE = 256, 8, 8, 4, 256
EPG = NE // NGRP
ROUTED_SCALE = 2.5
LN_EPS = 1e-5
NEG = -1e30
QB = 128
SLOT_ROWS = D // (2 * LANES)

U_QA, U_KA, U_VA = 0, 4, 5
U_QB = (6, 8, 10)
U_KVB = (7, 9, 11)
U_QC, U_KC, U_VC, U_OC, U_GC = 12, 16, 20, 24, 28
N_UNITS = 32
NPAD = N_UNITS * LANES


def _edge_flags(seq_lens, rows):
    first, last = [], []
    for s in seq_lens:
        nb = s // rows
        first += [1] + [0] * (nb - 1)
        last += [0] * (nb - 1) + [1]
    return np.array([first, last], np.int32)


def _mm(a, b):
    return jnp.dot(a.astype(BF16), b.astype(BF16), preferred_element_type=F32)


def _mm_nt(a, b):
    return lax.dot_general(a.astype(BF16), b.astype(BF16), (((1,), (1,)), ((), ())), preferred_element_type=F32)


def _mm_tn(a, b):
    return lax.dot_general(a.astype(BF16), b.astype(BF16), (((0,), (0,)), ((), ())), preferred_element_type=F32)


def _split3(x):
    hi = x.astype(BF16)
    r1 = x - hi.astype(F32)
    mid = r1.astype(BF16)
    lo = (r1 - mid.astype(F32)).astype(BF16)
    return hi, mid, lo


def _sigmoid(x):
    return 1.0 / (1.0 + jnp.exp(-x))


def _log_sigmoid(x):
    return jnp.minimum(x, 0.0) - jnp.log(1.0 + jnp.exp(-jnp.abs(x)))


def _pack_tokens(ref, x):
    n = x.shape[0]
    u = lax.bitcast_convert_type(x, U32)
    u = u + (jnp.uint32(0x7FFF) + ((u >> 16) & jnp.uint32(1)))
    for j in range(SLOT_ROWS):
        lo = u[:, j * LANES:(j + 1) * LANES] >> 16
        hi = u[:, D // 2 + j * LANES:D // 2 + (j + 1) * LANES] & jnp.uint32(0xFFFF0000)
        ref[pl.ds(j, n, stride=SLOT_ROWS), :] = lo | hi


def _unpack_tokens(ref):
    n = ref.shape[0] // SLOT_ROWS
    words = [ref[pl.ds(j, n, stride=SLOT_ROWS), :] for j in range(SLOT_ROWS)]
    lo = [lax.bitcast_convert_type(w << 16, F32) for w in words]
    hi = [lax.bitcast_convert_type(w & jnp.uint32(0xFFFF0000), F32) for w in words]
    return jnp.concatenate(lo + hi, axis=1)


def _layer_norm(x, g, b):
    mu = jnp.mean(x, axis=-1, keepdims=True)
    xc = x - mu
    var = jnp.mean(xc * xc, axis=-1, keepdims=True)
    return xc * lax.rsqrt(var + LN_EPS) * g + b


def _proj_kernel(x_ref, w_ref, b_ref, o_ref, xb_ref):
    @pl.when(pl.program_id(1) == 0)
    def _():
        xb_ref[...] = x_ref[...].astype(BF16)

    o_ref[...] = jnp.dot(xb_ref[...], w_ref[...], preferred_element_type=F32) + b_ref[...]


def _proj(x, w, b, tm, tn):
    T, N = x.shape[0], w.shape[1]
    return pl.pallas_call(
        _proj_kernel,
        grid=(T // tm, N // tn),
        in_specs=[pl.BlockSpec((tm, D), lambda i, j: (i, 0)),
                  pl.BlockSpec((D, tn), lambda i, j: (0, j)),
                  pl.BlockSpec((1, tn), lambda i, j: (0, j))],
        out_specs=pl.BlockSpec((tm, tn), lambda i, j: (i, j)),
        out_shape=jax.ShapeDtypeStruct((T, N), F32),
        scratch_shapes=[pltpu.VMEM((tm, D), BF16)],
        compiler_params=pltpu.CompilerParams(dimension_semantics=("parallel", "arbitrary")),
        name="proj",
    )(x, w, b)


def _attn_bias(slopes, g, band, dist):
    rel = np.abs(np.arange(3 * QB)[None, :] - QB - np.arange(QB)[:, None])
    dist_f = (rel * dist).astype(np.float32)
    per_head = [np.where(rel <= band, -(np.float32(sl) * dist_f), np.float32(NEG)) for sl in slopes]
    return np.stack([np.concatenate(per_head[h * g:(h + 1) * g], axis=0) for h in range(len(slopes) // g)])


def _attn_tile(q, kwin, vwin, bias_ref, lo, hi, sinks, *, hk, g, packed):
    qb, W = q.shape[0], kwin.shape[0]
    rows = g * qb
    one_col = jnp.where(lax.broadcasted_iota(I32, (rows, HD), 1) == 0, 1.0, 0.0)
    key = lax.broadcasted_iota(I32, (W, HD), 0)
    edge = jnp.where((key >= lo) & (key < hi), 0.0, NEG)
    row_g = jnp.right_shift(lax.broadcasted_iota(I32, (rows, 1), 0), qb.bit_length() - 1)
    outs, lses = [], []
    for h in range(hk):
        qs = jnp.concatenate([q[:, (h * g + gg) * HD:(h * g + gg + 1) * HD] for gg in range(g)], axis=0)
        qs = jnp.concatenate([qs * (HD ** -0.5), one_col], axis=1)
        ko = h * HD
        vo = HD if packed else h * HD
        kh = jnp.concatenate([kwin[:, ko:ko + HD], edge], axis=1)
        vh = vwin[:, vo:vo + HD]
        logits = _mm_nt(qs, kh) + bias_ref[h]
        m = jnp.max(logits, axis=-1, keepdims=True)
        if sinks is not None:
            sink_col = jnp.zeros((rows, 1), F32)
            for gg in range(g):
                sink_col = jnp.where(row_g == gg, sinks[h * g + gg], sink_col)
            m = jnp.maximum(m, sink_col)
        p = jnp.exp(logits - m)
        den = jnp.sum(p, axis=-1, keepdims=True)
        if sinks is not None:
            den = den + jnp.exp(sink_col - m)
        o = _mm(p, vh) / den
        for gg in range(g):
            outs.append(o[gg * qb:(gg + 1) * qb])
        if packed:
            l = m + jnp.log(den)
            for gg in range(g):
                lses.append(jnp.broadcast_to(l[gg * qb:(gg + 1) * qb], (qb, HD)))
    return outs, lses


def _edge_cols(flags_ref, b):
    return jnp.where(flags_ref[0, b] == 0, 0, QB), jnp.where(flags_ref[1, b] == 0, 3 * QB, 2 * QB)


def _attn_ab0_kernel(flags_ref, sink_ref, qa_ref, kp_ref, ko_ref, kn_ref, vp_ref, vo_ref, vn_ref, qb_ref, kvp_ref,
                     kvo_ref, kvn_ref, bias_a_ref, bias_b_ref, oa_ref, ob_ref, lse_ref):
    lo, hi = _edge_cols(flags_ref, pl.program_id(0))
    sinks = [sink_ref[0, i] for i in range(A_HK * A_G)]
    kwin = jnp.concatenate([kp_ref[...], ko_ref[...], kn_ref[...]], axis=0)
    vwin = jnp.concatenate([vp_ref[...], vo_ref[...], vn_ref[...]], axis=0)
    outs, _ = _attn_tile(qa_ref[...], kwin, vwin, bias_a_ref, lo, hi, sinks, hk=A_HK, g=A_G, packed=False)
    oa_ref[...] = jnp.concatenate(outs, axis=1).astype(oa_ref.dtype)
    kv = jnp.concatenate([kvp_ref[...], kvo_ref[...], kvn_ref[...]], axis=0)
    outs, lses = _attn_tile(qb_ref[...], kv, kv, bias_b_ref, lo, hi, None, hk=1, g=B_G, packed=True)
    ob_ref[...] = jnp.concatenate(outs, axis=1)
    lse_ref[...] = jnp.concatenate(lses, axis=1)


def _attn_ab0(proj, flags, sink, slopes_a, slopes_b0):
    T = proj.shape[0]
    nb = T // QB
    prev = lambda b, f: jnp.maximum(b - 1, 0)
    nxt = lambda b, f: jnp.minimum(b + 1, nb - 1)
    kspecs = lambda u: [pl.BlockSpec((QB, LANES), lambda b, f: (prev(b, f), u)),
                        pl.BlockSpec((QB, LANES), lambda b, f: (b, u)),
                        pl.BlockSpec((QB, LANES), lambda b, f: (nxt(b, f), u))]
    bias_a = jnp.asarray(_attn_bias(slopes_a, A_G, A_BAND, 1))
    bias_b = jnp.asarray(_attn_bias(slopes_b0, B_G, B_BAND, B_DILS[0]))
    full = lambda a: pl.BlockSpec(a.shape, lambda b, f: (0,) * a.ndim)
    wa = A_HK * A_G * HD
    return pl.pallas_call(
        _attn_ab0_kernel,
        grid_spec=pltpu.PrefetchScalarGridSpec(
            num_scalar_prefetch=1, grid=(nb,),
            in_specs=[pl.BlockSpec(memory_space=pltpu.SMEM), pl.BlockSpec((QB, wa), lambda b, f: (b, U_QA // 4))]
            + kspecs(U_KA) + kspecs(U_VA) + [pl.BlockSpec((QB, LANES), lambda b, f: (b, U_QB[0]))] + kspecs(U_KVB[0])
            + [full(bias_a), full(bias_b)],
            out_specs=[pl.BlockSpec((QB, wa), lambda b, f: (b, 0)), pl.BlockSpec((QB, LANES), lambda b, f: (b, 0)),
                       pl.BlockSpec((QB, LANES), lambda b, f: (b, 0))]),
        out_shape=[jax.ShapeDtypeStruct((T, wa), BF16), jax.ShapeDtypeStruct((T, LANES), F32),
                   jax.ShapeDtypeStruct((T, LANES), F32)],
        compiler_params=pltpu.CompilerParams(dimension_semantics=("parallel",)),
        name="attn_ab0",
    )(flags, sink, *([proj] * 11), bias_a, bias_b)


def _attn_b_kernel(flags_ref, q_ref, kp_ref, ko_ref, kn_ref, bias_ref, o_ref, lse_ref, *, dil):
    lo, hi = _edge_cols(flags_ref, pl.program_id(0))

    def body(r, c):
        sub = lambda ref: ref[pl.ds(r, QB, stride=dil), :]
        kv = jnp.concatenate([sub(kp_ref), sub(ko_ref), sub(kn_ref)], axis=0)
        outs, lses = _attn_tile(sub(q_ref), kv, kv, bias_ref, lo, hi, None, hk=1, g=B_G, packed=True)
        o_ref[pl.ds(r, QB, stride=dil), :] = jnp.concatenate(outs, axis=1)
        lse_ref[pl.ds(r, QB, stride=dil), :] = jnp.concatenate(lses, axis=1)
        return c

    lax.fori_loop(0, dil, body, 0, unroll=2)


def _attn_b(proj, flags, gi, slopes):
    dil = B_DILS[gi]
    T = proj.shape[0]
    rows = QB * dil
    nb = T // rows
    uq, ukv = U_QB[gi], U_KVB[gi]
    bias = jnp.asarray(_attn_bias(slopes, B_G, B_BAND, dil))
    return pl.pallas_call(
        functools.partial(_attn_b_kernel, dil=dil),
        grid_spec=pltpu.PrefetchScalarGridSpec(
            num_scalar_prefetch=1, grid=(nb,),
            in_specs=[pl.BlockSpec((rows, LANES), lambda b, f: (b, uq)),
                      pl.BlockSpec((rows, LANES), lambda b, f: (jnp.maximum(b - 1, 0), ukv)),
                      pl.BlockSpec((rows, LANES), lambda b, f: (b, ukv)),
                      pl.BlockSpec((rows, LANES), lambda b, f: (jnp.minimum(b + 1, nb - 1), ukv)),
                      pl.BlockSpec(bias.shape, lambda b, f: (0, 0, 0))],
            out_specs=[pl.BlockSpec((rows, LANES), lambda b, f: (b, 0)),
                       pl.BlockSpec((rows, LANES), lambda b, f: (b, 0))]),
        out_shape=[jax.ShapeDtypeStruct((T, LANES), F32), jax.ShapeDtypeStruct((T, LANES), F32)],
        compiler_params=pltpu.CompilerParams(dimension_semantics=("parallel",)),
        name=f"attn_b{gi}",
    )(flags, proj, proj, proj, proj, bias)


def _conv_silu(x, xp8, xn8, w, keep_prev, keep_next):
    rowi = lax.broadcasted_iota(I32, (CHUNK, 1), 0)
    x_prev = jnp.where(rowi == 0, xp8[7:8, :] * keep_prev, pltpu.roll(x, 1, 0))
    x_next = jnp.where(rowi == CHUNK - 1, xn8[0:1, :] * keep_next, pltpu.roll(x, CHUNK - 1, 0))
    y = w[0:1, :] * x_prev + w[1:2, :] * x + w[2:3, :] * x_next
    return y * _sigmoid(y)


def _mlstm_dir(rev, first, last, q_ref, qp_ref, qn_ref, k_ref, kp_ref, kn_ref, v_ref, g_ref, gt_ref, cw_ref,
               c_ref, n_ref, m_ref, h_ref):
    sd = 1 if rev else 0
    keep_prev = (1 - first).astype(F32)
    keep_next = (1 - last).astype(F32)

    @pl.when((last if rev else first) == 1)
    def _():
        c_ref[sd] = jnp.zeros(c_ref.shape[1:], F32)
        n_ref[sd] = jnp.zeros(n_ref.shape[1:], F32)
        m_ref[sd] = jnp.zeros(m_ref.shape[1:], F32)

    W = C_H * C_HD
    q = _conv_silu(q_ref[...], qp_ref[...], qn_ref[...], cw_ref[:, :W], keep_prev, keep_next)
    k = _conv_silu(k_ref[...], kp_ref[...], kn_ref[...], cw_ref[:, W:], keep_prev, keep_next) * (C_HD ** -0.5)
    v = v_ref[...]
    G = g_ref[...]
    GT = gt_ref[...]
    ti = lax.broadcasted_iota(I32, (CHUNK, CHUNK), 0)
    si = lax.broadcasted_iota(I32, (CHUNK, CHUNK), 1)
    tri = (si >= ti) if rev else (si <= ti)
    L = jnp.where(tri, 1.0, 0.0).astype(BF16)
    Lt = jnp.where((ti >= si) if rev else (ti <= si), 1.0, 0.0).astype(BF16)
    cum_col = sum(jnp.dot(L, p, preferred_element_type=F32) for p in _split3(_log_sigmoid(G)))
    cum_row = sum(jnp.dot(p, Lt, preferred_element_type=F32) for p in _split3(_log_sigmoid(GT)))
    e = 0 if rev else CHUNK - 1
    hs = []
    for h in range(C_H):
        ci = (2 * C_H if rev else 0) + h
        cf = ci + C_H
        cum_c, cum_r = cum_col[:, cf:cf + 1], cum_row[cf:cf + 1, :]
        li_c, li_r = G[:, ci:ci + 1], GT[ci:ci + 1, :]
        qh, kh, vh = (a[:, h * C_HD:(h + 1) * C_HD] for a in (q, k, v))
        m11 = m_ref[sd, h][:, :1]
        Dm = jnp.where(tri, cum_c - cum_r + li_r, NEG)
        m_inter = cum_c + m11
        mt = jnp.maximum(jnp.max(Dm, axis=-1, keepdims=True), m_inter)
        Wm = jnp.exp(Dm - mt) * _mm_nt(qh, kh)
        inter = jnp.exp(m_inter - mt)
        num = _mm(Wm, vh) + inter * _mm(qh, c_ref[sd, h])
        den = jnp.sum(Wm, axis=-1, keepdims=True) + inter * jnp.sum(qh * n_ref[sd, h], axis=-1, keepdims=True)
        hs.append(num / jnp.maximum(jnp.abs(den), jnp.exp(-mt)))
        total = cum_c[e:e + 1, :]
        gcol = total - cum_c + li_c
        m_new = jnp.maximum(total + m11, jnp.max(gcol, axis=0, keepdims=True))
        kw = kh * jnp.exp(gcol - m_new)
        decay = jnp.exp(total + m11 - m_new)
        c_ref[sd, h] = decay * c_ref[sd, h] + _mm_tn(kw, vh)
        n_ref[sd, h] = decay * n_ref[sd, h] + jnp.sum(kw, axis=0, keepdims=True)
        m_ref[sd, h] = jnp.broadcast_to(m_new, (1, LANES))
    h_ref[...] = jnp.concatenate(hs, axis=1)


def _mlstm_kernel(flags_ref, *refs, nb):
    fwd, bwd, (cw_ref, hf_ref, hb_ref, c_ref, n_ref, m_ref) = refs[:9], refs[9:18], refs[18:]
    j = pl.program_id(0)
    jb = nb - 1 - j
    _mlstm_dir(False, flags_ref[0, j], flags_ref[1, j], *fwd, cw_ref, c_ref, n_ref, m_ref, hf_ref)
    _mlstm_dir(True, flags_ref[0, jb], flags_ref[1, jb], *bwd, cw_ref, c_ref, n_ref, m_ref, hb_ref)


def _mlstm(proj, gct, conv_w, flags):
    T = proj.shape[0]
    nb = T // CHUNK
    W = C_H * C_HD
    r8 = CHUNK // 8

    def specs(blk):
        halo_p = lambda j, f: jnp.maximum(blk(j) * r8 - 1, 0)
        halo_n = lambda j, f: jnp.minimum(blk(j) * r8 + r8, nb * r8 - 1)
        out = []
        for u in (U_QC, U_KC):
            out += [pl.BlockSpec((CHUNK, W), lambda j, f, u=u: (blk(j), u // 4)),
                    pl.BlockSpec((8, W), lambda j, f, u=u: (halo_p(j, f), u // 4)),
                    pl.BlockSpec((8, W), lambda j, f, u=u: (halo_n(j, f), u // 4))]
        out += [pl.BlockSpec((CHUNK, W), lambda j, f: (blk(j), U_VC // 4)),
                pl.BlockSpec((CHUNK, LANES), lambda j, f: (blk(j), U_GC)),
                pl.BlockSpec((16, CHUNK), lambda j, f: (0, blk(j)))]
        return out

    fwd_blk = lambda j: j
    bwd_blk = lambda j: nb - 1 - j
    args = [proj] * 8 + [gct]
    return pl.pallas_call(
        functools.partial(_mlstm_kernel, nb=nb),
        grid_spec=pltpu.PrefetchScalarGridSpec(
            num_scalar_prefetch=1, grid=(nb,),
            in_specs=specs(fwd_blk) + specs(bwd_blk) + [pl.BlockSpec((3, 2 * W), lambda j, f: (0, 0))],
            out_specs=[pl.BlockSpec((CHUNK, W), lambda j, f: (j, 0)),
                       pl.BlockSpec((CHUNK, W), lambda j, f: (nb - 1 - j, 0))],
            scratch_shapes=[pltpu.VMEM((2, C_H, C_HD, C_HD), F32),
                            pltpu.VMEM((2, C_H, 1, C_HD), F32),
                            pltpu.VMEM((2, C_H, 1, LANES), F32)]),
        out_shape=[jax.ShapeDtypeStruct((T, W), F32), jax.ShapeDtypeStruct((T, W), F32)],
        compiler_params=pltpu.CompilerParams(dimension_semantics=("arbitrary",)),
        name="mlstm",
    )(flags, *args, *args, conv_w)


def _merge_kernel(x_ref, oa_ref, ob0_ref, ob1_ref, ob2_ref, l0_ref, l1_ref, l2_ref, hf_ref, hb_ref, oc_ref,
                  wg_ref, bg_ref, wa_ref, wb_ref, wc_ref, wo_ref, lg_ref, lb_ref, o_ref, ot_ref, *, alpha):
    ya = jnp.dot(oa_ref[...], wa_ref[...], preferred_element_type=F32)
    l0, l1, l2 = l0_ref[...], l1_ref[...], l2_ref[...]
    lm = jnp.maximum(jnp.maximum(l0, l1), l2)
    e0, e1, e2 = jnp.exp(l0 - lm), jnp.exp(l1 - lm), jnp.exp(l2 - lm)
    es = e0 + e1 + e2
    outb = jnp.concatenate([ob0_ref[...] * (e0 / es), ob1_ref[...] * (e1 / es), ob2_ref[...] * (e2 / es)], axis=1)
    yb = _mm(outb, wb_ref[...])
    yc = _mm(_sigmoid(oc_ref[...]) * (hf_ref[...] + hb_ref[...]), wc_ref[...])
    x = x_ref[...]
    gates = _sigmoid(_mm(x, wg_ref[...]) + bg_ref[...])
    merged = gates[:, 0:D] * ya + gates[:, D:2 * D] * yb + gates[:, 2 * D:3 * D] * yc
    mix = _mm(merged, wo_ref[...])
    x1 = _layer_norm(alpha * x + mix, lg_ref[...], lb_ref[...])
    o_ref[...] = x1
    _pack_tokens(ot_ref, x1)


def _merge(x, proj, oa, obs, lses, hf, hb, wg, bg, wa, wb, wc, wo, lg, lb, alpha, tm):
    T = x.shape[0]
    row = lambda w: pl.BlockSpec((tm, w), lambda i: (i, 0))
    full = lambda a: pl.BlockSpec(a.shape, lambda i: (0,) * a.ndim)
    return pl.pallas_call(
        functools.partial(_merge_kernel, alpha=alpha),
        grid=(T // tm,),
        in_specs=[row(D), row(oa.shape[1])] + [row(LANES)] * 6 + [row(hf.shape[1])] * 2
        + [pl.BlockSpec((tm, 512), lambda i: (i, U_OC // 4))]
        + [full(a) for a in (wg, bg, wa, wb, wc, wo, lg, lb)],
        out_specs=[row(D), pl.BlockSpec((tm * SLOT_ROWS, LANES), lambda i: (i, 0))],
        out_shape=[jax.ShapeDtypeStruct((T, D), F32), jax.ShapeDtypeStruct((T * SLOT_ROWS, LANES), U32)],
        compiler_params=pltpu.CompilerParams(dimension_semantics=("parallel",)),
        name="merge",
    )(x, oa, *obs, *lses, hf, hb, proj, wg, bg, wa, wb, wc, wo, lg, lb)


def _stack_rows(rows, iota):
    out = jnp.zeros(iota.shape, F32)
    for k, r in enumerate(rows):
        out = jnp.where(iota == float(k), r, out)
    return out


def _route_kernel(x_ref, wrt_ref, bias_ref, idx_ref, w_ref, rank_ref, cnt_ref, run_ref, *, tm):
    @pl.when(pl.program_id(0) == 0)
    def _():
        run_ref[...] = jnp.zeros(run_ref.shape, F32)

    x = x_ref[...]
    xh, xm, xl = _split3(x)
    wh, wm, wl = _split3(wrt_ref[...])
    nt = lambda a, b: lax.dot_general(a, b, (((1,), (1,)), ((), ())), preferred_element_type=F32)
    logits = nt(wh, xh) + (nt(wh, xm) + nt(wm, xh)) + (nt(wh, xl) + nt(wl, xh) + nt(wm, xm))
    scores = _sigmoid(logits)
    biased = scores + bias_ref[...]
    ninf = -jnp.inf
    iota_g = lax.broadcasted_iota(I32, (EPG, tm), 0).astype(F32)
    gs = []
    for gi in range(NGRP):
        v = biased[gi * EPG:(gi + 1) * EPG]
        t1 = jnp.max(v, axis=0, keepdims=True)
        i1 = jnp.min(jnp.where(v == t1, iota_g, float(EPG)), axis=0, keepdims=True)
        t2 = jnp.max(jnp.where(iota_g == i1, ninf, v), axis=0, keepdims=True)
        gs.append(t1 + t2)
    iota8 = lax.broadcasted_iota(I32, (NGRP, tm), 0).astype(F32)
    gs = _stack_rows(gs, iota8)
    gsel = jnp.zeros((NGRP, tm), F32)
    for _ in range(TOPG):
        mx = jnp.max(gs, axis=0, keepdims=True)
        ix = jnp.min(jnp.where(gs == mx, iota8, float(NGRP)), axis=0, keepdims=True)
        hit = iota8 == ix
        gsel = jnp.where(hit, 1.0, gsel)
        gs = jnp.where(hit, ninf, gs)
    v = jnp.concatenate([jnp.where(gsel[gi:gi + 1] > 0.0, biased[gi * EPG:(gi + 1) * EPG], ninf)
                         for gi in range(NGRP)], axis=0)
    iota_e = lax.broadcasted_iota(I32, (NE, tm), 0).astype(F32)
    sel = jnp.zeros((NE, tm), F32)
    idxs, svals = [], []
    for _ in range(TOPK):
        mx = jnp.max(v, axis=0, keepdims=True)
        ix = jnp.min(jnp.where(v == mx, iota_e, float(NE)), axis=0, keepdims=True)
        hit = iota_e == ix
        idxs.append(ix)
        svals.append(jnp.sum(jnp.where(hit, scores, 0.0), axis=0, keepdims=True))
        sel = jnp.where(hit, 1.0, sel)
        v = jnp.where(hit, ninf, v)
    top_s = _stack_rows(svals, iota8)
    idx_ref[...] = _stack_rows(idxs, iota8).astype(I32)
    w_ref[...] = top_s / jnp.sum(top_s, axis=0, keepdims=True) * ROUTED_SCALE
    ti = lax.broadcasted_iota(I32, (tm, tm), 0)
    si = lax.broadcasted_iota(I32, (tm, tm), 1)
    before = jnp.where(ti < si, 1.0, 0.0).astype(BF16)
    selb = sel.astype(BF16)
    cnt = run_ref[...] + jnp.dot(selb, before, preferred_element_type=F32)
    rank_ref[...] = _stack_rows(
        [jnp.sum(jnp.where(iota_e == ix, cnt, 0.0), axis=0, keepdims=True) for ix in idxs], iota8).astype(I32)
    run = run_ref[...] + jnp.dot(selb, jnp.ones((tm, tm), BF16), preferred_element_type=F32)
    run_ref[...] = run
    cnt_ref[...] = run.astype(I32)


def _route(x1, wrt, bias_col, tm):
    T = x1.shape[0]
    slot = lambda dt: jax.ShapeDtypeStruct((TOPK, T), dt)
    return pl.pallas_call(
        functools.partial(_route_kernel, tm=tm),
        grid=(T // tm,),
        in_specs=[pl.BlockSpec((tm, D), lambda i: (i, 0)),
                  pl.BlockSpec((NE, D), lambda i: (0, 0)),
                  pl.BlockSpec((NE, 1), lambda i: (0, 0))],
        out_specs=[pl.BlockSpec((TOPK, tm), lambda i: (0, i))] * 3 + [pl.BlockSpec((NE, tm), lambda i: (0, 0))],
        out_shape=[slot(I32), slot(F32), slot(I32), jax.ShapeDtypeStruct((NE, tm), I32)],
        scratch_shapes=[pltpu.VMEM((NE, tm), F32)],
        compiler_params=pltpu.CompilerParams(dimension_semantics=("arbitrary",)),
        name="route",
    )(x1, wrt, bias_col)


def _dest_kernel(idx_ref, rank_ref, start_ref, o_ref, *, tm):
    iota_e = lax.broadcasted_iota(I32, (NE, tm), 0)
    starts = start_ref[...]
    rows = []
    for k in range(TOPK):
        base = jnp.sum(jnp.where(iota_e == idx_ref[k:k + 1, :], starts, 0.0), axis=0, keepdims=True)
        rows.append(base.astype(I32) + rank_ref[k:k + 1, :])
    o_ref[0] = jnp.concatenate(rows, axis=1)


def _dest_tiles(top_idx, rank, pad_start, tm):
    T = top_idx.shape[1]
    return pl.pallas_call(
        functools.partial(_dest_kernel, tm=tm),
        grid=(T // tm,),
        in_specs=[pl.BlockSpec((TOPK, tm), lambda i: (0, i)), pl.BlockSpec((TOPK, tm), lambda i: (0, i)),
                  pl.BlockSpec((NE, 1), lambda i: (0, 0))],
        out_specs=pl.BlockSpec((1, 1, TOPK * tm), lambda i: (i, 0, 0)),
        out_shape=jax.ShapeDtypeStruct((T // tm, 1, TOPK * tm), I32),
        compiler_params=pltpu.CompilerParams(dimension_semantics=("parallel",)),
        name="dest",
    )(top_idx, rank, pad_start.astype(F32).reshape(NE, 1))


def _tile_rows(ref, slot, n=1):
    return ref.at[pl.ds(pl.multiple_of(slot * SLOT_ROWS, SLOT_ROWS), n * SLOT_ROWS)]


def _dispatch_kernel(cnt_ref, start_ref, dest_hbm, x_ref, xs_hbm, dest_smem, zero_ref, sem_idx, sem, sem_z, *, tm, bm):
    i = pl.program_id(0)
    pad_sizes = [1 << k for k in range(bm.bit_length() - 1)]
    half = bm // 2

    def pad_copies(e, fn):
        n = (-cnt_ref[e]) & (bm - 1)
        at = start_ref[e] + cnt_ref[e]
        for sz in pad_sizes:
            @pl.when((n & sz) != 0)
            def _(at=at, sz=sz):
                fn(pltpu.make_async_copy(_tile_rows(zero_ref, 0, sz), _tile_rows(xs_hbm, at, sz), sem_z))
            at = at + (n & sz)

    @pl.when(i == 0)
    def _():
        zero_ref[...] = jnp.zeros(zero_ref.shape, U32)

        def start(e, c):
            pad_copies(e, lambda cp: cp.start())
            return c

        def wait(e, c):
            pad_copies(e, lambda cp: cp.wait())
            return c

        lax.fori_loop(0, NE, start, 0)
        lax.fori_loop(0, NE, wait, 0)

        used = start_ref[NE - 1] + cnt_ref[NE - 1] + ((-cnt_ref[NE - 1]) & (bm - 1))
        tail = lambda c: pltpu.make_async_copy(zero_ref, _tile_rows(xs_hbm, used + c * half, half), sem_z)

        def tail_start(c, z):
            tail(c).start()
            return z

        def tail_wait(c, z):
            tail(c).wait()
            return z

        n_tail = (xs_hbm.shape[0] // SLOT_ROWS - used) // half
        lax.fori_loop(0, n_tail, tail_start, 0)
        lax.fori_loop(0, n_tail, tail_wait, 0)

    cp = pltpu.make_async_copy(dest_hbm.at[i, 0], dest_smem, sem_idx)
    cp.start()
    cp.wait()

    def body(r, c):
        for k in range(TOPK):
            pltpu.make_async_copy(_tile_rows(x_ref, r), _tile_rows(xs_hbm, dest_smem[k * tm + r]), sem).start(
                priority=k % 2)
        return c

    lax.fori_loop(0, tm, body, 0)
    for k in range(TOPK):
        pltpu.make_async_copy(x_ref, _tile_rows(xs_hbm, 0, tm), sem).wait()


def _dispatch(counts, pad_start, dest_tiles, x1t, P, tm, bm):
    T = x1t.shape[0] // SLOT_ROWS
    return pl.pallas_call(
        functools.partial(_dispatch_kernel, tm=tm, bm=bm),
        grid_spec=pltpu.PrefetchScalarGridSpec(
            num_scalar_prefetch=2, grid=(T // tm,),
            in_specs=[pl.BlockSpec(memory_space=pl.ANY),
                      pl.BlockSpec((tm * SLOT_ROWS, LANES), lambda i, c, s: (i, 0))],
            out_specs=pl.BlockSpec(memory_space=pl.ANY),
            scratch_shapes=[pltpu.SMEM((TOPK * tm,), I32), pltpu.VMEM((bm // 2 * SLOT_ROWS, LANES), U32),
                            pltpu.SemaphoreType.DMA, pltpu.SemaphoreType.DMA, pltpu.SemaphoreType.DMA]),
        out_shape=jax.ShapeDtypeStruct((P * SLOT_ROWS, LANES), U32),
        compiler_params=pltpu.CompilerParams(dimension_semantics=("arbitrary",)),
        name="dispatch",
    )(counts, pad_start, dest_tiles, x1t)


def _expert_kernel(blk_e_ref, n_used_ref, x_ref, wgu_ref, wdn_ref, y_ref, wgu_b, wdn_b):
    b = pl.program_id(0)
    live = b < n_used_ref[0]

    @pl.when(live & ((b == 0) | (blk_e_ref[b] != blk_e_ref[jnp.maximum(b - 1, 0)])))
    def _():
        wgu_b[...] = wgu_ref[0, 0].astype(BF16)
        wdn_b[...] = wdn_ref[0, 0].astype(BF16)

    @pl.when(live)
    def _():
        gu = _mm(_unpack_tokens(x_ref), wgu_b[...])
        gate, up = gu[:, :DE], gu[:, DE:]
        _pack_tokens(y_ref, _mm(gate * _sigmoid(gate) * up, wdn_b[...]))

    @pl.when(b >= n_used_ref[0])
    def _():
        y_ref[...] = jnp.zeros(y_ref.shape, U32)


def _experts(blk_e, n_used, xs, wgu, wdn, layer, bm):
    n_blocks = xs.shape[0] // (bm * SLOT_ROWS)
    live = lambda b, n: jnp.minimum(b, n[0] - 1)
    return pl.pallas_call(
        _expert_kernel,
        grid_spec=pltpu.PrefetchScalarGridSpec(
            num_scalar_prefetch=2, grid=(n_blocks,),
            in_specs=[pl.BlockSpec((bm * SLOT_ROWS, LANES), lambda b, e, n: (live(b, n), 0)),
                      pl.BlockSpec((1, 1, D, 2 * DE), lambda b, e, n: (layer, e[live(b, n)], 0, 0)),
                      pl.BlockSpec((1, 1, DE, D), lambda b, e, n: (layer, e[live(b, n)], 0, 0))],
            out_specs=pl.BlockSpec((bm * SLOT_ROWS, LANES), lambda b, e, n: (b, 0)),
            scratch_shapes=[pltpu.VMEM((D, 2 * DE), BF16), pltpu.VMEM((DE, D), BF16)]),
        out_shape=jax.ShapeDtypeStruct(xs.shape, U32),
        compiler_params=pltpu.CompilerParams(dimension_semantics=("arbitrary",)),
        name="experts",
    )(blk_e, n_used, xs, wgu, wdn)


def _combine_kernel(dest_hbm, y_hbm, x_ref, w_ref, wsg_ref, wsd_ref, lg_ref, lb_ref, o_ref, dest_smem, rows_ref,
                    sem_idx, sem, *, tm, alpha, tile0):
    i = pl.program_id(0) + tile0
    cp = pltpu.make_async_copy(dest_hbm.at[i, 0], dest_smem, sem_idx)
    cp.start()
    cp.wait()

    def body(r, c):
        for k in range(TOPK):
            pltpu.make_async_copy(_tile_rows(y_hbm, dest_smem[k * tm + r]), _tile_rows(rows_ref.at[k], r), sem).start(
                priority=k % 2)
        return c

    lax.fori_loop(0, tm, body, 0)
    x = x_ref[...]
    gu = _mm(x, wsg_ref[...])
    gate, up = gu[:, :DE], gu[:, DE:]
    shared = _mm(gate * _sigmoid(gate) * up, wsd_ref[...])
    for k in range(TOPK):
        pltpu.make_async_copy(_tile_rows(y_hbm, 0, tm), rows_ref.at[k], sem).wait()
    w = w_ref[...]
    routed = _unpack_tokens(rows_ref.at[0]) * w[:, 0:1]
    for k in range(1, TOPK):
        routed = routed + _unpack_tokens(rows_ref.at[k]) * w[:, k:k + 1]
    o_ref[...] = _layer_norm(alpha * x + (routed + shared), lg_ref[...], lb_ref[...])


def _combine(dest_tiles, ys, x1, w_tok, wsg, wsd, lg, lb, alpha, tm, tile0=0, n_tiles=None):
    n_tiles = x1.shape[0] // tm if n_tiles is None else n_tiles
    full = lambda a: pl.BlockSpec(a.shape, lambda i: (0,) * a.ndim)
    return pl.pallas_call(
        functools.partial(_combine_kernel, tm=tm, alpha=alpha, tile0=tile0),
        grid=(n_tiles,),
        in_specs=[pl.BlockSpec(memory_space=pl.ANY), pl.BlockSpec(memory_space=pl.ANY),
                  pl.BlockSpec((tm, D), lambda i: (i + tile0, 0)), pl.BlockSpec((tm, TOPK), lambda i: (i + tile0, 0))]
        + [full(a) for a in (wsg, wsd, lg, lb)],
        out_specs=pl.BlockSpec((tm, D), lambda i: (i, 0)),
        out_shape=jax.ShapeDtypeStruct((n_tiles * tm, D), F32),
        scratch_shapes=[pltpu.SMEM((TOPK * tm,), I32), pltpu.VMEM((TOPK, tm * SLOT_ROWS, LANES), U32),
                        pltpu.SemaphoreType.DMA, pltpu.SemaphoreType.DMA],
        compiler_params=pltpu.CompilerParams(dimension_semantics=("arbitrary",)),
        name="combine",
    )(dest_tiles, ys, x1, w_tok, wsg, wsd, lg, lb)


def _regroup_w_in(w, b):
    offs = np.cumsum([0, 512, 128, 128, 384, 192, 192, 512, 512, 512, 512, 16, 3 * D])
    qa, ka, va, qb, kb, vb, qc, kc, vc, oc, gc, gates = (int(o) for o in offs[:-1])
    pieces = [(qa, ka), (ka, va), (va, qb)]
    for gi in range(3):
        pieces += [(qb + gi * 128, qb + (gi + 1) * 128), (kb + gi * HD, kb + (gi + 1) * HD),
                   (vb + gi * HD, vb + (gi + 1) * HD)]
    pieces += [(qc, kc), (kc, vc), (vc, oc), (oc, gc), (gc, gates), None]
    gc_pad = (N_UNITS - U_GC) * LANES - (gates - gc)

    def build(a):
        parts = [jnp.zeros(a.shape[:-1] + (gc_pad,), a.dtype) if p is None else a[..., p[0]:p[1]] for p in pieces]
        return jnp.concatenate(parts, axis=-1)

    return (build(w).astype(BF16), build(b).reshape(1, NPAD), w[:, gates:].astype(BF16), b[gates:].reshape(1, 3 * D))


def _trunk(x, seq_lens, depth, params, *, bm, tm_proj, tn_proj, tm_merge, tm_route, tm_moe, split=None):
    (w_in, b_in, sink, conv_w, w_br_a, w_br_b, w_br_c, w_out, ln1_g, ln1_b, w_router, router_bias, w_exp_gu,
     w_exp_down, w_sh_gu, w_sh_down, ln2_g, ln2_b) = params
    T = x.shape[0]
    alpha = float((2 * depth) ** 0.25)
    slopes_a = tuple(float(2.0 ** (-8.0 * i / 8)) for i in range(1, 9))
    slopes_b = tuple(float(2.0 ** (-8.0 * i / 6)) for i in range(1, 7))
    flags1 = jnp.asarray(_edge_flags(seq_lens, QB))
    flags_b = [jnp.asarray(_edge_flags(seq_lens, QB * d)) for d in B_DILS]
    A = T * TOPK
    n_blocks = A // bm + NE
    P = n_blocks * bm
    row2 = lambda a: a.reshape(1, -1)
    for l in range(depth):
        w_all, b_all, w_gates, b_gates = _regroup_w_in(w_in[l], b_in[l])
        proj = _proj(x, w_all, b_all, tm_proj, tn_proj)
        oa, ob0, lse0 = _attn_ab0(proj, flags1, sink[l].reshape(1, A_HK * A_G), slopes_a, slopes_b[:B_G])
        obs, lses = [ob0], [lse0]
        for gi in (1, 2):
            o, s = _attn_b(proj, flags_b[gi], gi, slopes_b[gi * B_G:(gi + 1) * B_G])
            obs.append(o)
            lses.append(s)
        gct = proj[:, U_GC * LANES:U_GC * LANES + 16].T
        hf, hb = _mlstm(proj, gct, conv_w[l], flags1)
        x1, x1t = _merge(x, proj, oa, obs, lses, hf, hb, w_gates, b_gates, w_br_a[l].astype(BF16), w_br_b[l].astype(BF16),
                    w_br_c[l].astype(BF16), w_out[l].astype(BF16), row2(ln1_g[l]), row2(ln1_b[l]), alpha, tm_merge)
        top_idx, top_w, rank, counts = _route(x1, w_router[l].T, router_bias[l].reshape(NE, 1), tm_route)
        counts = counts[:, 0]
        padded = (counts + bm - 1) // bm * bm
        pad_end = jnp.cumsum(padded)
        pad_start = pad_end - padded
        dest = _dest_tiles(top_idx, rank, pad_start, tm_moe)
        blk_e = jnp.minimum(jnp.searchsorted(pad_end, jnp.arange(n_blocks, dtype=I32) * bm, side='right'),
                            NE - 1).astype(I32)
        n_used = (pad_end[-1:] // bm).astype(I32)
        xs = _dispatch(counts, pad_start, dest, x1t, P, tm_moe, bm)
        ys = _experts(blk_e, n_used, xs, w_exp_gu, w_exp_down, l, bm)
        comb = functools.partial(_combine, dest, ys, x1, top_w.T, w_sh_gu[l].astype(BF16), w_sh_down[l].astype(BF16),
                                 row2(ln2_g[l]), row2(ln2_b[l]), alpha, tm_moe)
        if l + 1 < depth or split is None:
            x = comb()
        else:
            return comb(0, split // tm_moe), comb(split // tm_moe, (T - split) // tm_moe)
    return x


def kernel(x_prompt, x_sample, w_in, b_in, sink, conv_w, w_br_a, w_br_b, w_br_c, w_out, ln1_g, ln1_b, w_router,
           router_bias, w_exp_gu, w_exp_down, w_sh_gu, w_sh_down, ln2_g, ln2_b):
    depth = w_in.shape[0]
    n1, s1, _ = x_prompt.shape
    n2, s2, _ = x_sample.shape
    span = QB * max(B_DILS)
    assert s1 % span == 0 and s2 % span == 0, "sequence lengths must be multiples of the widest dilated block"
    x = jnp.concatenate([x_prompt.reshape(n1 * s1, D), x_sample.reshape(n2 * s2, D)], axis=0)
    params = (w_in, b_in, sink, conv_w, w_br_a, w_br_b, w_br_c, w_out, ln1_g, ln1_b, w_router, router_bias,
              w_exp_gu, w_exp_down, w_sh_gu, w_sh_down, ln2_g, ln2_b)
    y1, y2 = _trunk(x, [s1] * n1 + [s2] * n2, depth, params, bm=512, tm_proj=1024, tn_proj=512, tm_merge=256,
                    tm_route=256, tm_moe=512, split=n1 * s1)
    return y1.reshape(n1, s1, D), y2.reshape(n2, s2, D)
```

```python
import functools

import numpy as np
import jax
import jax.numpy as jnp
from jax import lax
from jax.experimental import pallas as pl
from jax.experimental.pallas import tpu as pltpu

F32, BF16, I32, U32 = jnp.float32, jnp.bfloat16, jnp.int32, jnp.uint32

LANES = 128
D = 1024
HD = 64
A_HK, A_G, A_BAND = 2, 4, 128
B_G, B_BAND = 2, 64
B_DILS = (1, 4, 16)
C_H, C_HD, CHUNK = 4, 128, 128
NE, TOPK, NGRP, TOPG, DE = 256, 8, 8, 4, 256
EPG = NE // NGRP
ROUTED_SCALE = 2.5
LN_EPS = 1e-5
NEG = -1e30
QB = 128
SLOT_ROWS = D // (2 * LANES)
COMBINE_CHUNK = 32

U_QA, U_KA, U_VA = 0, 4, 5
U_QB = (6, 8, 10)
U_KVB = (7, 9, 11)
U_QC, U_KC, U_VC, U_OC, U_GC = 12, 16, 20, 24, 28
N_UNITS = 32
NPAD = N_UNITS * LANES


def _edge_flags(seq_lens, rows):
    first, last = [], []
    for s in seq_lens:
        nb = s // rows
        first += [1] + [0] * (nb - 1)
        last += [0] * (nb - 1) + [1]
    return np.array([first, last], np.int32)


def _mm(a, b):
    return jnp.dot(a.astype(BF16), b.astype(BF16), preferred_element_type=F32)


def _mm_nt(a, b):
    return lax.dot_general(a.astype(BF16), b.astype(BF16), (((1,), (1,)), ((), ())), preferred_element_type=F32)


def _mm_tn(a, b):
    return lax.dot_general(a.astype(BF16), b.astype(BF16), (((0,), (0,)), ((), ())), preferred_element_type=F32)


def _split3(x):
    hi = x.astype(BF16)
    r1 = x - hi.astype(F32)
    mid = r1.astype(BF16)
    lo = (r1 - mid.astype(F32)).astype(BF16)
    return hi, mid, lo


def _sigmoid(x):
    return 1.0 / (1.0 + jnp.exp(-x))


def _log_sigmoid(x):
    return jnp.minimum(x, 0.0) - jnp.log(1.0 + jnp.exp(-jnp.abs(x)))


def _pack_tokens(ref, x):
    n = x.shape[0]
    u = lax.bitcast_convert_type(x, U32)
    u = u + (jnp.uint32(0x7FFF) + ((u >> 16) & jnp.uint32(1)))
    for j in range(SLOT_ROWS):
        lo = u[:, j * LANES:(j + 1) * LANES] >> 16
        hi = u[:, D // 2 + j * LANES:D // 2 + (j + 1) * LANES] & jnp.uint32(0xFFFF0000)
        ref[pl.ds(j, n, stride=SLOT_ROWS), :] = lo | hi


def _unpack_tokens(ref):
    n = ref.shape[0] // SLOT_ROWS
    words = [ref[pl.ds(j, n, stride=SLOT_ROWS), :] for j in range(SLOT_ROWS)]
    lo = [lax.bitcast_convert_type(w << 16, F32) for w in words]
    hi = [lax.bitcast_convert_type(w & jnp.uint32(0xFFFF0000), F32) for w in words]
    return jnp.concatenate(lo + hi, axis=1)


def _layer_norm(x, g, b):
    mu = jnp.mean(x, axis=-1, keepdims=True)
    xc = x - mu
    var = jnp.mean(xc * xc, axis=-1, keepdims=True)
    return xc * lax.rsqrt(var + LN_EPS) * g + b


def _proj_kernel(x_ref, w_ref, b_ref, o_ref, xb_ref):
    @pl.when(pl.program_id(1) == 0)
    def _():
        xb_ref[...] = x_ref[...].astype(BF16)

    o_ref[...] = jnp.dot(xb_ref[...], w_ref[...], preferred_element_type=F32) + b_ref[...]


def _proj(x, w, b, tm, tn):
    T, N = x.shape[0], w.shape[1]
    return pl.pallas_call(
        _proj_kernel,
        grid=(T // tm, N // tn),
        in_specs=[pl.BlockSpec((tm, D), lambda i, j: (i, 0)),
                  pl.BlockSpec((D, tn), lambda i, j: (0, j)),
                  pl.BlockSpec((1, tn), lambda i, j: (0, j))],
        out_specs=pl.BlockSpec((tm, tn), lambda i, j: (i, j)),
        out_shape=jax.ShapeDtypeStruct((T, N), F32),
        scratch_shapes=[pltpu.VMEM((tm, D), BF16)],
        compiler_params=pltpu.CompilerParams(dimension_semantics=("parallel", "arbitrary")),
        name="proj",
    )(x, w, b)


def _attn_bias(slopes, g, band, dist):
    rel = np.abs(np.arange(3 * QB)[None, :] - QB - np.arange(QB)[:, None])
    dist_f = (rel * dist).astype(np.float32)
    per_head = [np.where(rel <= band, -(np.float32(sl) * dist_f), np.float32(NEG)) for sl in slopes]
    return np.stack([np.concatenate(per_head[h * g:(h + 1) * g], axis=0) for h in range(len(slopes) // g)])


def _attn_tile(q, kwin, vwin, bias_ref, lo, hi, sinks, *, hk, g, packed):
    qb, W = q.shape[0], kwin.shape[0]
    rows = g * qb
    one_col = jnp.where(lax.broadcasted_iota(I32, (rows, HD), 1) == 0, 1.0, 0.0)
    key = lax.broadcasted_iota(I32, (W, HD), 0)
    edge = jnp.where((key >= lo) & (key < hi), 0.0, NEG)
    row_g = jnp.right_shift(lax.broadcasted_iota(I32, (rows, 1), 0), qb.bit_length() - 1)
    outs, lses = [], []
    for h in range(hk):
        qs = jnp.concatenate([q[:, (h * g + gg) * HD:(h * g + gg + 1) * HD] for gg in range(g)], axis=0)
        qs = jnp.concatenate([qs * (HD ** -0.5), one_col], axis=1)
        ko = h * HD
        vo = HD if packed else h * HD
        kh = jnp.concatenate([kwin[:, ko:ko + HD], edge], axis=1)
        vh = vwin[:, vo:vo + HD]
        logits = _mm_nt(qs, kh) + bias_ref[h]
        m = jnp.max(logits, axis=-1, keepdims=True)
        if sinks is not None:
            sink_col = jnp.zeros((rows, 1), F32)
            for gg in range(g):
                sink_col = jnp.where(row_g == gg, sinks[h * g + gg], sink_col)
            m = jnp.maximum(m, sink_col)
        p = jnp.exp(logits - m)
        den = jnp.sum(p, axis=-1, keepdims=True)
        if sinks is not None:
            den = den + jnp.exp(sink_col - m)
        o = _mm(p, vh) / den
        for gg in range(g):
            outs.append(o[gg * qb:(gg + 1) * qb])
        if packed:
            l = m + jnp.log(den)
            for gg in range(g):
                lses.append(jnp.broadcast_to(l[gg * qb:(gg + 1) * qb], (qb, HD)))
    return outs, lses


def _edge_cols(flags_ref, b):
    return jnp.where(flags_ref[0, b] == 0, 0, QB), jnp.where(flags_ref[1, b] == 0, 3 * QB, 2 * QB)


def _attn_ab0_kernel(flags_ref, sink_ref, qa_ref, kp_ref, ko_ref, kn_ref, vp_ref, vo_ref, vn_ref, qb_ref, kvp_ref,
                     kvo_ref, kvn_ref, bias_a_ref, bias_b_ref, oa_ref, ob_ref, lse_ref):
    lo, hi = _edge_cols(flags_ref, pl.program_id(0))
    sinks = [sink_ref[0, i] for i in range(A_HK * A_G)]
    kwin = jnp.concatenate([kp_ref[...], ko_ref[...], kn_ref[...]], axis=0)
    vwin = jnp.concatenate([vp_ref[...], vo_ref[...], vn_ref[...]], axis=0)
    outs, _ = _attn_tile(qa_ref[...], kwin, vwin, bias_a_ref, lo, hi, sinks, hk=A_HK, g=A_G, packed=False)
    oa_ref[...] = jnp.concatenate(outs, axis=1).astype(oa_ref.dtype)
    kv = jnp.concatenate([kvp_ref[...], kvo_ref[...], kvn_ref[...]], axis=0)
    outs, lses = _attn_tile(qb_ref[...], kv, kv, bias_b_ref, lo, hi, None, hk=1, g=B_G, packed=True)
    ob_ref[...] = jnp.concatenate(outs, axis=1)
    lse_ref[...] = jnp.concatenate(lses, axis=1)


def _attn_ab0(proj, flags, sink, slopes_a, slopes_b0):
    T = proj.shape[0]
    nb = T // QB
    prev = lambda b, f: jnp.maximum(b - 1, 0)
    nxt = lambda b, f: jnp.minimum(b + 1, nb - 1)
    kspecs = lambda u: [pl.BlockSpec((QB, LANES), lambda b, f: (prev(b, f), u)),
                        pl.BlockSpec((QB, LANES), lambda b, f: (b, u)),
                        pl.BlockSpec((QB, LANES), lambda b, f: (nxt(b, f), u))]
    bias_a = jnp.asarray(_attn_bias(slopes_a, A_G, A_BAND, 1))
    bias_b = jnp.asarray(_attn_bias(slopes_b0, B_G, B_BAND, B_DILS[0]))
    full = lambda a: pl.BlockSpec(a.shape, lambda b, f: (0,) * a.ndim)
    wa = A_HK * A_G * HD
    return pl.pallas_call(
        _attn_ab0_kernel,
        grid_spec=pltpu.PrefetchScalarGridSpec(
            num_scalar_prefetch=1, grid=(nb,),
            in_specs=[pl.BlockSpec(memory_space=pltpu.SMEM), pl.BlockSpec((QB, wa), lambda b, f: (b, U_QA // 4))]
            + kspecs(U_KA) + kspecs(U_VA) + [pl.BlockSpec((QB, LANES), lambda b, f: (b, U_QB[0]))] + kspecs(U_KVB[0])
            + [full(bias_a), full(bias_b)],
            out_specs=[pl.BlockSpec((QB, wa), lambda b, f: (b, 0)), pl.BlockSpec((QB, LANES), lambda b, f: (b, 0)),
                       pl.BlockSpec((QB, LANES), lambda b, f: (b, 0))]),
        out_shape=[jax.ShapeDtypeStruct((T, wa), BF16), jax.ShapeDtypeStruct((T, LANES), F32),
                   jax.ShapeDtypeStruct((T, LANES), F32)],
        compiler_params=pltpu.CompilerParams(dimension_semantics=("parallel",)),
        name="attn_ab0",
    )(flags, sink, *([proj] * 11), bias_a, bias_b)


def _attn_b_kernel(flags_ref, q_ref, kp_ref, ko_ref, kn_ref, bias_ref, o_ref, lse_ref, *, dil):
    lo, hi = _edge_cols(flags_ref, pl.program_id(0))

    def body(r, c):
        sub = lambda ref: ref[pl.ds(r, QB, stride=dil), :]
        kv = jnp.concatenate([sub(kp_ref), sub(ko_ref), sub(kn_ref)], axis=0)
        outs, lses = _attn_tile(sub(q_ref), kv, kv, bias_ref, lo, hi, None, hk=1, g=B_G, packed=True)
        o_ref[pl.ds(r, QB, stride=dil), :] = jnp.concatenate(outs, axis=1)
        lse_ref[pl.ds(r, QB, stride=dil), :] = jnp.concatenate(lses, axis=1)
        return c

    lax.fori_loop(0, dil, body, 0, unroll=2)


def _attn_b(proj, flags, gi, slopes):
    dil = B_DILS[gi]
    T = proj.shape[0]
    rows = QB * dil
    nb = T // rows
    uq, ukv = U_QB[gi], U_KVB[gi]
    bias = jnp.asarray(_attn_bias(slopes, B_G, B_BAND, dil))
    return pl.pallas_call(
        functools.partial(_attn_b_kernel, dil=dil),
        grid_spec=pltpu.PrefetchScalarGridSpec(
            num_scalar_prefetch=1, grid=(nb,),
            in_specs=[pl.BlockSpec((rows, LANES), lambda b, f: (b, uq)),
                      pl.BlockSpec((rows, LANES), lambda b, f: (jnp.maximum(b - 1, 0), ukv)),
                      pl.BlockSpec((rows, LANES), lambda b, f: (b, ukv)),
                      pl.BlockSpec((rows, LANES), lambda b, f: (jnp.minimum(b + 1, nb - 1), ukv)),
                      pl.BlockSpec(bias.shape, lambda b, f: (0, 0, 0))],
            out_specs=[pl.BlockSpec((rows, LANES), lambda b, f: (b, 0)),
                       pl.BlockSpec((rows, LANES), lambda b, f: (b, 0))]),
        out_shape=[jax.ShapeDtypeStruct((T, LANES), F32), jax.ShapeDtypeStruct((T, LANES), F32)],
        compiler_params=pltpu.CompilerParams(dimension_semantics=("parallel",)),
        name=f"attn_b{gi}",
    )(flags, proj, proj, proj, proj, bias)


def _conv_silu(x, xp8, xn8, w, keep_prev, keep_next):
    rowi = lax.broadcasted_iota(I32, (CHUNK, 1), 0)
    x_prev = jnp.where(rowi == 0, xp8[7:8, :] * keep_prev, pltpu.roll(x, 1, 0))
    x_next = jnp.where(rowi == CHUNK - 1, xn8[0:1, :] * keep_next, pltpu.roll(x, CHUNK - 1, 0))
    y = w[0:1, :] * x_prev + w[1:2, :] * x + w[2:3, :] * x_next
    return y * _sigmoid(y)


def _mlstm_dir(rev, first, last, q_ref, qp_ref, qn_ref, k_ref, kp_ref, kn_ref, v_ref, g_ref, gt_ref, cw_ref,
               c_ref, n_ref, m_ref, h_ref):
    sd = 1 if rev else 0
    keep_prev = (1 - first).astype(F32)
    keep_next = (1 - last).astype(F32)

    @pl.when((last if rev else first) == 1)
    def _():
        c_ref[sd] = jnp.zeros(c_ref.shape[1:], F32)
        n_ref[sd] = jnp.zeros(n_ref.shape[1:], F32)
        m_ref[sd] = jnp.zeros(m_ref.shape[1:], F32)

    W = C_H * C_HD
    q = _conv_silu(q_ref[...], qp_ref[...], qn_ref[...], cw_ref[:, :W], keep_prev, keep_next)
    k = _conv_silu(k_ref[...], kp_ref[...], kn_ref[...], cw_ref[:, W:], keep_prev, keep_next) * (C_HD ** -0.5)
    v = v_ref[...]
    G = g_ref[...]
    GT = gt_ref[...]
    ti = lax.broadcasted_iota(I32, (CHUNK, CHUNK), 0)
    si = lax.broadcasted_iota(I32, (CHUNK, CHUNK), 1)
    tri = (si >= ti) if rev else (si <= ti)
    L = jnp.where(tri, 1.0, 0.0).astype(BF16)
    Lt = jnp.where((ti >= si) if rev else (ti <= si), 1.0, 0.0).astype(BF16)
    cum_col = sum(jnp.dot(L, p, preferred_element_type=F32) for p in _split3(_log_sigmoid(G)))
    cum_row = sum(jnp.dot(p, Lt, preferred_element_type=F32) for p in _split3(_log_sigmoid(GT)))
    e = 0 if rev else CHUNK - 1
    hs = []
    for h in range(C_H):
        ci = (2 * C_H if rev else 0) + h
        cf = ci + C_H
        cum_c, cum_r = cum_col[:, cf:cf + 1], cum_row[cf:cf + 1, :]
        li_c, li_r = G[:, ci:ci + 1], GT[ci:ci + 1, :]
        qh, kh, vh = (a[:, h * C_HD:(h + 1) * C_HD] for a in (q, k, v))
        m11 = m_ref[sd, h][:, :1]
        Dm = jnp.where(tri, cum_c - cum_r + li_r, NEG)
        m_inter = cum_c + m11
        mt = jnp.maximum(jnp.max(Dm, axis=-1, keepdims=True), m_inter)
        Wm = jnp.exp(Dm - mt) * _mm_nt(qh, kh)
        inter = jnp.exp(m_inter - mt)
        num = _mm(Wm, vh) + inter * _mm(qh, c_ref[sd, h])
        den = jnp.sum(Wm, axis=-1, keepdims=True) + inter * jnp.sum(qh * n_ref[sd, h], axis=-1, keepdims=True)
        hs.append(num / jnp.maximum(jnp.abs(den), jnp.exp(-mt)))
        total = cum_c[e:e + 1, :]
        gcol = total - cum_c + li_c
        m_new = jnp.maximum(total + m11, jnp.max(gcol, axis=0, keepdims=True))
        kw = kh * jnp.exp(gcol - m_new)
        decay = jnp.exp(total + m11 - m_new)
        c_ref[sd, h] = decay * c_ref[sd, h] + _mm_tn(kw, vh)
        n_ref[sd, h] = decay * n_ref[sd, h] + jnp.sum(kw, axis=0, keepdims=True)
        m_ref[sd, h] = jnp.broadcast_to(m_new, (1, LANES))
    h_ref[...] = jnp.concatenate(hs, axis=1)


def _mlstm_kernel(flags_ref, *refs, nb):
    fwd, bwd, (cw_ref, hf_ref, hb_ref, c_ref, n_ref, m_ref) = refs[:9], refs[9:18], refs[18:]
    j = pl.program_id(0)
    jb = nb - 1 - j
    _mlstm_dir(False, flags_ref[0, j], flags_ref[1, j], *fwd, cw_ref, c_ref, n_ref, m_ref, hf_ref)
    _mlstm_dir(True, flags_ref[0, jb], flags_ref[1, jb], *bwd, cw_ref, c_ref, n_ref, m_ref, hb_ref)


def _mlstm(proj, gct, conv_w, flags):
    T = proj.shape[0]
    nb = T // CHUNK
    W = C_H * C_HD
    r8 = CHUNK // 8

    def specs(blk):
        halo_p = lambda j, f: jnp.maximum(blk(j) * r8 - 1, 0)
        halo_n = lambda j, f: jnp.minimum(blk(j) * r8 + r8, nb * r8 - 1)
        out = []
        for u in (U_QC, U_KC):
            out += [pl.BlockSpec((CHUNK, W), lambda j, f, u=u: (blk(j), u // 4)),
                    pl.BlockSpec((8, W), lambda j, f, u=u: (halo_p(j, f), u // 4)),
                    pl.BlockSpec((8, W), lambda j, f, u=u: (halo_n(j, f), u // 4))]
        out += [pl.BlockSpec((CHUNK, W), lambda j, f: (blk(j), U_VC // 4)),
                pl.BlockSpec((CHUNK, LANES), lambda j, f: (blk(j), U_GC)),
                pl.BlockSpec((16, CHUNK), lambda j, f: (0, blk(j)))]
        return out

    fwd_blk = lambda j: j
    bwd_blk = lambda j: nb - 1 - j
    args = [proj] * 8 + [gct]
    return pl.pallas_call(
        functools.partial(_mlstm_kernel, nb=nb),
        grid_spec=pltpu.PrefetchScalarGridSpec(
            num_scalar_prefetch=1, grid=(nb,),
            in_specs=specs(fwd_blk) + specs(bwd_blk) + [pl.BlockSpec((3, 2 * W), lambda j, f: (0, 0))],
            out_specs=[pl.BlockSpec((CHUNK, W), lambda j, f: (j, 0)),
                       pl.BlockSpec((CHUNK, W), lambda j, f: (nb - 1 - j, 0))],
            scratch_shapes=[pltpu.VMEM((2, C_H, C_HD, C_HD), F32),
                            pltpu.VMEM((2, C_H, 1, C_HD), F32),
                            pltpu.VMEM((2, C_H, 1, LANES), F32)]),
        out_shape=[jax.ShapeDtypeStruct((T, W), F32), jax.ShapeDtypeStruct((T, W), F32)],
        compiler_params=pltpu.CompilerParams(dimension_semantics=("arbitrary",)),
        name="mlstm",
    )(flags, *args, *args, conv_w)


def _merge_kernel(x_ref, oa_ref, ob0_ref, ob1_ref, ob2_ref, l0_ref, l1_ref, l2_ref, hf_ref, hb_ref, oc_ref,
                  wg_ref, bg_ref, wa_ref, wb_ref, wc_ref, wo_ref, lg_ref, lb_ref, o_ref, ot_ref, *, alpha):
    ya = jnp.dot(oa_ref[...], wa_ref[...], preferred_element_type=F32)
    l0, l1, l2 = l0_ref[...], l1_ref[...], l2_ref[...]
    lm = jnp.maximum(jnp.maximum(l0, l1), l2)
    e0, e1, e2 = jnp.exp(l0 - lm), jnp.exp(l1 - lm), jnp.exp(l2 - lm)
    es = e0 + e1 + e2
    outb = jnp.concatenate([ob0_ref[...] * (e0 / es), ob1_ref[...] * (e1 / es), ob2_ref[...] * (e2 / es)], axis=1)
    yb = _mm(outb, wb_ref[...])
    yc = _mm(_sigmoid(oc_ref[...]) * (hf_ref[...] + hb_ref[...]), wc_ref[...])
    x = x_ref[...]
    gates = _sigmoid(_mm(x, wg_ref[...]) + bg_ref[...])
    merged = gates[:, 0:D] * ya + gates[:, D:2 * D] * yb + gates[:, 2 * D:3 * D] * yc
    mix = _mm(merged, wo_ref[...])
    x1 = _layer_norm(alpha * x + mix, lg_ref[...], lb_ref[...])
    o_ref[...] = x1
    _pack_tokens(ot_ref, x1)


def _merge(x, proj, oa, obs, lses, hf, hb, wg, bg, wa, wb, wc, wo, lg, lb, alpha, tm):
    T = x.shape[0]
    row = lambda w: pl.BlockSpec((tm, w), lambda i: (i, 0))
    full = lambda a: pl.BlockSpec(a.shape, lambda i: (0,) * a.ndim)
    return pl.pallas_call(
        functools.partial(_merge_kernel, alpha=alpha),
        grid=(T // tm,),
        in_specs=[row(D), row(oa.shape[1])] + [row(LANES)] * 6 + [row(hf.shape[1])] * 2
        + [pl.BlockSpec((tm, 512), lambda i: (i, U_OC // 4))]
        + [full(a) for a in (wg, bg, wa, wb, wc, wo, lg, lb)],
        out_specs=[row(D), pl.BlockSpec((tm * SLOT_ROWS, LANES), lambda i: (i, 0))],
        out_shape=[jax.ShapeDtypeStruct((T, D), F32), jax.ShapeDtypeStruct((T * SLOT_ROWS, LANES), U32)],
        compiler_params=pltpu.CompilerParams(dimension_semantics=("parallel",)),
        name="merge",
    )(x, oa, *obs, *lses, hf, hb, proj, wg, bg, wa, wb, wc, wo, lg, lb)


def _stack_rows(rows, iota):
    out = jnp.zeros(iota.shape, F32)
    for k, r in enumerate(rows):
        out = jnp.where(iota == float(k), r, out)
    return out


def _route_kernel(x_ref, wrt_ref, bias_ref, idx_ref, w_ref, rank_ref, cnt_ref, run_ref, *, tm):
    @pl.when(pl.program_id(0) == 0)
    def _():
        run_ref[...] = jnp.zeros(run_ref.shape, F32)

    x = x_ref[...]
    xh, xm, xl = _split3(x)
    wh, wm, wl = _split3(wrt_ref[...])
    nt = lambda a, b: lax.dot_general(a, b, (((1,), (1,)), ((), ())), preferred_element_type=F32)
    logits = nt(wh, xh) + (nt(wh, xm) + nt(wm, xh)) + (nt(wh, xl) + nt(wl, xh) + nt(wm, xm))
    scores = _sigmoid(logits)
    biased = scores + bias_ref[...]
    ninf = -jnp.inf
    iota_g = lax.broadcasted_iota(I32, (EPG, tm), 0).astype(F32)
    gs = []
    for gi in range(NGRP):
        v = biased[gi * EPG:(gi + 1) * EPG]
        t1 = jnp.max(v, axis=0, keepdims=True)
        i1 = jnp.min(jnp.where(v == t1, iota_g, float(EPG)), axis=0, keepdims=True)
        t2 = jnp.max(jnp.where(iota_g == i1, ninf, v), axis=0, keepdims=True)
        gs.append(t1 + t2)
    iota8 = lax.broadcasted_iota(I32, (NGRP, tm), 0).astype(F32)
    gs = _stack_rows(gs, iota8)
    gsel = jnp.zeros((NGRP, tm), F32)
    for _ in range(TOPG):
        mx = jnp.max(gs, axis=0, keepdims=True)
        ix = jnp.min(jnp.where(gs == mx, iota8, float(NGRP)), axis=0, keepdims=True)
        hit = iota8 == ix
        gsel = jnp.where(hit, 1.0, gsel)
        gs = jnp.where(hit, ninf, gs)
    v = jnp.concatenate([jnp.where(gsel[gi:gi + 1] > 0.0, biased[gi * EPG:(gi + 1) * EPG], ninf)
                         for gi in range(NGRP)], axis=0)
    iota_e = lax.broadcasted_iota(I32, (NE, tm), 0).astype(F32)
    sel = jnp.zeros((NE, tm), F32)
    idxs, svals = [], []
    for _ in range(TOPK):
        mx = jnp.max(v, axis=0, keepdims=True)
        ix = jnp.min(jnp.where(v == mx, iota_e, float(NE)), axis=0, keepdims=True)
        hit = iota_e == ix
        idxs.append(ix)
        svals.append(jnp.sum(jnp.where(hit, scores, 0.0), axis=0, keepdims=True))
        sel = jnp.where(hit, 1.0, sel)
        v = jnp.where(hit, ninf, v)
    top_s = _stack_rows(svals, iota8)
    idx_ref[...] = _stack_rows(idxs, iota8).astype(I32)
    w_ref[...] = top_s / jnp.sum(top_s, axis=0, keepdims=True) * ROUTED_SCALE
    ti = lax.broadcasted_iota(I32, (tm, tm), 0)
    si = lax.broadcasted_iota(I32, (tm, tm), 1)
    before = jnp.where(ti < si, 1.0, 0.0).astype(BF16)
    selb = sel.astype(BF16)
    cnt = run_ref[...] + jnp.dot(selb, before, preferred_element_type=F32)
    rank_ref[...] = _stack_rows(
        [jnp.sum(jnp.where(iota_e == ix, cnt, 0.0), axis=0, keepdims=True) for ix in idxs], iota8).astype(I32)
    run = run_ref[...] + jnp.dot(selb, jnp.ones((tm, tm), BF16), preferred_element_type=F32)
    run_ref[...] = run
    cnt_ref[...] = run.astype(I32)


def _route(x1, wrt, bias_col, tm):
    T = x1.shape[0]
    slot = lambda dt: jax.ShapeDtypeStruct((TOPK, T), dt)
    return pl.pallas_call(
        functools.partial(_route_kernel, tm=tm),
        grid=(T // tm,),
        in_specs=[pl.BlockSpec((tm, D), lambda i: (i, 0)),
                  pl.BlockSpec((NE, D), lambda i: (0, 0)),
                  pl.BlockSpec((NE, 1), lambda i: (0, 0))],
        out_specs=[pl.BlockSpec((TOPK, tm), lambda i: (0, i))] * 3 + [pl.BlockSpec((NE, tm), lambda i: (0, 0))],
        out_shape=[slot(I32), slot(F32), slot(I32), jax.ShapeDtypeStruct((NE, tm), I32)],
        scratch_shapes=[pltpu.VMEM((NE, tm), F32)],
        compiler_params=pltpu.CompilerParams(dimension_semantics=("arbitrary",)),
        name="route",
    )(x1, wrt, bias_col)


def _dest_kernel(idx_ref, rank_ref, start_ref, o_ref, *, tm):
    iota_e = lax.broadcasted_iota(I32, (NE, tm), 0)
    starts = start_ref[...]
    rows = []
    for k in range(TOPK):
        base = jnp.sum(jnp.where(iota_e == idx_ref[k:k + 1, :], starts, 0.0), axis=0, keepdims=True)
        rows.append(base.astype(I32) + rank_ref[k:k + 1, :])
    o_ref[0] = jnp.concatenate(rows, axis=1)


def _dest_tiles(top_idx, rank, pad_start, tm):
    T = top_idx.shape[1]
    return pl.pallas_call(
        functools.partial(_dest_kernel, tm=tm),
        grid=(T // tm,),
        in_specs=[pl.BlockSpec((TOPK, tm), lambda i: (0, i)), pl.BlockSpec((TOPK, tm), lambda i: (0, i)),
                  pl.BlockSpec((NE, 1), lambda i: (0, 0))],
        out_specs=pl.BlockSpec((1, 1, TOPK * tm), lambda i: (i, 0, 0)),
        out_shape=jax.ShapeDtypeStruct((T // tm, 1, TOPK * tm), I32),
        compiler_params=pltpu.CompilerParams(dimension_semantics=("parallel",)),
        name="dest",
    )(top_idx, rank, pad_start.astype(F32).reshape(NE, 1))


def _tile_rows(ref, slot, n=1):
    return ref.at[pl.ds(pl.multiple_of(slot * SLOT_ROWS, SLOT_ROWS), n * SLOT_ROWS)]


def _dispatch_kernel(cnt_ref, start_ref, dest_hbm, x_ref, xs_hbm, dest_smem, zero_ref, sem_idx, sem, sem_z, *, tm, bm):
    i = pl.program_id(0)
    pad_sizes = [1 << k for k in range(bm.bit_length() - 1)]
    half = bm // 2

    def pad_copies(e, fn):
        n = (-cnt_ref[e]) & (bm - 1)
        at = start_ref[e] + cnt_ref[e]
        for sz in pad_sizes:
            @pl.when((n & sz) != 0)
            def _(at=at, sz=sz):
                fn(pltpu.make_async_copy(_tile_rows(zero_ref, 0, sz), _tile_rows(xs_hbm, at, sz), sem_z))
            at = at + (n & sz)

    @pl.when(i == 0)
    def _():
        zero_ref[...] = jnp.zeros(zero_ref.shape, U32)

        def start(e, c):
            pad_copies(e, lambda cp: cp.start())
            return c

        def wait(e, c):
            pad_copies(e, lambda cp: cp.wait())
            return c

        lax.fori_loop(0, NE, start, 0)
        lax.fori_loop(0, NE, wait, 0)

        used = start_ref[NE - 1] + cnt_ref[NE - 1] + ((-cnt_ref[NE - 1]) & (bm - 1))
        tail = lambda c: pltpu.make_async_copy(zero_ref, _tile_rows(xs_hbm, used + c * half, half), sem_z)

        def tail_start(c, z):
            tail(c).start()
            return z

        def tail_wait(c, z):
            tail(c).wait()
            return z

        n_tail = (xs_hbm.shape[0] // SLOT_ROWS - used) // half
        lax.fori_loop(0, n_tail, tail_start, 0)
        lax.fori_loop(0, n_tail, tail_wait, 0)

    cp = pltpu.make_async_copy(dest_hbm.at[i, 0], dest_smem, sem_idx)
    cp.start()
    cp.wait()

    def body(r, c):
        for k in range(TOPK):
            pltpu.make_async_copy(_tile_rows(x_ref, r), _tile_rows(xs_hbm, dest_smem[k * tm + r]), sem).start(
                priority=k % 2)
        return c

    lax.fori_loop(0, tm, body, 0)
    for k in range(TOPK):
        pltpu.make_async_copy(x_ref, _tile_rows(xs_hbm, 0, tm), sem).wait()


def _dispatch(counts, pad_start, dest_tiles, x1t, P, tm, bm):
    T = x1t.shape[0] // SLOT_ROWS
    return pl.pallas_call(
        functools.partial(_dispatch_kernel, tm=tm, bm=bm),
        grid_spec=pltpu.PrefetchScalarGridSpec(
            num_scalar_prefetch=2, grid=(T // tm,),
            in_specs=[pl.BlockSpec(memory_space=pl.ANY),
                      pl.BlockSpec((tm * SLOT_ROWS, LANES), lambda i, c, s: (i, 0))],
            out_specs=pl.BlockSpec(memory_space=pl.ANY),
            scratch_shapes=[pltpu.SMEM((TOPK * tm,), I32), pltpu.VMEM((bm // 2 * SLOT_ROWS, LANES), U32),
                            pltpu.SemaphoreType.DMA, pltpu.SemaphoreType.DMA, pltpu.SemaphoreType.DMA]),
        out_shape=jax.ShapeDtypeStruct((P * SLOT_ROWS, LANES), U32),
        compiler_params=pltpu.CompilerParams(dimension_semantics=("arbitrary",)),
        name="dispatch",
    )(counts, pad_start, dest_tiles, x1t)


def _expert_kernel(blk_e_ref, n_used_ref, x_ref, wgu_ref, wdn_ref, y_ref, wgu_b, wdn_b):
    b = pl.program_id(0)
    live = b < n_used_ref[0]

    @pl.when(live & ((b == 0) | (blk_e_ref[b] != blk_e_ref[jnp.maximum(b - 1, 0)])))
    def _():
        wgu_b[...] = wgu_ref[0, 0].astype(BF16)
        wdn_b[...] = wdn_ref[0, 0].astype(BF16)

    @pl.when(live)
    def _():
        gu = _mm(_unpack_tokens(x_ref), wgu_b[...])
        gate, up = gu[:, :DE], gu[:, DE:]
        _pack_tokens(y_ref, _mm(gate * _sigmoid(gate) * up, wdn_b[...]))

    @pl.when(b >= n_used_ref[0])
    def _():
        y_ref[...] = jnp.zeros(y_ref.shape, U32)


def _experts(blk_e, n_used, xs, wgu, wdn, layer, bm):
    n_blocks = xs.shape[0] // (bm * SLOT_ROWS)
    live = lambda b, n: jnp.minimum(b, n[0] - 1)
    return pl.pallas_call(
        _expert_kernel,
        grid_spec=pltpu.PrefetchScalarGridSpec(
            num_scalar_prefetch=2, grid=(n_blocks,),
            in_specs=[pl.BlockSpec((bm * SLOT_ROWS, LANES), lambda b, e, n: (live(b, n), 0)),
                      pl.BlockSpec((1, 1, D, 2 * DE), lambda b, e, n: (layer, e[live(b, n)], 0, 0)),
                      pl.BlockSpec((1, 1, DE, D), lambda b, e, n: (layer, e[live(b, n)], 0, 0))],
            out_specs=pl.BlockSpec((bm * SLOT_ROWS, LANES), lambda b, e, n: (b, 0)),
            scratch_shapes=[pltpu.VMEM((D, 2 * DE), BF16), pltpu.VMEM((DE, D), BF16)]),
        out_shape=jax.ShapeDtypeStruct(xs.shape, U32),
        compiler_params=pltpu.CompilerParams(dimension_semantics=("arbitrary",)),
        name="experts",
    )(blk_e, n_used, xs, wgu, wdn)


def _combine_kernel(dest_hbm, y_hbm, x_ref, w_ref, wsg_ref, wsd_ref, lg_ref, lb_ref, o_ref, dest_a, dest_b, rows_ref,
                    routed_ref, sem_idx, sem, *, tm, alpha, tile0, n_tiles, chunk):
    g = pl.program_id(0)
    t_next = jnp.minimum(g + 1, n_tiles - 1) + tile0

    dest_smem = (dest_a, dest_b)

    def slots_copy(tile, s):
        return pltpu.make_async_copy(dest_hbm.at[tile, 0], dest_smem[s], sem_idx.at[s])

    def issue(s, r):
        for k in range(TOPK):
            pltpu.make_async_copy(_tile_rows(y_hbm, dest_smem[s][k * tm + r]), _tile_rows(rows_ref.at[s, k], r),
                                  sem.at[s]).start(priority=k % 2)

    def wait_rows(s):
        for k in range(TOPK):
            pltpu.make_async_copy(_tile_rows(y_hbm, 0, tm), rows_ref.at[s, k], sem.at[s]).wait()

    @pl.when(g == 0)
    def _():
        cp = slots_copy(tile0, 0)
        cp.start()
        cp.wait()

        def first(r, c):
            issue(0, r)
            return c

        lax.fori_loop(0, tm, first, 0)

    def step(cur, nxt):
        slots_next = slots_copy(t_next, nxt)
        slots_next.start()
        x = x_ref[...]
        gu = _mm(x, wsg_ref[...])
        gate, up = gu[:, :DE], gu[:, DE:]
        shared = _mm(gate * _sigmoid(gate) * up, wsd_ref[...])
        slots_next.wait()
        wait_rows(cur)

        def chunk_body(c, carry):
            r0 = pl.multiple_of(c * chunk, chunk)
            acc = None
            for k in range(TOPK):
                yk = _unpack_tokens(rows_ref.at[cur, k, pl.ds(r0 * SLOT_ROWS, chunk * SLOT_ROWS)])
                term = yk * w_ref[pl.ds(r0, chunk), k:k + 1]
                acc = term if acc is None else acc + term
            routed_ref[pl.ds(r0, chunk), :] = acc
            for rr in range(chunk):
                issue(nxt, r0 + rr)
            return carry

        lax.fori_loop(0, tm // chunk, chunk_body, 0)
        o_ref[...] = _layer_norm(alpha * x + (routed_ref[...] + shared), lg_ref[...], lb_ref[...])

        @pl.when(g == n_tiles - 1)
        def _():
            wait_rows(nxt)

    for parity in range(2):
        pl.when(g % 2 == parity)(functools.partial(step, parity, 1 - parity))


def _combine(dest_tiles, ys, x1, w_tok, wsg, wsd, lg, lb, alpha, tm, tile0=0, n_tiles=None):
    n_tiles = x1.shape[0] // tm if n_tiles is None else n_tiles
    full = lambda a: pl.BlockSpec(a.shape, lambda i: (0,) * a.ndim)
    return pl.pallas_call(
        functools.partial(_combine_kernel, tm=tm, alpha=alpha, tile0=tile0, n_tiles=n_tiles, chunk=COMBINE_CHUNK),
        grid=(n_tiles,),
        in_specs=[pl.BlockSpec(memory_space=pl.ANY), pl.BlockSpec(memory_space=pl.ANY),
                  pl.BlockSpec((tm, D), lambda i: (i + tile0, 0)), pl.BlockSpec((tm, TOPK), lambda i: (i + tile0, 0))]
        + [full(a) for a in (wsg, wsd, lg, lb)],
        out_specs=pl.BlockSpec((tm, D), lambda i: (i, 0)),
        out_shape=jax.ShapeDtypeStruct((n_tiles * tm, D), F32),
        scratch_shapes=[pltpu.SMEM((TOPK * tm,), I32), pltpu.SMEM((TOPK * tm,), I32),
                        pltpu.VMEM((2, TOPK, tm * SLOT_ROWS, LANES), U32),
                        pltpu.VMEM((tm, D), F32), pltpu.SemaphoreType.DMA((2,)), pltpu.SemaphoreType.DMA((2,))],
        compiler_params=pltpu.CompilerParams(dimension_semantics=("arbitrary",)),
        name="combine",
    )(dest_tiles, ys, x1, w_tok, wsg, wsd, lg, lb)


def _regroup_w_in(w, b):
    offs = np.cumsum([0, 512, 128, 128, 384, 192, 192, 512, 512, 512, 512, 16, 3 * D])
    qa, ka, va, qb, kb, vb, qc, kc, vc, oc, gc, gates = (int(o) for o in offs[:-1])
    pieces = [(qa, ka), (ka, va), (va, qb)]
    for gi in range(3):
        pieces += [(qb + gi * 128, qb + (gi + 1) * 128), (kb + gi * HD, kb + (gi + 1) * HD),
                   (vb + gi * HD, vb + (gi + 1) * HD)]
    pieces += [(qc, kc), (kc, vc), (vc, oc), (oc, gc), (gc, gates), None]
    gc_pad = (N_UNITS - U_GC) * LANES - (gates - gc)

    def build(a):
        parts = [jnp.zeros(a.shape[:-1] + (gc_pad,), a.dtype) if p is None else a[..., p[0]:p[1]] for p in pieces]
        return jnp.concatenate(parts, axis=-1)

    return (build(w).astype(BF16), build(b).reshape(1, NPAD), w[:, gates:].astype(BF16), b[gates:].reshape(1, 3 * D))


def _trunk(x, seq_lens, depth, params, *, bm, tm_proj, tn_proj, tm_merge, tm_route, tm_moe, split=None):
    (w_in, b_in, sink, conv_w, w_br_a, w_br_b, w_br_c, w_out, ln1_g, ln1_b, w_router, router_bias, w_exp_gu,
     w_exp_down, w_sh_gu, w_sh_down, ln2_g, ln2_b) = params
    T = x.shape[0]
    alpha = float((2 * depth) ** 0.25)
    slopes_a = tuple(float(2.0 ** (-8.0 * i / 8)) for i in range(1, 9))
    slopes_b = tuple(float(2.0 ** (-8.0 * i / 6)) for i in range(1, 7))
    flags1 = jnp.asarray(_edge_flags(seq_lens, QB))
    flags_b = [jnp.asarray(_edge_flags(seq_lens, QB * d)) for d in B_DILS]
    A = T * TOPK
    n_blocks = A // bm + NE
    P = n_blocks * bm
    row2 = lambda a: a.reshape(1, -1)
    for l in range(depth):
        w_all, b_all, w_gates, b_gates = _regroup_w_in(w_in[l], b_in[l])
        proj = _proj(x, w_all, b_all, tm_proj, tn_proj)
        oa, ob0, lse0 = _attn_ab0(proj, flags1, sink[l].reshape(1, A_HK * A_G), slopes_a, slopes_b[:B_G])
        obs, lses = [ob0], [lse0]
        for gi in (1, 2):
            o, s = _attn_b(proj, flags_b[gi], gi, slopes_b[gi * B_G:(gi + 1) * B_G])
            obs.append(o)
            lses.append(s)
        gct = proj[:, U_GC * LANES:U_GC * LANES + 16].T
        hf, hb = _mlstm(proj, gct, conv_w[l], flags1)
        x1, x1t = _merge(x, proj, oa, obs, lses, hf, hb, w_gates, b_gates, w_br_a[l].astype(BF16), w_br_b[l].astype(BF16),
                    w_br_c[l].astype(BF16), w_out[l].astype(BF16), row2(ln1_g[l]), row2(ln1_b[l]), alpha, tm_merge)
        top_idx, top_w, rank, counts = _route(x1, w_router[l].T, router_bias[l].reshape(NE, 1), tm_route)
        counts = counts[:, 0]
        padded = (counts + bm - 1) // bm * bm
        pad_end = jnp.cumsum(padded)
        pad_start = pad_end - padded
        dest = _dest_tiles(top_idx, rank, pad_start, tm_moe)
        blk_e = jnp.minimum(jnp.searchsorted(pad_end, jnp.arange(n_blocks, dtype=I32) * bm, side='right'),
                            NE - 1).astype(I32)
        n_used = (pad_end[-1:] // bm).astype(I32)
        xs = _dispatch(counts, pad_start, dest, x1t, P, tm_moe, bm)
        ys = _experts(blk_e, n_used, xs, w_exp_gu, w_exp_down, l, bm)
        comb = functools.partial(_combine, dest, ys, x1, top_w.T, w_sh_gu[l].astype(BF16), w_sh_down[l].astype(BF16),
                                 row2(ln2_g[l]), row2(ln2_b[l]), alpha, tm_moe)
        if l + 1 < depth or split is None:
            x = comb()
        else:
            return comb(0, split // tm_moe), comb(split // tm_moe, (T - split) // tm_moe)
    return x


def kernel(x_prompt, x_sample, w_in, b_in, sink, conv_w, w_br_a, w_br_b, w_br_c, w_out, ln1_g, ln1_b, w_router,
           router_bias, w_exp_gu, w_exp_down, w_sh_gu, w_sh_down, ln2_g, ln2_b):
    depth = w_in.shape[0]
    n1, s1, _ = x_prompt.shape
    n2, s2, _ = x_sample.shape
    span = QB * max(B_DILS)
    assert s1 % span == 0 and s2 % span == 0, "sequence lengths must be multiples of the widest dilated block"
    x = jnp.concatenate([x_prompt.reshape(n1 * s1, D), x_sample.reshape(n2 * s2, D)], axis=0)
    params = (w_in, b_in, sink, conv_w, w_br_a, w_br_b, w_br_c, w_out, ln1_g, ln1_b, w_router, router_bias,
              w_exp_gu, w_exp_down, w_sh_gu, w_sh_down, ln2_g, ln2_b)
    y1, y2 = _trunk(x, [s1] * n1 + [s2] * n2, depth, params, bm=512, tm_proj=1024, tn_proj=512, tm_merge=256,
                    tm_route=256, tm_moe=512, split=n1 * s1)
    return y1.reshape(n1, s1, D), y2.reshape(n2, s2, D)
```

```python
import functools

import numpy as np
import jax
import jax.numpy as jnp
from jax import lax
from jax.experimental import pallas as pl
from jax.experimental.pallas import tpu as pltpu

F32, BF16, I32, U32 = jnp.float32, jnp.bfloat16, jnp.int32, jnp.uint32

LANES = 128
D = 1024
HD = 64
A_HK, A_G, A_BAND = 2, 4, 128
B_G, B_BAND = 2, 64
B_DILS = (1, 4, 16)
C_H, C_HD, CHUNK = 4, 128, 128
NE, TOPK, NGRP, TOPG, DE = 256, 8, 8, 4, 256
EPG = NE // NGRP
ROUTED_SCALE = 2.5
LN_EPS = 1e-5
NEG = -1e30
QB = 128
SLOT_ROWS = D // (2 * LANES)
COMBINE_CHUNK = 32

U_QA, U_KA, U_VA = 0, 4, 5
U_QB = (6, 8, 10)
U_KVB = (7, 9, 11)
U_QC, U_KC, U_VC, U_OC, U_GC = 12, 16, 20, 24, 28
N_UNITS = 32
NPAD = N_UNITS * LANES


def _edge_flags(seq_lens, rows):
    first, last = [], []
    for s in seq_lens:
        nb = s // rows
        first += [1] + [0] * (nb - 1)
        last += [0] * (nb - 1) + [1]
    return np.array([first, last], np.int32)


def _mm(a, b):
    return jnp.dot(a.astype(BF16), b.astype(BF16), preferred_element_type=F32)


def _mm_nt(a, b):
    return lax.dot_general(a.astype(BF16), b.astype(BF16), (((1,), (1,)), ((), ())), preferred_element_type=F32)


def _mm_tn(a, b):
    return lax.dot_general(a.astype(BF16), b.astype(BF16), (((0,), (0,)), ((), ())), preferred_element_type=F32)


def _split3(x):
    hi = x.astype(BF16)
    r1 = x - hi.astype(F32)
    mid = r1.astype(BF16)
    lo = (r1 - mid.astype(F32)).astype(BF16)
    return hi, mid, lo


def _sigmoid(x):
    return 1.0 / (1.0 + jnp.exp(-x))


def _log_sigmoid(x):
    return jnp.minimum(x, 0.0) - jnp.log(1.0 + jnp.exp(-jnp.abs(x)))


def _pack_tokens(ref, x):
    n = x.shape[0]
    u = lax.bitcast_convert_type(x, U32)
    u = u + (jnp.uint32(0x7FFF) + ((u >> 16) & jnp.uint32(1)))
    for j in range(SLOT_ROWS):
        lo = u[:, j * LANES:(j + 1) * LANES] >> 16
        hi = u[:, D // 2 + j * LANES:D // 2 + (j + 1) * LANES] & jnp.uint32(0xFFFF0000)
        ref[pl.ds(j, n, stride=SLOT_ROWS), :] = lo | hi


def _unpack_tokens(ref):
    n = ref.shape[0] // SLOT_ROWS
    words = [ref[pl.ds(j, n, stride=SLOT_ROWS), :] for j in range(SLOT_ROWS)]
    lo = [lax.bitcast_convert_type(w << 16, F32) for w in words]
    hi = [lax.bitcast_convert_type(w & jnp.uint32(0xFFFF0000), F32) for w in words]
    return jnp.concatenate(lo + hi, axis=1)


def _layer_norm(x, g, b):
    mu = jnp.mean(x, axis=-1, keepdims=True)
    xc = x - mu
    var = jnp.mean(xc * xc, axis=-1, keepdims=True)
    return xc * lax.rsqrt(var + LN_EPS) * g + b


def _two_part_specs(tm, n_a):
    return [pl.BlockSpec((tm, D), lambda i, *_: (jnp.minimum(i, n_a - 1), 0)),
            pl.BlockSpec((tm, D), lambda i, *_: (jnp.maximum(i - n_a, 0), 0))]


def _proj_kernel(xa_ref, xb_ref, w_ref, b_ref, o_ref, x16_ref, *, n_a):
    @pl.when(pl.program_id(1) == 0)
    def _():
        x16_ref[...] = jnp.where(pl.program_id(0) < n_a, xa_ref[...], xb_ref[...]).astype(BF16)

    o_ref[...] = jnp.dot(x16_ref[...], w_ref[...], preferred_element_type=F32) + b_ref[...]


def _proj(xa, xb, rows_a, T, w, b, tm, tn):
    N = w.shape[1]
    n_a = rows_a // tm
    return pl.pallas_call(
        functools.partial(_proj_kernel, n_a=n_a),
        grid=(T // tm, N // tn),
        in_specs=_two_part_specs(tm, n_a)
        + [pl.BlockSpec((D, tn), lambda i, j: (0, j)),
           pl.BlockSpec((1, tn), lambda i, j: (0, j))],
        out_specs=pl.BlockSpec((tm, tn), lambda i, j: (i, j)),
        out_shape=jax.ShapeDtypeStruct((T, N), F32),
        scratch_shapes=[pltpu.VMEM((tm, D), BF16)],
        compiler_params=pltpu.CompilerParams(dimension_semantics=("parallel", "arbitrary")),
        name="proj",
    )(xa, xb, w, b)


def _attn_bias(slopes, g, band, dist):
    rel = np.abs(np.arange(3 * QB)[None, :] - QB - np.arange(QB)[:, None])
    dist_f = (rel * dist).astype(np.float32)
    per_head = [np.where(rel <= band, -(np.float32(sl) * dist_f), np.float32(NEG)) for sl in slopes]
    return np.stack([np.concatenate(per_head[h * g:(h + 1) * g], axis=0) for h in range(len(slopes) // g)])


def _attn_tile(q, kwin, vwin, bias_ref, lo, hi, sinks, *, hk, g, packed):
    qb, W = q.shape[0], kwin.shape[0]
    rows = g * qb
    one_col = jnp.where(lax.broadcasted_iota(I32, (rows, HD), 1) == 0, 1.0, 0.0)
    key = lax.broadcasted_iota(I32, (W, HD), 0)
    edge = jnp.where((key >= lo) & (key < hi), 0.0, NEG)
    row_g = jnp.right_shift(lax.broadcasted_iota(I32, (rows, 1), 0), qb.bit_length() - 1)
    outs, lses = [], []
    for h in range(hk):
        qs = jnp.concatenate([q[:, (h * g + gg) * HD:(h * g + gg + 1) * HD] for gg in range(g)], axis=0)
        qs = jnp.concatenate([qs * (HD ** -0.5), one_col], axis=1)
        ko = h * HD
        vo = HD if packed else h * HD
        kh = jnp.concatenate([kwin[:, ko:ko + HD], edge], axis=1)
        vh = vwin[:, vo:vo + HD]
        logits = _mm_nt(qs, kh) + bias_ref[h]
        m = jnp.max(logits, axis=-1, keepdims=True)
        if sinks is not None:
            sink_col = jnp.zeros((rows, 1), F32)
            for gg in range(g):
                sink_col = jnp.where(row_g == gg, sinks[h * g + gg], sink_col)
            m = jnp.maximum(m, sink_col)
        p = jnp.exp(logits - m)
        den = jnp.sum(p, axis=-1, keepdims=True)
        if sinks is not None:
            den = den + jnp.exp(sink_col - m)
        o = _mm(p, vh) / den
        for gg in range(g):
            outs.append(o[gg * qb:(gg + 1) * qb])
        if packed:
            l = m + jnp.log(den)
            for gg in range(g):
                lses.append(jnp.broadcast_to(l[gg * qb:(gg + 1) * qb], (qb, HD)))
    return outs, lses


def _edge_cols(flags_ref, b):
    return jnp.where(flags_ref[0, b] == 0, 0, QB), jnp.where(flags_ref[1, b] == 0, 3 * QB, 2 * QB)


def _attn_ab0_kernel(flags_ref, sink_ref, qa_ref, kp_ref, ko_ref, kn_ref, vp_ref, vo_ref, vn_ref, qb_ref, kvp_ref,
                     kvo_ref, kvn_ref, bias_a_ref, bias_b_ref, oa_ref, ob_ref, lse_ref):
    lo, hi = _edge_cols(flags_ref, pl.program_id(0))
    sinks = [sink_ref[0, i] for i in range(A_HK * A_G)]
    kwin = jnp.concatenate([kp_ref[...], ko_ref[...], kn_ref[...]], axis=0)
    vwin = jnp.concatenate([vp_ref[...], vo_ref[...], vn_ref[...]], axis=0)
    outs, _ = _attn_tile(qa_ref[...], kwin, vwin, bias_a_ref, lo, hi, sinks, hk=A_HK, g=A_G, packed=False)
    oa_ref[...] = jnp.concatenate(outs, axis=1).astype(oa_ref.dtype)
    kv = jnp.concatenate([kvp_ref[...], kvo_ref[...], kvn_ref[...]], axis=0)
    outs, lses = _attn_tile(qb_ref[...], kv, kv, bias_b_ref, lo, hi, None, hk=1, g=B_G, packed=True)
    ob_ref[...] = jnp.concatenate(outs, axis=1)
    lse_ref[...] = jnp.concatenate(lses, axis=1)


def _attn_ab0(proj, flags, sink, slopes_a, slopes_b0):
    T = proj.shape[0]
    nb = T // QB
    prev = lambda b, f: jnp.maximum(b - 1, 0)
    nxt = lambda b, f: jnp.minimum(b + 1, nb - 1)
    kspecs = lambda u: [pl.BlockSpec((QB, LANES), lambda b, f: (prev(b, f), u)),
                        pl.BlockSpec((QB, LANES), lambda b, f: (b, u)),
                        pl.BlockSpec((QB, LANES), lambda b, f: (nxt(b, f), u))]
    bias_a = jnp.asarray(_attn_bias(slopes_a, A_G, A_BAND, 1))
    bias_b = jnp.asarray(_attn_bias(slopes_b0, B_G, B_BAND, B_DILS[0]))
    full = lambda a: pl.BlockSpec(a.shape, lambda b, f: (0,) * a.ndim)
    wa = A_HK * A_G * HD
    return pl.pallas_call(
        _attn_ab0_kernel,
        grid_spec=pltpu.PrefetchScalarGridSpec(
            num_scalar_prefetch=1, grid=(nb,),
            in_specs=[pl.BlockSpec(memory_space=pltpu.SMEM), pl.BlockSpec((QB, wa), lambda b, f: (b, U_QA // 4))]
            + kspecs(U_KA) + kspecs(U_VA) + [pl.BlockSpec((QB, LANES), lambda b, f: (b, U_QB[0]))] + kspecs(U_KVB[0])
            + [full(bias_a), full(bias_b)],
            out_specs=[pl.BlockSpec((QB, wa), lambda b, f: (b, 0)), pl.BlockSpec((QB, LANES), lambda b, f: (b, 0)),
                       pl.BlockSpec((QB, LANES), lambda b, f: (b, 0))]),
        out_shape=[jax.ShapeDtypeStruct((T, wa), BF16), jax.ShapeDtypeStruct((T, LANES), F32),
                   jax.ShapeDtypeStruct((T, LANES), F32)],
        compiler_params=pltpu.CompilerParams(dimension_semantics=("parallel",)),
        name="attn_ab0",
    )(flags, sink, *([proj] * 11), bias_a, bias_b)


def _attn_b_kernel(flags_ref, q_ref, kp_ref, ko_ref, kn_ref, bias_ref, o_ref, lse_ref, *, dil):
    lo, hi = _edge_cols(flags_ref, pl.program_id(0))

    def body(r, c):
        sub = lambda ref: ref[pl.ds(r, QB, stride=dil), :]
        kv = jnp.concatenate([sub(kp_ref), sub(ko_ref), sub(kn_ref)], axis=0)
        outs, lses = _attn_tile(sub(q_ref), kv, kv, bias_ref, lo, hi, None, hk=1, g=B_G, packed=True)
        o_ref[pl.ds(r, QB, stride=dil), :] = jnp.concatenate(outs, axis=1)
        lse_ref[pl.ds(r, QB, stride=dil), :] = jnp.concatenate(lses, axis=1)
        return c

    lax.fori_loop(0, dil, body, 0, unroll=2)


def _attn_b(proj, flags, gi, slopes):
    dil = B_DILS[gi]
    T = proj.shape[0]
    rows = QB * dil
    nb = T // rows
    uq, ukv = U_QB[gi], U_KVB[gi]
    bias = jnp.asarray(_attn_bias(slopes, B_G, B_BAND, dil))
    return pl.pallas_call(
        functools.partial(_attn_b_kernel, dil=dil),
        grid_spec=pltpu.PrefetchScalarGridSpec(
            num_scalar_prefetch=1, grid=(nb,),
            in_specs=[pl.BlockSpec((rows, LANES), lambda b, f: (b, uq)),
                      pl.BlockSpec((rows, LANES), lambda b, f: (jnp.maximum(b - 1, 0), ukv)),
                      pl.BlockSpec((rows, LANES), lambda b, f: (b, ukv)),
                      pl.BlockSpec((rows, LANES), lambda b, f: (jnp.minimum(b + 1, nb - 1), ukv)),
                      pl.BlockSpec(bias.shape, lambda b, f: (0, 0, 0))],
            out_specs=[pl.BlockSpec((rows, LANES), lambda b, f: (b, 0)),
                       pl.BlockSpec((rows, LANES), lambda b, f: (b, 0))]),
        out_shape=[jax.ShapeDtypeStruct((T, LANES), F32), jax.ShapeDtypeStruct((T, LANES), F32)],
        compiler_params=pltpu.CompilerParams(dimension_semantics=("parallel",)),
        name=f"attn_b{gi}",
    )(flags, proj, proj, proj, proj, bias)


def _conv_silu(x, xp8, xn8, w, keep_prev, keep_next):
    rowi = lax.broadcasted_iota(I32, (CHUNK, 1), 0)
    x_prev = jnp.where(rowi == 0, xp8[7:8, :] * keep_prev, pltpu.roll(x, 1, 0))
    x_next = jnp.where(rowi == CHUNK - 1, xn8[0:1, :] * keep_next, pltpu.roll(x, CHUNK - 1, 0))
    y = w[0:1, :] * x_prev + w[1:2, :] * x + w[2:3, :] * x_next
    return y * _sigmoid(y)


def _mlstm_dir(rev, first, last, q_ref, qp_ref, qn_ref, k_ref, kp_ref, kn_ref, v_ref, g_ref, gt_ref, cw_ref,
               c_ref, n_ref, m_ref, h_ref):
    sd = 1 if rev else 0
    keep_prev = (1 - first).astype(F32)
    keep_next = (1 - last).astype(F32)

    @pl.when((last if rev else first) == 1)
    def _():
        c_ref[sd] = jnp.zeros(c_ref.shape[1:], F32)
        n_ref[sd] = jnp.zeros(n_ref.shape[1:], F32)
        m_ref[sd] = jnp.zeros(m_ref.shape[1:], F32)

    W = C_H * C_HD
    q = _conv_silu(q_ref[...], qp_ref[...], qn_ref[...], cw_ref[:, :W], keep_prev, keep_next)
    k = _conv_silu(k_ref[...], kp_ref[...], kn_ref[...], cw_ref[:, W:], keep_prev, keep_next) * (C_HD ** -0.5)
    v = v_ref[...]
    G = g_ref[...]
    GT = gt_ref[...]
    ti = lax.broadcasted_iota(I32, (CHUNK, CHUNK), 0)
    si = lax.broadcasted_iota(I32, (CHUNK, CHUNK), 1)
    tri = (si >= ti) if rev else (si <= ti)
    L = jnp.where(tri, 1.0, 0.0).astype(BF16)
    Lt = jnp.where((ti >= si) if rev else (ti <= si), 1.0, 0.0).astype(BF16)
    cum_col = sum(jnp.dot(L, p, preferred_element_type=F32) for p in _split3(_log_sigmoid(G)))
    cum_row = sum(jnp.dot(p, Lt, preferred_element_type=F32) for p in _split3(_log_sigmoid(GT)))
    e = 0 if rev else CHUNK - 1
    hs = []
    for h in range(C_H):
        ci = (2 * C_H if rev else 0) + h
        cf = ci + C_H
        cum_c, cum_r = cum_col[:, cf:cf + 1], cum_row[cf:cf + 1, :]
        li_c, li_r = G[:, ci:ci + 1], GT[ci:ci + 1, :]
        qh, kh, vh = (a[:, h * C_HD:(h + 1) * C_HD] for a in (q, k, v))
        m11 = m_ref[sd, h][:, :1]
        Dm = jnp.where(tri, cum_c - cum_r + li_r, NEG)
        m_inter = cum_c + m11
        mt = jnp.maximum(jnp.max(Dm, axis=-1, keepdims=True), m_inter)
        Wm = jnp.exp(Dm - mt) * _mm_nt(qh, kh)
        inter = jnp.exp(m_inter - mt)
        num = _mm(Wm, vh) + inter * _mm(qh, c_ref[sd, h])
        den = jnp.sum(Wm, axis=-1, keepdims=True) + inter * jnp.sum(qh * n_ref[sd, h], axis=-1, keepdims=True)
        hs.append(num / jnp.maximum(jnp.abs(den), jnp.exp(-mt)))
        total = cum_c[e:e + 1, :]
        gcol = total - cum_c + li_c
        m_new = jnp.maximum(total + m11, jnp.max(gcol, axis=0, keepdims=True))
        kw = kh * jnp.exp(gcol - m_new)
        decay = jnp.exp(total + m11 - m_new)
        c_ref[sd, h] = decay * c_ref[sd, h] + _mm_tn(kw, vh)
        n_ref[sd, h] = decay * n_ref[sd, h] + jnp.sum(kw, axis=0, keepdims=True)
        m_ref[sd, h] = jnp.broadcast_to(m_new, (1, LANES))
    h_ref[...] = jnp.concatenate(hs, axis=1)


def _mlstm_kernel(flags_ref, *refs, nb):
    fwd, bwd, (cw_ref, hf_ref, hb_ref, c_ref, n_ref, m_ref) = refs[:9], refs[9:18], refs[18:]
    j = pl.program_id(0)
    jb = nb - 1 - j
    _mlstm_dir(False, flags_ref[0, j], flags_ref[1, j], *fwd, cw_ref, c_ref, n_ref, m_ref, hf_ref)
    _mlstm_dir(True, flags_ref[0, jb], flags_ref[1, jb], *bwd, cw_ref, c_ref, n_ref, m_ref, hb_ref)


def _mlstm(proj, gct, conv_w, flags):
    T = proj.shape[0]
    nb = T // CHUNK
    W = C_H * C_HD
    r8 = CHUNK // 8

    def specs(blk):
        halo_p = lambda j, f: jnp.maximum(blk(j) * r8 - 1, 0)
        halo_n = lambda j, f: jnp.minimum(blk(j) * r8 + r8, nb * r8 - 1)
        out = []
        for u in (U_QC, U_KC):
            out += [pl.BlockSpec((CHUNK, W), lambda j, f, u=u: (blk(j), u // 4)),
                    pl.BlockSpec((8, W), lambda j, f, u=u: (halo_p(j, f), u // 4)),
                    pl.BlockSpec((8, W), lambda j, f, u=u: (halo_n(j, f), u // 4))]
        out += [pl.BlockSpec((CHUNK, W), lambda j, f: (blk(j), U_VC // 4)),
                pl.BlockSpec((CHUNK, LANES), lambda j, f: (blk(j), U_GC)),
                pl.BlockSpec((16, CHUNK), lambda j, f: (0, blk(j)))]
        return out

    fwd_blk = lambda j: j
    bwd_blk = lambda j: nb - 1 - j
    args = [proj] * 8 + [gct]
    return pl.pallas_call(
        functools.partial(_mlstm_kernel, nb=nb),
        grid_spec=pltpu.PrefetchScalarGridSpec(
            num_scalar_prefetch=1, grid=(nb,),
            in_specs=specs(fwd_blk) + specs(bwd_blk) + [pl.BlockSpec((3, 2 * W), lambda j, f: (0, 0))],
            out_specs=[pl.BlockSpec((CHUNK, W), lambda j, f: (j, 0)),
                       pl.BlockSpec((CHUNK, W), lambda j, f: (nb - 1 - j, 0))],
            scratch_shapes=[pltpu.VMEM((2, C_H, C_HD, C_HD), F32),
                            pltpu.VMEM((2, C_H, 1, C_HD), F32),
                            pltpu.VMEM((2, C_H, 1, LANES), F32)]),
        out_shape=[jax.ShapeDtypeStruct((T, W), F32), jax.ShapeDtypeStruct((T, W), F32)],
        compiler_params=pltpu.CompilerParams(dimension_semantics=("arbitrary",)),
        name="mlstm",
    )(flags, *args, *args, conv_w)


def _merge_kernel(xa_ref, xb_ref, oa_ref, ob0_ref, ob1_ref, ob2_ref, l0_ref, l1_ref, l2_ref, hf_ref, hb_ref, oc_ref,
                  wg_ref, bg_ref, wa_ref, wb_ref, wc_ref, wo_ref, lg_ref, lb_ref, o_ref, ot_ref, *, alpha, n_a):
    ya = jnp.dot(oa_ref[...], wa_ref[...], preferred_element_type=F32)
    l0, l1, l2 = l0_ref[...], l1_ref[...], l2_ref[...]
    lm = jnp.maximum(jnp.maximum(l0, l1), l2)
    e0, e1, e2 = jnp.exp(l0 - lm), jnp.exp(l1 - lm), jnp.exp(l2 - lm)
    es = e0 + e1 + e2
    outb = jnp.concatenate([ob0_ref[...] * (e0 / es), ob1_ref[...] * (e1 / es), ob2_ref[...] * (e2 / es)], axis=1)
    yb = _mm(outb, wb_ref[...])
    yc = _mm(_sigmoid(oc_ref[...]) * (hf_ref[...] + hb_ref[...]), wc_ref[...])
    x = jnp.where(pl.program_id(0) < n_a, xa_ref[...], xb_ref[...])
    gates = _sigmoid(_mm(x, wg_ref[...]) + bg_ref[...])
    merged = gates[:, 0:D] * ya + gates[:, D:2 * D] * yb + gates[:, 2 * D:3 * D] * yc
    mix = _mm(merged, wo_ref[...])
    x1 = _layer_norm(alpha * x + mix, lg_ref[...], lb_ref[...])
    o_ref[...] = x1
    _pack_tokens(ot_ref, x1)


def _merge(xa, xb, rows_a, proj, oa, obs, lses, hf, hb, wg, bg, wa, wb, wc, wo, lg, lb, alpha, tm):
    T = proj.shape[0]
    n_a = rows_a // tm
    row = lambda w: pl.BlockSpec((tm, w), lambda i: (i, 0))
    full = lambda a: pl.BlockSpec(a.shape, lambda i: (0,) * a.ndim)
    return pl.pallas_call(
        functools.partial(_merge_kernel, alpha=alpha, n_a=n_a),
        grid=(T // tm,),
        in_specs=_two_part_specs(tm, n_a) + [row(oa.shape[1])] + [row(LANES)] * 6 + [row(hf.shape[1])] * 2
        + [pl.BlockSpec((tm, 512), lambda i: (i, U_OC // 4))]
        + [full(a) for a in (wg, bg, wa, wb, wc, wo, lg, lb)],
        out_specs=[row(D), pl.BlockSpec((tm * SLOT_ROWS, LANES), lambda i: (i, 0))],
        out_shape=[jax.ShapeDtypeStruct((T, D), F32), jax.ShapeDtypeStruct((T * SLOT_ROWS, LANES), U32)],
        compiler_params=pltpu.CompilerParams(dimension_semantics=("parallel",)),
        name="merge",
    )(xa, xb, oa, *obs, *lses, hf, hb, proj, wg, bg, wa, wb, wc, wo, lg, lb)


def _stack_rows(rows, iota):
    out = jnp.zeros(iota.shape, F32)
    for k, r in enumerate(rows):
        out = jnp.where(iota == float(k), r, out)
    return out


def _route_kernel(x_ref, wrt_ref, bias_ref, idx_ref, w_ref, rank_ref, cnt_ref, run_ref, *, tm):
    @pl.when(pl.program_id(0) == 0)
    def _():
        run_ref[...] = jnp.zeros(run_ref.shape, F32)

    x = x_ref[...]
    xh, xm, xl = _split3(x)
    wh, wm, wl = _split3(wrt_ref[...])
    nt = lambda a, b: lax.dot_general(a, b, (((1,), (1,)), ((), ())), preferred_element_type=F32)
    logits = nt(wh, xh) + (nt(wh, xm) + nt(wm, xh)) + (nt(wh, xl) + nt(wl, xh) + nt(wm, xm))
    scores = _sigmoid(logits)
    biased = scores + bias_ref[...]
    ninf = -jnp.inf
    iota_g = lax.broadcasted_iota(I32, (EPG, tm), 0).astype(F32)
    gs = []
    for gi in range(NGRP):
        v = biased[gi * EPG:(gi + 1) * EPG]
        t1 = jnp.max(v, axis=0, keepdims=True)
        i1 = jnp.min(jnp.where(v == t1, iota_g, float(EPG)), axis=0, keepdims=True)
        t2 = jnp.max(jnp.where(iota_g == i1, ninf, v), axis=0, keepdims=True)
        gs.append(t1 + t2)
    iota8 = lax.broadcasted_iota(I32, (NGRP, tm), 0).astype(F32)
    gs = _stack_rows(gs, iota8)
    gsel = jnp.zeros((NGRP, tm), F32)
    for _ in range(TOPG):
        mx = jnp.max(gs, axis=0, keepdims=True)
        ix = jnp.min(jnp.where(gs == mx, iota8, float(NGRP)), axis=0, keepdims=True)
        hit = iota8 == ix
        gsel = jnp.where(hit, 1.0, gsel)
        gs = jnp.where(hit, ninf, gs)
    v = jnp.concatenate([jnp.where(gsel[gi:gi + 1] > 0.0, biased[gi * EPG:(gi + 1) * EPG], ninf)
                         for gi in range(NGRP)], axis=0)
    iota_e = lax.broadcasted_iota(I32, (NE, tm), 0).astype(F32)
    sel = jnp.zeros((NE, tm), F32)
    idxs, svals = [], []
    for _ in range(TOPK):
        mx = jnp.max(v, axis=0, keepdims=True)
        ix = jnp.min(jnp.where(v == mx, iota_e, float(NE)), axis=0, keepdims=True)
        hit = iota_e == ix
        idxs.append(ix)
        svals.append(jnp.sum(jnp.where(hit, scores, 0.0), axis=0, keepdims=True))
        sel = jnp.where(hit, 1.0, sel)
        v = jnp.where(hit, ninf, v)
    top_s = _stack_rows(svals, iota8)
    idx_ref[...] = _stack_rows(idxs, iota8).astype(I32)
    w_ref[...] = top_s / jnp.sum(top_s, axis=0, keepdims=True) * ROUTED_SCALE
    ti = lax.broadcasted_iota(I32, (tm, tm), 0)
    si = lax.broadcasted_iota(I32, (tm, tm), 1)
    before = jnp.where(ti < si, 1.0, 0.0).astype(BF16)
    selb = sel.astype(BF16)
    cnt = run_ref[...] + jnp.dot(selb, before, preferred_element_type=F32)
    rank_ref[...] = _stack_rows(
        [jnp.sum(jnp.where(iota_e == ix, cnt, 0.0), axis=0, keepdims=True) for ix in idxs], iota8).astype(I32)
    run = run_ref[...] + jnp.dot(selb, jnp.ones((tm, tm), BF16), preferred_element_type=F32)
    run_ref[...] = run
    cnt_ref[...] = run.astype(I32)


def _route(x1, wrt, bias_col, tm):
    T = x1.shape[0]
    slot = lambda dt: jax.ShapeDtypeStruct((TOPK, T), dt)
    return pl.pallas_call(
        functools.partial(_route_kernel, tm=tm),
        grid=(T // tm,),
        in_specs=[pl.BlockSpec((tm, D), lambda i: (i, 0)),
                  pl.BlockSpec((NE, D), lambda i: (0, 0)),
                  pl.BlockSpec((NE, 1), lambda i: (0, 0))],
        out_specs=[pl.BlockSpec((TOPK, tm), lambda i: (0, i))] * 3 + [pl.BlockSpec((NE, tm), lambda i: (0, 0))],
        out_shape=[slot(I32), slot(F32), slot(I32), jax.ShapeDtypeStruct((NE, tm), I32)],
        scratch_shapes=[pltpu.VMEM((NE, tm), F32)],
        compiler_params=pltpu.CompilerParams(dimension_semantics=("arbitrary",)),
        name="route",
    )(x1, wrt, bias_col)


def _dest_kernel(idx_ref, rank_ref, start_ref, o_ref, *, tm):
    iota_e = lax.broadcasted_iota(I32, (NE, tm), 0)
    starts = start_ref[...]
    rows = []
    for k in range(TOPK):
        base = jnp.sum(jnp.where(iota_e == idx_ref[k:k + 1, :], starts, 0.0), axis=0, keepdims=True)
        rows.append(base.astype(I32) + rank_ref[k:k + 1, :])
    o_ref[0] = jnp.concatenate(rows, axis=1)


def _dest_tiles(top_idx, rank, pad_start, tm):
    T = top_idx.shape[1]
    return pl.pallas_call(
        functools.partial(_dest_kernel, tm=tm),
        grid=(T // tm,),
        in_specs=[pl.BlockSpec((TOPK, tm), lambda i: (0, i)), pl.BlockSpec((TOPK, tm), lambda i: (0, i)),
                  pl.BlockSpec((NE, 1), lambda i: (0, 0))],
        out_specs=pl.BlockSpec((1, 1, TOPK * tm), lambda i: (i, 0, 0)),
        out_shape=jax.ShapeDtypeStruct((T // tm, 1, TOPK * tm), I32),
        compiler_params=pltpu.CompilerParams(dimension_semantics=("parallel",)),
        name="dest",
    )(top_idx, rank, pad_start.astype(F32).reshape(NE, 1))


def _tile_rows(ref, slot, n=1):
    return ref.at[pl.ds(pl.multiple_of(slot * SLOT_ROWS, SLOT_ROWS), n * SLOT_ROWS)]


def _dispatch_kernel(cnt_ref, start_ref, dest_hbm, x_ref, xs_hbm, dest_smem, zero_ref, sem_idx, sem, sem_z, *, tm, bm):
    i = pl.program_id(0)
    pad_sizes = [1 << k for k in range(bm.bit_length() - 1)]
    half = bm // 2

    def pad_copies(e, fn):
        n = (-cnt_ref[e]) & (bm - 1)
        at = start_ref[e] + cnt_ref[e]
        for sz in pad_sizes:
            @pl.when((n & sz) != 0)
            def _(at=at, sz=sz):
                fn(pltpu.make_async_copy(_tile_rows(zero_ref, 0, sz), _tile_rows(xs_hbm, at, sz), sem_z))
            at = at + (n & sz)

    @pl.when(i == 0)
    def _():
        zero_ref[...] = jnp.zeros(zero_ref.shape, U32)

        def start(e, c):
            pad_copies(e, lambda cp: cp.start())
            return c

        def wait(e, c):
            pad_copies(e, lambda cp: cp.wait())
            return c

        lax.fori_loop(0, NE, start, 0)
        lax.fori_loop(0, NE, wait, 0)

        used = start_ref[NE - 1] + cnt_ref[NE - 1] + ((-cnt_ref[NE - 1]) & (bm - 1))
        tail = lambda c: pltpu.make_async_copy(zero_ref, _tile_rows(xs_hbm, used + c * half, half), sem_z)

        def tail_start(c, z):
            tail(c).start()
            return z

        def tail_wait(c, z):
            tail(c).wait()
            return z

        n_tail = (xs_hbm.shape[0] // SLOT_ROWS - used) // half
        lax.fori_loop(0, n_tail, tail_start, 0)
        lax.fori_loop(0, n_tail, tail_wait, 0)

    cp = pltpu.make_async_copy(dest_hbm.at[i, 0], dest_smem, sem_idx)
    cp.start()
    cp.wait()

    def body(r, c):
        for k in range(TOPK):
            pltpu.make_async_copy(_tile_rows(x_ref, r), _tile_rows(xs_hbm, dest_smem[k * tm + r]), sem).start(
                priority=k % 2)
        return c

    lax.fori_loop(0, tm, body, 0)
    for k in range(TOPK):
        pltpu.make_async_copy(x_ref, _tile_rows(xs_hbm, 0, tm), sem).wait()


def _dispatch(counts, pad_start, dest_tiles, x1t, P, tm, bm):
    T = x1t.shape[0] // SLOT_ROWS
    return pl.pallas_call(
        functools.partial(_dispatch_kernel, tm=tm, bm=bm),
        grid_spec=pltpu.PrefetchScalarGridSpec(
            num_scalar_prefetch=2, grid=(T // tm,),
            in_specs=[pl.BlockSpec(memory_space=pl.ANY),
                      pl.BlockSpec((tm * SLOT_ROWS, LANES), lambda i, c, s: (i, 0))],
            out_specs=pl.BlockSpec(memory_space=pl.ANY),
            scratch_shapes=[pltpu.SMEM((TOPK * tm,), I32), pltpu.VMEM((bm // 2 * SLOT_ROWS, LANES), U32),
                            pltpu.SemaphoreType.DMA, pltpu.SemaphoreType.DMA, pltpu.SemaphoreType.DMA]),
        out_shape=jax.ShapeDtypeStruct((P * SLOT_ROWS, LANES), U32),
        compiler_params=pltpu.CompilerParams(dimension_semantics=("arbitrary",)),
        name="dispatch",
    )(counts, pad_start, dest_tiles, x1t)


def _expert_kernel(blk_e_ref, n_used_ref, x_ref, wgu_ref, wdn_ref, y_ref, wgu_b, wdn_b):
    b = pl.program_id(0)
    live = b < n_used_ref[0]

    @pl.when(live & ((b == 0) | (blk_e_ref[b] != blk_e_ref[jnp.maximum(b - 1, 0)])))
    def _():
        wgu_b[...] = wgu_ref[0, 0].astype(BF16)
        wdn_b[...] = wdn_ref[0, 0].astype(BF16)

    @pl.when(live)
    def _():
        gu = _mm(_unpack_tokens(x_ref), wgu_b[...])
        gate, up = gu[:, :DE], gu[:, DE:]
        _pack_tokens(y_ref, _mm(gate * _sigmoid(gate) * up, wdn_b[...]))

    @pl.when(b >= n_used_ref[0])
    def _():
        y_ref[...] = jnp.zeros(y_ref.shape, U32)


def _experts(blk_e, n_used, xs, wgu, wdn, layer, bm):
    n_blocks = xs.shape[0] // (bm * SLOT_ROWS)
    live = lambda b, n: jnp.minimum(b, n[0] - 1)
    return pl.pallas_call(
        _expert_kernel,
        grid_spec=pltpu.PrefetchScalarGridSpec(
            num_scalar_prefetch=2, grid=(n_blocks,),
            in_specs=[pl.BlockSpec((bm * SLOT_ROWS, LANES), lambda b, e, n: (live(b, n), 0)),
                      pl.BlockSpec((1, 1, D, 2 * DE), lambda b, e, n: (layer, e[live(b, n)], 0, 0)),
                      pl.BlockSpec((1, 1, DE, D), lambda b, e, n: (layer, e[live(b, n)], 0, 0))],
            out_specs=pl.BlockSpec((bm * SLOT_ROWS, LANES), lambda b, e, n: (b, 0)),
            scratch_shapes=[pltpu.VMEM((D, 2 * DE), BF16), pltpu.VMEM((DE, D), BF16)]),
        out_shape=jax.ShapeDtypeStruct(xs.shape, U32),
        compiler_params=pltpu.CompilerParams(dimension_semantics=("arbitrary",)),
        name="experts",
    )(blk_e, n_used, xs, wgu, wdn)


def _combine_kernel(dest_hbm, y_hbm, x_ref, w_ref, wsg_ref, wsd_ref, lg_ref, lb_ref, o_ref, dest_a, dest_b, rows_ref,
                    routed_ref, sem_idx, sem, *, tm, alpha, tile0, n_tiles, chunk):
    g = pl.program_id(0)
    t_next = jnp.minimum(g + 1, n_tiles - 1) + tile0

    dest_smem = (dest_a, dest_b)

    def slots_copy(tile, s):
        return pltpu.make_async_copy(dest_hbm.at[tile, 0], dest_smem[s], sem_idx.at[s])

    def issue(s, r):
        for k in range(TOPK):
            pltpu.make_async_copy(_tile_rows(y_hbm, dest_smem[s][k * tm + r]), _tile_rows(rows_ref.at[s, k], r),
                                  sem.at[s]).start(priority=k % 2)

    def wait_rows(s):
        for k in range(TOPK):
            pltpu.make_async_copy(_tile_rows(y_hbm, 0, tm), rows_ref.at[s, k], sem.at[s]).wait()

    @pl.when(g == 0)
    def _():
        cp = slots_copy(tile0, 0)
        cp.start()
        cp.wait()

        def first(r, c):
            issue(0, r)
            return c

        lax.fori_loop(0, tm, first, 0)

    def step(cur, nxt):
        slots_next = slots_copy(t_next, nxt)
        slots_next.start()
        x = x_ref[...]
        gu = _mm(x, wsg_ref[...])
        gate, up = gu[:, :DE], gu[:, DE:]
        shared = _mm(gate * _sigmoid(gate) * up, wsd_ref[...])
        slots_next.wait()
        wait_rows(cur)

        def chunk_body(c, carry):
            r0 = pl.multiple_of(c * chunk, chunk)
            acc = None
            for k in range(TOPK):
                yk = _unpack_tokens(rows_ref.at[cur, k, pl.ds(r0 * SLOT_ROWS, chunk * SLOT_ROWS)])
                term = yk * w_ref[pl.ds(r0, chunk), k:k + 1]
                acc = term if acc is None else acc + term
            routed_ref[pl.ds(r0, chunk), :] = acc
            for rr in range(chunk):
                issue(nxt, r0 + rr)
            return carry

        lax.fori_loop(0, tm // chunk, chunk_body, 0)
        o_ref[...] = _layer_norm(alpha * x + (routed_ref[...] + shared), lg_ref[...], lb_ref[...])

        @pl.when(g == n_tiles - 1)
        def _():
            wait_rows(nxt)

    for parity in range(2):
        pl.when(g % 2 == parity)(functools.partial(step, parity, 1 - parity))


def _combine(dest_tiles, ys, x1, w_tok, wsg, wsd, lg, lb, alpha, tm, tile0=0, n_tiles=None):
    n_tiles = x1.shape[0] // tm if n_tiles is None else n_tiles
    full = lambda a: pl.BlockSpec(a.shape, lambda i: (0,) * a.ndim)
    return pl.pallas_call(
        functools.partial(_combine_kernel, tm=tm, alpha=alpha, tile0=tile0, n_tiles=n_tiles, chunk=COMBINE_CHUNK),
        grid=(n_tiles,),
        in_specs=[pl.BlockSpec(memory_space=pl.ANY), pl.BlockSpec(memory_space=pl.ANY),
                  pl.BlockSpec((tm, D), lambda i: (i + tile0, 0)), pl.BlockSpec((tm, TOPK), lambda i: (i + tile0, 0))]
        + [full(a) for a in (wsg, wsd, lg, lb)],
        out_specs=pl.BlockSpec((tm, D), lambda i: (i, 0)),
        out_shape=jax.ShapeDtypeStruct((n_tiles * tm, D), F32),
        scratch_shapes=[pltpu.SMEM((TOPK * tm,), I32), pltpu.SMEM((TOPK * tm,), I32),
                        pltpu.VMEM((2, TOPK, tm * SLOT_ROWS, LANES), U32),
                        pltpu.VMEM((tm, D), F32), pltpu.SemaphoreType.DMA((2,)), pltpu.SemaphoreType.DMA((2,))],
        compiler_params=pltpu.CompilerParams(dimension_semantics=("arbitrary",)),
        name="combine",
    )(dest_tiles, ys, x1, w_tok, wsg, wsd, lg, lb)


def _regroup_w_in(w, b):
    offs = np.cumsum([0, 512, 128, 128, 384, 192, 192, 512, 512, 512, 512, 16, 3 * D])
    qa, ka, va, qb, kb, vb, qc, kc, vc, oc, gc, gates = (int(o) for o in offs[:-1])
    pieces = [(qa, ka), (ka, va), (va, qb)]
    for gi in range(3):
        pieces += [(qb + gi * 128, qb + (gi + 1) * 128), (kb + gi * HD, kb + (gi + 1) * HD),
                   (vb + gi * HD, vb + (gi + 1) * HD)]
    pieces += [(qc, kc), (kc, vc), (vc, oc), (oc, gc), (gc, gates), None]
    gc_pad = (N_UNITS - U_GC) * LANES - (gates - gc)

    def build(a):
        parts = [jnp.zeros(a.shape[:-1] + (gc_pad,), a.dtype) if p is None else a[..., p[0]:p[1]] for p in pieces]
        return jnp.concatenate(parts, axis=-1)

    return (build(w).astype(BF16), build(b).reshape(1, NPAD), w[:, gates:].astype(BF16), b[gates:].reshape(1, 3 * D))


def _trunk(xa, xb, seq_lens, depth, params, *, bm, tm_proj, tn_proj, tm_merge, tm_route, tm_moe, split=None):
    (w_in, b_in, sink, conv_w, w_br_a, w_br_b, w_br_c, w_out, ln1_g, ln1_b, w_router, router_bias, w_exp_gu,
     w_exp_down, w_sh_gu, w_sh_down, ln2_g, ln2_b) = params
    rows_a = xa.shape[0]
    T = rows_a + xb.shape[0]
    alpha = float((2 * depth) ** 0.25)
    slopes_a = tuple(float(2.0 ** (-8.0 * i / 8)) for i in range(1, 9))
    slopes_b = tuple(float(2.0 ** (-8.0 * i / 6)) for i in range(1, 7))
    flags1 = jnp.asarray(_edge_flags(seq_lens, QB))
    flags_b = [jnp.asarray(_edge_flags(seq_lens, QB * d)) for d in B_DILS]
    A = T * TOPK
    n_blocks = A // bm + NE
    P = n_blocks * bm
    row2 = lambda a: a.reshape(1, -1)
    for l in range(depth):
        w_all, b_all, w_gates, b_gates = _regroup_w_in(w_in[l], b_in[l])
        proj = _proj(xa, xb, rows_a, T, w_all, b_all, tm_proj, tn_proj)
        oa, ob0, lse0 = _attn_ab0(proj, flags1, sink[l].reshape(1, A_HK * A_G), slopes_a, slopes_b[:B_G])
        obs, lses = [ob0], [lse0]
        for gi in (1, 2):
            o, s = _attn_b(proj, flags_b[gi], gi, slopes_b[gi * B_G:(gi + 1) * B_G])
            obs.append(o)
            lses.append(s)
        gct = proj[:, U_GC * LANES:U_GC * LANES + 16].T
        hf, hb = _mlstm(proj, gct, conv_w[l], flags1)
        x1, x1t = _merge(xa, xb, rows_a, proj, oa, obs, lses, hf, hb, w_gates, b_gates, w_br_a[l].astype(BF16),
                         w_br_b[l].astype(BF16), w_br_c[l].astype(BF16), w_out[l].astype(BF16), row2(ln1_g[l]),
                         row2(ln1_b[l]), alpha, tm_merge)
        top_idx, top_w, rank, counts = _route(x1, w_router[l].T, router_bias[l].reshape(NE, 1), tm_route)
        counts = counts[:, 0]
        padded = (counts + bm - 1) // bm * bm
        pad_end = jnp.cumsum(padded)
        pad_start = pad_end - padded
        dest = _dest_tiles(top_idx, rank, pad_start, tm_moe)
        blk_e = jnp.minimum(jnp.searchsorted(pad_end, jnp.arange(n_blocks, dtype=I32) * bm, side='right'),
                            NE - 1).astype(I32)
        n_used = (pad_end[-1:] // bm).astype(I32)
        xs = _dispatch(counts, pad_start, dest, x1t, P, tm_moe, bm)
        ys = _experts(blk_e, n_used, xs, w_exp_gu, w_exp_down, l, bm)
        comb = functools.partial(_combine, dest, ys, x1, top_w.T, w_sh_gu[l].astype(BF16), w_sh_down[l].astype(BF16),
                                 row2(ln2_g[l]), row2(ln2_b[l]), alpha, tm_moe)
        if l + 1 < depth or split is None:
            xa = xb = comb()
            rows_a = T
        else:
            return comb(0, split // tm_moe), comb(split // tm_moe, (T - split) // tm_moe)
    return xa


def kernel(x_prompt, x_sample, w_in, b_in, sink, conv_w, w_br_a, w_br_b, w_br_c, w_out, ln1_g, ln1_b, w_router,
           router_bias, w_exp_gu, w_exp_down, w_sh_gu, w_sh_down, ln2_g, ln2_b):
    depth = w_in.shape[0]
    n1, s1, _ = x_prompt.shape
    n2, s2, _ = x_sample.shape
    span = QB * max(B_DILS)
    assert s1 % span == 0 and s2 % span == 0, "sequence lengths must be multiples of the widest dilated block"
    params = (w_in, b_in, sink, conv_w, w_br_a, w_br_b, w_br_c, w_out, ln1_g, ln1_b, w_router, router_bias,
              w_exp_gu, w_exp_down, w_sh_gu, w_sh_down, ln2_g, ln2_b)
    y1, y2 = _trunk(x_prompt.reshape(n1 * s1, D), x_sample.reshape(n2 * s2, D), [s1] * n1 + [s2] * n2, depth, params,
                    bm=1024, tm_proj=1024, tn_proj=1024, tm_merge=256, tm_route=256, tm_moe=512, split=n1 * s1)
    return y1.reshape(n1, s1, D), y2.reshape(n2, s2, D)
```

```python
import functools

import numpy as np
import jax
import jax.numpy as jnp
from jax import lax
from jax.experimental import pallas as pl
from jax.experimental.pallas import tpu as pltpu

F32, BF16, I32, U32 = jnp.float32, jnp.bfloat16, jnp.int32, jnp.uint32

LANES = 128
D = 1024
HD = 64
A_HK, A_G, A_BAND = 2, 4, 128
B_G, B_BAND = 2, 64
B_DILS = (1, 4, 16)
C_H, C_HD, CHUNK = 4, 128, 128
NE, TOPK, NGRP, TOPG, DE = 256, 8, 8, 4, 256
EPG = NE // NGRP
ROUTED_SCALE = 2.5
LN_EPS = 1e-5
NEG = -1e30
QB = 128
SLOT_ROWS = D // (2 * LANES)
COMBINE_CHUNK = 32

U_QA, U_KA, U_VA = 0, 4, 5
U_QB = (6, 8, 10)
U_KVB = (7, 9, 11)
U_QC, U_KC, U_VC, U_OC, U_GC = 12, 16, 20, 24, 28
N_UNITS = 32
NPAD = N_UNITS * LANES


def _edge_flags(seq_lens, rows):
    first, last = [], []
    for s in seq_lens:
        nb = s // rows
        first += [1] + [0] * (nb - 1)
        last += [0] * (nb - 1) + [1]
    return np.array([first, last], np.int32)


def _mm(a, b):
    return jnp.dot(a.astype(BF16), b.astype(BF16), preferred_element_type=F32)


def _mm_nt(a, b):
    return lax.dot_general(a.astype(BF16), b.astype(BF16), (((1,), (1,)), ((), ())), preferred_element_type=F32)


def _mm_tn(a, b):
    return lax.dot_general(a.astype(BF16), b.astype(BF16), (((0,), (0,)), ((), ())), preferred_element_type=F32)


def _split3(x):
    hi = x.astype(BF16)
    r1 = x - hi.astype(F32)
    mid = r1.astype(BF16)
    lo = (r1 - mid.astype(F32)).astype(BF16)
    return hi, mid, lo


def _sigmoid(x):
    return 1.0 / (1.0 + jnp.exp(-x))


def _log_sigmoid(x):
    return jnp.minimum(x, 0.0) - jnp.log(1.0 + jnp.exp(-jnp.abs(x)))


def _pack_tokens(ref, x):
    n = x.shape[0]
    u = lax.bitcast_convert_type(x, U32)
    u = u + (jnp.uint32(0x7FFF) + ((u >> 16) & jnp.uint32(1)))
    for j in range(SLOT_ROWS):
        lo = u[:, j * LANES:(j + 1) * LANES] >> 16
        hi = u[:, D // 2 + j * LANES:D // 2 + (j + 1) * LANES] & jnp.uint32(0xFFFF0000)
        ref[pl.ds(j, n, stride=SLOT_ROWS), :] = lo | hi


def _unpack_tokens(ref):
    n = ref.shape[0] // SLOT_ROWS
    words = [ref[pl.ds(j, n, stride=SLOT_ROWS), :] for j in range(SLOT_ROWS)]
    lo = [lax.bitcast_convert_type(w << 16, F32) for w in words]
    hi = [lax.bitcast_convert_type(w & jnp.uint32(0xFFFF0000), F32) for w in words]
    return jnp.concatenate(lo + hi, axis=1)


def _layer_norm(x, g, b):
    mu = jnp.mean(x, axis=-1, keepdims=True)
    xc = x - mu
    var = jnp.mean(xc * xc, axis=-1, keepdims=True)
    return xc * lax.rsqrt(var + LN_EPS) * g + b


def _two_part_specs(tm, n_a):
    return [pl.BlockSpec((tm, D), lambda i, *_: (jnp.minimum(i, n_a - 1), 0)),
            pl.BlockSpec((tm, D), lambda i, *_: (jnp.maximum(i - n_a, 0), 0))]


def _proj_kernel(xa_ref, xb_ref, w_ref, b_ref, o_ref, x16_ref, *, n_a):
    @pl.when(pl.program_id(1) == 0)
    def _():
        x16_ref[...] = jnp.where(pl.program_id(0) < n_a, xa_ref[...], xb_ref[...]).astype(BF16)

    o_ref[...] = jnp.dot(x16_ref[...], w_ref[...], preferred_element_type=F32) + b_ref[...]


def _proj(xa, xb, rows_a, T, w, b, tm, tn):
    N = w.shape[1]
    n_a = rows_a // tm
    return pl.pallas_call(
        functools.partial(_proj_kernel, n_a=n_a),
        grid=(T // tm, N // tn),
        in_specs=_two_part_specs(tm, n_a)
        + [pl.BlockSpec((D, tn), lambda i, j: (0, j)),
           pl.BlockSpec((1, tn), lambda i, j: (0, j))],
        out_specs=pl.BlockSpec((tm, tn), lambda i, j: (i, j)),
        out_shape=jax.ShapeDtypeStruct((T, N), F32),
        scratch_shapes=[pltpu.VMEM((tm, D), BF16)],
        compiler_params=pltpu.CompilerParams(dimension_semantics=("parallel", "arbitrary")),
        name="proj",
    )(xa, xb, w, b)


def _attn_bias(slopes, g, band, dist):
    rel = np.abs(np.arange(3 * QB)[None, :] - QB - np.arange(QB)[:, None])
    dist_f = (rel * dist).astype(np.float32)
    per_head = [np.where(rel <= band, -(np.float32(sl) * dist_f), np.float32(NEG)) for sl in slopes]
    return np.stack([np.concatenate(per_head[h * g:(h + 1) * g], axis=0) for h in range(len(slopes) // g)])


def _attn_tile(q, kwin, vwin, bias_ref, lo, hi, sinks, *, hk, g, packed):
    qb, W = q.shape[0], kwin.shape[0]
    rows = g * qb
    one_col = jnp.where(lax.broadcasted_iota(I32, (rows, HD), 1) == 0, 1.0, 0.0)
    key = lax.broadcasted_iota(I32, (W, HD), 0)
    edge = jnp.where((key >= lo) & (key < hi), 0.0, NEG)
    row_g = jnp.right_shift(lax.broadcasted_iota(I32, (rows, 1), 0), qb.bit_length() - 1)
    outs, lses = [], []
    for h in range(hk):
        qs = jnp.concatenate([q[:, (h * g + gg) * HD:(h * g + gg + 1) * HD] for gg in range(g)], axis=0)
        qs = jnp.concatenate([qs * (HD ** -0.5), one_col], axis=1)
        ko = h * HD
        vo = HD if packed else h * HD
        kh = jnp.concatenate([kwin[:, ko:ko + HD], edge], axis=1)
        vh = vwin[:, vo:vo + HD]
        logits = _mm_nt(qs, kh) + bias_ref[h]
        m = jnp.max(logits, axis=-1, keepdims=True)
        if sinks is not None:
            sink_col = jnp.zeros((rows, 1), F32)
            for gg in range(g):
                sink_col = jnp.where(row_g == gg, sinks[h * g + gg], sink_col)
            m = jnp.maximum(m, sink_col)
        p = jnp.exp(logits - m)
        den = jnp.sum(p, axis=-1, keepdims=True)
        if sinks is not None:
            den = den + jnp.exp(sink_col - m)
        o = _mm(p, vh) / den
        for gg in range(g):
            outs.append(o[gg * qb:(gg + 1) * qb])
        if packed:
            l = m + jnp.log(den)
            for gg in range(g):
                lses.append(jnp.broadcast_to(l[gg * qb:(gg + 1) * qb], (qb, HD)))
    return outs, lses


def _edge_cols(flags_ref, b):
    return jnp.where(flags_ref[0, b] == 0, 0, QB), jnp.where(flags_ref[1, b] == 0, 3 * QB, 2 * QB)


def _attn_ab0_kernel(flags_ref, sink_ref, qa_ref, kp_ref, ko_ref, kn_ref, vp_ref, vo_ref, vn_ref, qb_ref, kvp_ref,
                     kvo_ref, kvn_ref, bias_a_ref, bias_b_ref, oa_ref, ob_ref, lse_ref):
    lo, hi = _edge_cols(flags_ref, pl.program_id(0))
    sinks = [sink_ref[0, i] for i in range(A_HK * A_G)]
    kwin = jnp.concatenate([kp_ref[...], ko_ref[...], kn_ref[...]], axis=0)
    vwin = jnp.concatenate([vp_ref[...], vo_ref[...], vn_ref[...]], axis=0)
    outs, _ = _attn_tile(qa_ref[...], kwin, vwin, bias_a_ref, lo, hi, sinks, hk=A_HK, g=A_G, packed=False)
    oa_ref[...] = jnp.concatenate(outs, axis=1).astype(oa_ref.dtype)
    kv = jnp.concatenate([kvp_ref[...], kvo_ref[...], kvn_ref[...]], axis=0)
    outs, lses = _attn_tile(qb_ref[...], kv, kv, bias_b_ref, lo, hi, None, hk=1, g=B_G, packed=True)
    ob_ref[...] = jnp.concatenate(outs, axis=1)
    lse_ref[...] = jnp.concatenate(lses, axis=1)


def _attn_ab0(proj, flags, sink, slopes_a, slopes_b0):
    T = proj.shape[0]
    nb = T // QB
    prev = lambda b, f: jnp.maximum(b - 1, 0)
    nxt = lambda b, f: jnp.minimum(b + 1, nb - 1)
    kspecs = lambda u: [pl.BlockSpec((QB, LANES), lambda b, f: (prev(b, f), u)),
                        pl.BlockSpec((QB, LANES), lambda b, f: (b, u)),
                        pl.BlockSpec((QB, LANES), lambda b, f: (nxt(b, f), u))]
    bias_a = jnp.asarray(_attn_bias(slopes_a, A_G, A_BAND, 1))
    bias_b = jnp.asarray(_attn_bias(slopes_b0, B_G, B_BAND, B_DILS[0]))
    full = lambda a: pl.BlockSpec(a.shape, lambda b, f: (0,) * a.ndim)
    wa = A_HK * A_G * HD
    return pl.pallas_call(
        _attn_ab0_kernel,
        grid_spec=pltpu.PrefetchScalarGridSpec(
            num_scalar_prefetch=1, grid=(nb,),
            in_specs=[pl.BlockSpec(memory_space=pltpu.SMEM), pl.BlockSpec((QB, wa), lambda b, f: (b, U_QA // 4))]
            + kspecs(U_KA) + kspecs(U_VA) + [pl.BlockSpec((QB, LANES), lambda b, f: (b, U_QB[0]))] + kspecs(U_KVB[0])
            + [full(bias_a), full(bias_b)],
            out_specs=[pl.BlockSpec((QB, wa), lambda b, f: (b, 0)), pl.BlockSpec((QB, LANES), lambda b, f: (b, 0)),
                       pl.BlockSpec((QB, LANES), lambda b, f: (b, 0))]),
        out_shape=[jax.ShapeDtypeStruct((T, wa), BF16), jax.ShapeDtypeStruct((T, LANES), F32),
                   jax.ShapeDtypeStruct((T, LANES), F32)],
        compiler_params=pltpu.CompilerParams(dimension_semantics=("parallel",)),
        name="attn_ab0",
    )(flags, sink, *([proj] * 11), bias_a, bias_b)


def _attn_b_kernel(flags_ref, q_ref, kp_ref, ko_ref, kn_ref, bias_ref, o_ref, lse_ref, *, dil):
    lo, hi = _edge_cols(flags_ref, pl.program_id(0))

    def body(r, c):
        sub = lambda ref: ref[pl.ds(r, QB, stride=dil), :]
        kv = jnp.concatenate([sub(kp_ref), sub(ko_ref), sub(kn_ref)], axis=0)
        outs, lses = _attn_tile(sub(q_ref), kv, kv, bias_ref, lo, hi, None, hk=1, g=B_G, packed=True)
        o_ref[pl.ds(r, QB, stride=dil), :] = jnp.concatenate(outs, axis=1)
        lse_ref[pl.ds(r, QB, stride=dil), :] = jnp.concatenate(lses, axis=1)
        return c

    lax.fori_loop(0, dil, body, 0, unroll=2)


def _attn_b(proj, flags, gi, slopes):
    dil = B_DILS[gi]
    T = proj.shape[0]
    rows = QB * dil
    nb = T // rows
    uq, ukv = U_QB[gi], U_KVB[gi]
    bias = jnp.asarray(_attn_bias(slopes, B_G, B_BAND, dil))
    return pl.pallas_call(
        functools.partial(_attn_b_kernel, dil=dil),
        grid_spec=pltpu.PrefetchScalarGridSpec(
            num_scalar_prefetch=1, grid=(nb,),
            in_specs=[pl.BlockSpec((rows, LANES), lambda b, f: (b, uq)),
                      pl.BlockSpec((rows, LANES), lambda b, f: (jnp.maximum(b - 1, 0), ukv)),
                      pl.BlockSpec((rows, LANES), lambda b, f: (b, ukv)),
                      pl.BlockSpec((rows, LANES), lambda b, f: (jnp.minimum(b + 1, nb - 1), ukv)),
                      pl.BlockSpec(bias.shape, lambda b, f: (0, 0, 0))],
            out_specs=[pl.BlockSpec((rows, LANES), lambda b, f: (b, 0)),
                       pl.BlockSpec((rows, LANES), lambda b, f: (b, 0))]),
        out_shape=[jax.ShapeDtypeStruct((T, LANES), F32), jax.ShapeDtypeStruct((T, LANES), F32)],
        compiler_params=pltpu.CompilerParams(dimension_semantics=("parallel",)),
        name=f"attn_b{gi}",
    )(flags, proj, proj, proj, proj, bias)


def _conv_silu(x, xp8, xn8, w, keep_prev, keep_next):
    rowi = lax.broadcasted_iota(I32, (CHUNK, 1), 0)
    x_prev = jnp.where(rowi == 0, xp8[7:8, :] * keep_prev, pltpu.roll(x, 1, 0))
    x_next = jnp.where(rowi == CHUNK - 1, xn8[0:1, :] * keep_next, pltpu.roll(x, CHUNK - 1, 0))
    y = w[0:1, :] * x_prev + w[1:2, :] * x + w[2:3, :] * x_next
    return y * _sigmoid(y)


def _mlstm_dir(rev, first, last, q_ref, qp_ref, qn_ref, k_ref, kp_ref, kn_ref, v_ref, g_ref, gt_ref, cw_ref,
               c_ref, n_ref, m_ref, h_ref):
    sd = 1 if rev else 0
    keep_prev = (1 - first).astype(F32)
    keep_next = (1 - last).astype(F32)

    @pl.when((last if rev else first) == 1)
    def _():
        c_ref[sd] = jnp.zeros(c_ref.shape[1:], F32)
        n_ref[sd] = jnp.zeros(n_ref.shape[1:], F32)
        m_ref[sd] = jnp.zeros(m_ref.shape[1:], F32)

    W = C_H * C_HD
    q = _conv_silu(q_ref[...], qp_ref[...], qn_ref[...], cw_ref[:, :W], keep_prev, keep_next)
    k = _conv_silu(k_ref[...], kp_ref[...], kn_ref[...], cw_ref[:, W:], keep_prev, keep_next) * (C_HD ** -0.5)
    v = v_ref[...]
    G = g_ref[...]
    GT = gt_ref[...]
    ti = lax.broadcasted_iota(I32, (CHUNK, CHUNK), 0)
    si = lax.broadcasted_iota(I32, (CHUNK, CHUNK), 1)
    tri = (si >= ti) if rev else (si <= ti)
    L = jnp.where(tri, 1.0, 0.0).astype(BF16)
    Lt = jnp.where((ti >= si) if rev else (ti <= si), 1.0, 0.0).astype(BF16)
    cum_col = sum(jnp.dot(L, p, preferred_element_type=F32) for p in _split3(_log_sigmoid(G)))
    cum_row = sum(jnp.dot(p, Lt, preferred_element_type=F32) for p in _split3(_log_sigmoid(GT)))
    e = 0 if rev else CHUNK - 1
    hs = []
    for h in range(C_H):
        ci = (2 * C_H if rev else 0) + h
        cf = ci + C_H
        cum_c, cum_r = cum_col[:, cf:cf + 1], cum_row[cf:cf + 1, :]
        li_c, li_r = G[:, ci:ci + 1], GT[ci:ci + 1, :]
        qh, kh, vh = (a[:, h * C_HD:(h + 1) * C_HD] for a in (q, k, v))
        m11 = m_ref[sd, h][:, :1]
        Dm = jnp.where(tri, cum_c - cum_r + li_r, NEG)
        m_inter = cum_c + m11
        mt = jnp.maximum(jnp.max(Dm, axis=-1, keepdims=True), m_inter)
        Wm = jnp.exp(Dm - mt) * _mm_nt(qh, kh)
        inter = jnp.exp(m_inter - mt)
        num = _mm(Wm, vh) + inter * _mm(qh, c_ref[sd, h])
        den = jnp.sum(Wm, axis=-1, keepdims=True) + inter * jnp.sum(qh * n_ref[sd, h], axis=-1, keepdims=True)
        hs.append(num / jnp.maximum(jnp.abs(den), jnp.exp(-mt)))
        total = cum_c[e:e + 1, :]
        gcol = total - cum_c + li_c
        m_new = jnp.maximum(total + m11, jnp.max(gcol, axis=0, keepdims=True))
        kw = kh * jnp.exp(gcol - m_new)
        decay = jnp.exp(total + m11 - m_new)
        c_ref[sd, h] = decay * c_ref[sd, h] + _mm_tn(kw, vh)
        n_ref[sd, h] = decay * n_ref[sd, h] + jnp.sum(kw, axis=0, keepdims=True)
        m_ref[sd, h] = jnp.broadcast_to(m_new, (1, LANES))
    h_ref[...] = jnp.concatenate(hs, axis=1)


def _mlstm_kernel(flags_ref, *refs, nb):
    fwd, bwd, (cw_ref, hf_ref, hb_ref, c_ref, n_ref, m_ref) = refs[:9], refs[9:18], refs[18:]
    j = pl.program_id(0)
    jb = nb - 1 - j
    _mlstm_dir(False, flags_ref[0, j], flags_ref[1, j], *fwd, cw_ref, c_ref, n_ref, m_ref, hf_ref)
    _mlstm_dir(True, flags_ref[0, jb], flags_ref[1, jb], *bwd, cw_ref, c_ref, n_ref, m_ref, hb_ref)


def _mlstm(proj, gct, conv_w, flags):
    T = proj.shape[0]
    nb = T // CHUNK
    W = C_H * C_HD
    r8 = CHUNK // 8

    def specs(blk):
        halo_p = lambda j, f: jnp.maximum(blk(j) * r8 - 1, 0)
        halo_n = lambda j, f: jnp.minimum(blk(j) * r8 + r8, nb * r8 - 1)
        out = []
        for u in (U_QC, U_KC):
            out += [pl.BlockSpec((CHUNK, W), lambda j, f, u=u: (blk(j), u // 4)),
                    pl.BlockSpec((8, W), lambda j, f, u=u: (halo_p(j, f), u // 4)),
                    pl.BlockSpec((8, W), lambda j, f, u=u: (halo_n(j, f), u // 4))]
        out += [pl.BlockSpec((CHUNK, W), lambda j, f: (blk(j), U_VC // 4)),
                pl.BlockSpec((CHUNK, LANES), lambda j, f: (blk(j), U_GC)),
                pl.BlockSpec((16, CHUNK), lambda j, f: (0, blk(j)))]
        return out

    fwd_blk = lambda j: j
    bwd_blk = lambda j: nb - 1 - j
    args = [proj] * 8 + [gct]
    return pl.pallas_call(
        functools.partial(_mlstm_kernel, nb=nb),
        grid_spec=pltpu.PrefetchScalarGridSpec(
            num_scalar_prefetch=1, grid=(nb,),
            in_specs=specs(fwd_blk) + specs(bwd_blk) + [pl.BlockSpec((3, 2 * W), lambda j, f: (0, 0))],
            out_specs=[pl.BlockSpec((CHUNK, W), lambda j, f: (j, 0)),
                       pl.BlockSpec((CHUNK, W), lambda j, f: (nb - 1 - j, 0))],
            scratch_shapes=[pltpu.VMEM((2, C_H, C_HD, C_HD), F32),
                            pltpu.VMEM((2, C_H, 1, C_HD), F32),
                            pltpu.VMEM((2, C_H, 1, LANES), F32)]),
        out_shape=[jax.ShapeDtypeStruct((T, W), F32), jax.ShapeDtypeStruct((T, W), F32)],
        compiler_params=pltpu.CompilerParams(dimension_semantics=("arbitrary",)),
        name="mlstm",
    )(flags, *args, *args, conv_w)


def _merge_kernel(xa_ref, xb_ref, oa_ref, ob0_ref, ob1_ref, ob2_ref, l0_ref, l1_ref, l2_ref, hf_ref, hb_ref, oc_ref,
                  wg_ref, bg_ref, wa_ref, wb_ref, wc_ref, wo_ref, lg_ref, lb_ref, o_ref, ot_ref, *, alpha, n_a):
    ya = jnp.dot(oa_ref[...], wa_ref[...], preferred_element_type=F32)
    l0, l1, l2 = l0_ref[...], l1_ref[...], l2_ref[...]
    lm = jnp.maximum(jnp.maximum(l0, l1), l2)
    e0, e1, e2 = jnp.exp(l0 - lm), jnp.exp(l1 - lm), jnp.exp(l2 - lm)
    es = e0 + e1 + e2
    outb = jnp.concatenate([ob0_ref[...] * (e0 / es), ob1_ref[...] * (e1 / es), ob2_ref[...] * (e2 / es)], axis=1)
    yb = _mm(outb, wb_ref[...])
    yc = _mm(_sigmoid(oc_ref[...]) * (hf_ref[...] + hb_ref[...]), wc_ref[...])
    x = jnp.where(pl.program_id(0) < n_a, xa_ref[...], xb_ref[...])
    gates = _sigmoid(_mm(x, wg_ref[...]) + bg_ref[...])
    merged = gates[:, 0:D] * ya + gates[:, D:2 * D] * yb + gates[:, 2 * D:3 * D] * yc
    mix = _mm(merged, wo_ref[...])
    x1 = _layer_norm(alpha * x + mix, lg_ref[...], lb_ref[...])
    o_ref[...] = x1
    _pack_tokens(ot_ref, x1)


def _merge(xa, xb, rows_a, proj, oa, obs, lses, hf, hb, wg, bg, wa, wb, wc, wo, lg, lb, alpha, tm):
    T = proj.shape[0]
    n_a = rows_a // tm
    row = lambda w: pl.BlockSpec((tm, w), lambda i: (i, 0))
    full = lambda a: pl.BlockSpec(a.shape, lambda i: (0,) * a.ndim)
    return pl.pallas_call(
        functools.partial(_merge_kernel, alpha=alpha, n_a=n_a),
        grid=(T // tm,),
        in_specs=_two_part_specs(tm, n_a) + [row(oa.shape[1])] + [row(LANES)] * 6 + [row(hf.shape[1])] * 2
        + [pl.BlockSpec((tm, 512), lambda i: (i, U_OC // 4))]
        + [full(a) for a in (wg, bg, wa, wb, wc, wo, lg, lb)],
        out_specs=[row(D), pl.BlockSpec((tm * SLOT_ROWS, LANES), lambda i: (i, 0))],
        out_shape=[jax.ShapeDtypeStruct((T, D), F32), jax.ShapeDtypeStruct((T * SLOT_ROWS, LANES), U32)],
        compiler_params=pltpu.CompilerParams(dimension_semantics=("parallel",)),
        name="merge",
    )(xa, xb, oa, *obs, *lses, hf, hb, proj, wg, bg, wa, wb, wc, wo, lg, lb)


def _stack_rows(rows, iota):
    out = jnp.zeros(iota.shape, F32)
    for k, r in enumerate(rows):
        out = jnp.where(iota == float(k), r, out)
    return out


def _route_kernel(x_ref, wrt_ref, bias_ref, idx_ref, w_ref, rank_ref, cnt_ref, run_ref, *, tm):
    @pl.when(pl.program_id(0) == 0)
    def _():
        run_ref[...] = jnp.zeros(run_ref.shape, F32)

    x = x_ref[...]
    xh, xm, xl = _split3(x)
    wh, wm, wl = _split3(wrt_ref[...])
    nt = lambda a, b: lax.dot_general(a, b, (((1,), (1,)), ((), ())), preferred_element_type=F32)
    logits = nt(wh, xh) + (nt(wh, xm) + nt(wm, xh)) + (nt(wh, xl) + nt(wl, xh) + nt(wm, xm))
    scores = _sigmoid(logits)
    biased = scores + bias_ref[...]
    ninf = -jnp.inf
    iota_g = lax.broadcasted_iota(I32, (EPG, tm), 0).astype(F32)
    gs = []
    for gi in range(NGRP):
        v = biased[gi * EPG:(gi + 1) * EPG]
        t1 = jnp.max(v, axis=0, keepdims=True)
        i1 = jnp.min(jnp.where(v == t1, iota_g, float(EPG)), axis=0, keepdims=True)
        t2 = jnp.max(jnp.where(iota_g == i1, ninf, v), axis=0, keepdims=True)
        gs.append(t1 + t2)
    iota8 = lax.broadcasted_iota(I32, (NGRP, tm), 0).astype(F32)
    gs = _stack_rows(gs, iota8)
    gsel = jnp.zeros((NGRP, tm), F32)
    for _ in range(TOPG):
        mx = jnp.max(gs, axis=0, keepdims=True)
        ix = jnp.min(jnp.where(gs == mx, iota8, float(NGRP)), axis=0, keepdims=True)
        hit = iota8 == ix
        gsel = jnp.where(hit, 1.0, gsel)
        gs = jnp.where(hit, ninf, gs)
    v = jnp.concatenate([jnp.where(gsel[gi:gi + 1] > 0.0, biased[gi * EPG:(gi + 1) * EPG], ninf)
                         for gi in range(NGRP)], axis=0)
    iota_e = lax.broadcasted_iota(I32, (NE, tm), 0).astype(F32)
    sel = jnp.zeros((NE, tm), F32)
    idxs, svals = [], []
    for _ in range(TOPK):
        mx = jnp.max(v, axis=0, keepdims=True)
        ix = jnp.min(jnp.where(v == mx, iota_e, float(NE)), axis=0, keepdims=True)
        hit = iota_e == ix
        idxs.append(ix)
        svals.append(jnp.sum(jnp.where(hit, scores, 0.0), axis=0, keepdims=True))
        sel = jnp.where(hit, 1.0, sel)
        v = jnp.where(hit, ninf, v)
    top_s = _stack_rows(svals, iota8)
    idx_ref[...] = _stack_rows(idxs, iota8).astype(I32)
    w_ref[...] = top_s / jnp.sum(top_s, axis=0, keepdims=True) * ROUTED_SCALE
    ti = lax.broadcasted_iota(I32, (tm, tm), 0)
    si = lax.broadcasted_iota(I32, (tm, tm), 1)
    before = jnp.where(ti < si, 1.0, 0.0).astype(BF16)
    selb = sel.astype(BF16)
    cnt = run_ref[...] + jnp.dot(selb, before, preferred_element_type=F32)
    rank_ref[...] = _stack_rows(
        [jnp.sum(jnp.where(iota_e == ix, cnt, 0.0), axis=0, keepdims=True) for ix in idxs], iota8).astype(I32)
    run = run_ref[...] + jnp.dot(selb, jnp.ones((tm, tm), BF16), preferred_element_type=F32)
    run_ref[...] = run
    cnt_ref[...] = run.astype(I32)


def _route(x1, wrt, bias_col, tm):
    T = x1.shape[0]
    slot = lambda dt: jax.ShapeDtypeStruct((TOPK, T), dt)
    return pl.pallas_call(
        functools.partial(_route_kernel, tm=tm),
        grid=(T // tm,),
        in_specs=[pl.BlockSpec((tm, D), lambda i: (i, 0)),
                  pl.BlockSpec((NE, D), lambda i: (0, 0)),
                  pl.BlockSpec((NE, 1), lambda i: (0, 0))],
        out_specs=[pl.BlockSpec((TOPK, tm), lambda i: (0, i))] * 3 + [pl.BlockSpec((NE, tm), lambda i: (0, 0))],
        out_shape=[slot(I32), slot(F32), slot(I32), jax.ShapeDtypeStruct((NE, tm), I32)],
        scratch_shapes=[pltpu.VMEM((NE, tm), F32)],
        compiler_params=pltpu.CompilerParams(dimension_semantics=("arbitrary",)),
        name="route",
    )(x1, wrt, bias_col)


def _dest_kernel(idx_ref, rank_ref, start_ref, o_ref, *, tm):
    iota_e = lax.broadcasted_iota(I32, (NE, tm), 0)
    starts = start_ref[...]
    rows = []
    for k in range(TOPK):
        base = jnp.sum(jnp.where(iota_e == idx_ref[k:k + 1, :], starts, 0.0), axis=0, keepdims=True)
        rows.append(base.astype(I32) + rank_ref[k:k + 1, :])
    o_ref[0] = jnp.concatenate(rows, axis=1)


def _dest_tiles(top_idx, rank, pad_start, tm):
    T = top_idx.shape[1]
    return pl.pallas_call(
        functools.partial(_dest_kernel, tm=tm),
        grid=(T // tm,),
        in_specs=[pl.BlockSpec((TOPK, tm), lambda i: (0, i)), pl.BlockSpec((TOPK, tm), lambda i: (0, i)),
                  pl.BlockSpec((NE, 1), lambda i: (0, 0))],
        out_specs=pl.BlockSpec((1, 1, TOPK * tm), lambda i: (i, 0, 0)),
        out_shape=jax.ShapeDtypeStruct((T // tm, 1, TOPK * tm), I32),
        compiler_params=pltpu.CompilerParams(dimension_semantics=("parallel",)),
        name="dest",
    )(top_idx, rank, pad_start.astype(F32).reshape(NE, 1))


def _tile_rows(ref, slot, n=1):
    return ref.at[pl.ds(pl.multiple_of(slot * SLOT_ROWS, SLOT_ROWS), n * SLOT_ROWS)]


def _dispatch_kernel(cnt_ref, start_ref, dest_hbm, x_ref, xs_hbm, dest_a, dest_b, zero_ref, sem_idx, sem, sem_z, *,
                     tm, bm, n_tiles):
    i = pl.program_id(0)
    pad_sizes = [1 << k for k in range(bm.bit_length() - 1)]
    half = bm // 2
    tables = (dest_a, dest_b)

    def table_copy(tile, s):
        return pltpu.make_async_copy(dest_hbm.at[tile, 0], tables[s], sem_idx.at[s])

    used = start_ref[NE - 1] + cnt_ref[NE - 1] + ((-cnt_ref[NE - 1]) & (bm - 1))
    n_tail = (xs_hbm.shape[0] // SLOT_ROWS - used) // half
    tail = lambda c: pltpu.make_async_copy(zero_ref, _tile_rows(xs_hbm, used + c * half, half), sem_z)

    def pad_copies(e, fn):
        n = (-cnt_ref[e]) & (bm - 1)
        at = start_ref[e] + cnt_ref[e]
        for sz in pad_sizes:
            @pl.when((n & sz) != 0)
            def _(at=at, sz=sz):
                fn(pltpu.make_async_copy(_tile_rows(zero_ref, 0, sz), _tile_rows(xs_hbm, at, sz), sem_z))
            at = at + (n & sz)

    @pl.when(i == 0)
    def _():
        table_copy(0, 0).start()
        zero_ref[...] = jnp.zeros(zero_ref.shape, U32)

        def start(e, c):
            pad_copies(e, lambda cp: cp.start())
            return c

        def tail_start(c, z):
            tail(c).start()
            return z

        lax.fori_loop(0, NE, start, 0)
        lax.fori_loop(0, n_tail, tail_start, 0)

    def step(cur, nxt):
        table_copy(i, cur).wait()
        table_copy(jnp.minimum(i + 1, n_tiles - 1), nxt).start()

        def body(r, c):
            for k in range(TOPK):
                pltpu.make_async_copy(_tile_rows(x_ref, r), _tile_rows(xs_hbm, tables[cur][k * tm + r]), sem).start(
                    priority=k % 2)
            return c

        lax.fori_loop(0, tm, body, 0)
        for k in range(TOPK):
            pltpu.make_async_copy(x_ref, _tile_rows(xs_hbm, 0, tm), sem).wait()

        @pl.when(i == n_tiles - 1)
        def _():
            table_copy(i, nxt).wait()

    for parity in range(2):
        pl.when(i % 2 == parity)(functools.partial(step, parity, 1 - parity))

    @pl.when(i == n_tiles - 1)
    def _():
        def wait(e, c):
            pad_copies(e, lambda cp: cp.wait())
            return c

        def tail_wait(c, z):
            tail(c).wait()
            return z

        lax.fori_loop(0, NE, wait, 0)
        lax.fori_loop(0, n_tail, tail_wait, 0)


def _dispatch(counts, pad_start, dest_tiles, x1t, P, tm, bm):
    T = x1t.shape[0] // SLOT_ROWS
    return pl.pallas_call(
        functools.partial(_dispatch_kernel, tm=tm, bm=bm, n_tiles=T // tm),
        grid_spec=pltpu.PrefetchScalarGridSpec(
            num_scalar_prefetch=2, grid=(T // tm,),
            in_specs=[pl.BlockSpec(memory_space=pl.ANY),
                      pl.BlockSpec((tm * SLOT_ROWS, LANES), lambda i, c, s: (i, 0))],
            out_specs=pl.BlockSpec(memory_space=pl.ANY),
            scratch_shapes=[pltpu.SMEM((TOPK * tm,), I32), pltpu.SMEM((TOPK * tm,), I32),
                            pltpu.VMEM((bm // 2 * SLOT_ROWS, LANES), U32),
                            pltpu.SemaphoreType.DMA((2,)), pltpu.SemaphoreType.DMA, pltpu.SemaphoreType.DMA]),
        out_shape=jax.ShapeDtypeStruct((P * SLOT_ROWS, LANES), U32),
        compiler_params=pltpu.CompilerParams(dimension_semantics=("arbitrary",)),
        name="dispatch",
    )(counts, pad_start, dest_tiles, x1t)


def _expert_kernel(blk_e_ref, n_used_ref, x_ref, wgu_ref, wdn_ref, y_ref, wgu_b, wdn_b):
    b = pl.program_id(0)
    live = b < n_used_ref[0]

    @pl.when(live & ((b == 0) | (blk_e_ref[b] != blk_e_ref[jnp.maximum(b - 1, 0)])))
    def _():
        wgu_b[...] = wgu_ref[0, 0].astype(BF16)
        wdn_b[...] = wdn_ref[0, 0].astype(BF16)

    @pl.when(live)
    def _():
        gu = _mm(_unpack_tokens(x_ref), wgu_b[...])
        gate, up = gu[:, :DE], gu[:, DE:]
        _pack_tokens(y_ref, _mm(gate * _sigmoid(gate) * up, wdn_b[...]))

    @pl.when(b >= n_used_ref[0])
    def _():
        y_ref[...] = jnp.zeros(y_ref.shape, U32)


def _experts(blk_e, n_used, xs, wgu, wdn, layer, bm):
    n_blocks = xs.shape[0] // (bm * SLOT_ROWS)
    live = lambda b, n: jnp.minimum(b, n[0] - 1)
    return pl.pallas_call(
        _expert_kernel,
        grid_spec=pltpu.PrefetchScalarGridSpec(
            num_scalar_prefetch=2, grid=(n_blocks,),
            in_specs=[pl.BlockSpec((bm * SLOT_ROWS, LANES), lambda b, e, n: (live(b, n), 0)),
                      pl.BlockSpec((1, 1, D, 2 * DE), lambda b, e, n: (layer, e[live(b, n)], 0, 0)),
                      pl.BlockSpec((1, 1, DE, D), lambda b, e, n: (layer, e[live(b, n)], 0, 0))],
            out_specs=pl.BlockSpec((bm * SLOT_ROWS, LANES), lambda b, e, n: (b, 0)),
            scratch_shapes=[pltpu.VMEM((D, 2 * DE), BF16), pltpu.VMEM((DE, D), BF16)]),
        out_shape=jax.ShapeDtypeStruct(xs.shape, U32),
        compiler_params=pltpu.CompilerParams(dimension_semantics=("arbitrary",)),
        name="experts",
    )(blk_e, n_used, xs, wgu, wdn)


def _combine_kernel(dest_hbm, y_hbm, x_ref, w_ref, wsg_ref, wsd_ref, lg_ref, lb_ref, o_ref, dest_a, dest_b, rows_ref,
                    routed_ref, sem_idx, sem, *, tm, alpha, tile0, n_tiles, chunk):
    g = pl.program_id(0)
    t_next = jnp.minimum(g + 1, n_tiles - 1) + tile0

    dest_smem = (dest_a, dest_b)

    def slots_copy(tile, s):
        return pltpu.make_async_copy(dest_hbm.at[tile, 0], dest_smem[s], sem_idx.at[s])

    def issue(s, r):
        for k in range(TOPK):
            pltpu.make_async_copy(_tile_rows(y_hbm, dest_smem[s][k * tm + r]), _tile_rows(rows_ref.at[s, k], r),
                                  sem.at[s]).start(priority=k % 2)

    def wait_rows(s):
        for k in range(TOPK):
            pltpu.make_async_copy(_tile_rows(y_hbm, 0, tm), rows_ref.at[s, k], sem.at[s]).wait()

    @pl.when(g == 0)
    def _():
        cp = slots_copy(tile0, 0)
        cp.start()
        cp.wait()

        def first(r, c):
            issue(0, r)
            return c

        lax.fori_loop(0, tm, first, 0)

    def step(cur, nxt):
        slots_next = slots_copy(t_next, nxt)
        slots_next.start()
        x = x_ref[...]
        gu = _mm(x, wsg_ref[...])
        gate, up = gu[:, :DE], gu[:, DE:]
        shared = _mm(gate * _sigmoid(gate) * up, wsd_ref[...])
        slots_next.wait()
        wait_rows(cur)

        def chunk_body(c, carry):
            r0 = pl.multiple_of(c * chunk, chunk)
            acc = None
            for k in range(TOPK):
                yk = _unpack_tokens(rows_ref.at[cur, k, pl.ds(r0 * SLOT_ROWS, chunk * SLOT_ROWS)])
                term = yk * w_ref[pl.ds(r0, chunk), k:k + 1]
                acc = term if acc is None else acc + term
            routed_ref[pl.ds(r0, chunk), :] = acc
            for rr in range(chunk):
                issue(nxt, r0 + rr)
            return carry

        lax.fori_loop(0, tm // chunk, chunk_body, 0)
        o_ref[...] = _layer_norm(alpha * x + (routed_ref[...] + shared), lg_ref[...], lb_ref[...])

        @pl.when(g == n_tiles - 1)
        def _():
            wait_rows(nxt)

    for parity in range(2):
        pl.when(g % 2 == parity)(functools.partial(step, parity, 1 - parity))


def _combine(dest_tiles, ys, x1, w_tok, wsg, wsd, lg, lb, alpha, tm, tile0=0, n_tiles=None):
    n_tiles = x1.shape[0] // tm if n_tiles is None else n_tiles
    full = lambda a: pl.BlockSpec(a.shape, lambda i: (0,) * a.ndim)
    return pl.pallas_call(
        functools.partial(_combine_kernel, tm=tm, alpha=alpha, tile0=tile0, n_tiles=n_tiles, chunk=COMBINE_CHUNK),
        grid=(n_tiles,),
        in_specs=[pl.BlockSpec(memory_space=pl.ANY), pl.BlockSpec(memory_space=pl.ANY),
                  pl.BlockSpec((tm, D), lambda i: (i + tile0, 0)), pl.BlockSpec((tm, TOPK), lambda i: (i + tile0, 0))]
        + [full(a) for a in (wsg, wsd, lg, lb)],
        out_specs=pl.BlockSpec((tm, D), lambda i: (i, 0)),
        out_shape=jax.ShapeDtypeStruct((n_tiles * tm, D), F32),
        scratch_shapes=[pltpu.SMEM((TOPK * tm,), I32), pltpu.SMEM((TOPK * tm,), I32),
                        pltpu.VMEM((2, TOPK, tm * SLOT_ROWS, LANES), U32),
                        pltpu.VMEM((tm, D), F32), pltpu.SemaphoreType.DMA((2,)), pltpu.SemaphoreType.DMA((2,))],
        compiler_params=pltpu.CompilerParams(dimension_semantics=("arbitrary",)),
        name="combine",
    )(dest_tiles, ys, x1, w_tok, wsg, wsd, lg, lb)


def _regroup_w_in(w, b):
    offs = np.cumsum([0, 512, 128, 128, 384, 192, 192, 512, 512, 512, 512, 16, 3 * D])
    qa, ka, va, qb, kb, vb, qc, kc, vc, oc, gc, gates = (int(o) for o in offs[:-1])
    pieces = [(qa, ka), (ka, va), (va, qb)]
    for gi in range(3):
        pieces += [(qb + gi * 128, qb + (gi + 1) * 128), (kb + gi * HD, kb + (gi + 1) * HD),
                   (vb + gi * HD, vb + (gi + 1) * HD)]
    pieces += [(qc, kc), (kc, vc), (vc, oc), (oc, gc), (gc, gates), None]
    gc_pad = (N_UNITS - U_GC) * LANES - (gates - gc)

    def build(a):
        parts = [jnp.zeros(a.shape[:-1] + (gc_pad,), a.dtype) if p is None else a[..., p[0]:p[1]] for p in pieces]
        return jnp.concatenate(parts, axis=-1)

    return (build(w).astype(BF16), build(b).reshape(1, NPAD), w[:, gates:].astype(BF16), b[gates:].reshape(1, 3 * D))


def _trunk(xa, xb, seq_lens, depth, params, *, bm, tm_proj, tn_proj, tm_merge, tm_route, tm_moe, split=None):
    (w_in, b_in, sink, conv_w, w_br_a, w_br_b, w_br_c, w_out, ln1_g, ln1_b, w_router, router_bias, w_exp_gu,
     w_exp_down, w_sh_gu, w_sh_down, ln2_g, ln2_b) = params
    rows_a = xa.shape[0]
    T = rows_a + xb.shape[0]
    alpha = float((2 * depth) ** 0.25)
    slopes_a = tuple(float(2.0 ** (-8.0 * i / 8)) for i in range(1, 9))
    slopes_b = tuple(float(2.0 ** (-8.0 * i / 6)) for i in range(1, 7))
    flags1 = jnp.asarray(_edge_flags(seq_lens, QB))
    flags_b = [jnp.asarray(_edge_flags(seq_lens, QB * d)) for d in B_DILS]
    A = T * TOPK
    n_blocks = A // bm + NE
    P = n_blocks * bm
    row2 = lambda a: a.reshape(1, -1)
    for l in range(depth):
        w_all, b_all, w_gates, b_gates = _regroup_w_in(w_in[l], b_in[l])
        proj = _proj(xa, xb, rows_a, T, w_all, b_all, tm_proj, tn_proj)
        oa, ob0, lse0 = _attn_ab0(proj, flags1, sink[l].reshape(1, A_HK * A_G), slopes_a, slopes_b[:B_G])
        obs, lses = [ob0], [lse0]
        for gi in (1, 2):
            o, s = _attn_b(proj, flags_b[gi], gi, slopes_b[gi * B_G:(gi + 1) * B_G])
            obs.append(o)
            lses.append(s)
        gct = proj[:, U_GC * LANES:U_GC * LANES + 16].T
        hf, hb = _mlstm(proj, gct, conv_w[l], flags1)
        x1, x1t = _merge(xa, xb, rows_a, proj, oa, obs, lses, hf, hb, w_gates, b_gates, w_br_a[l].astype(BF16),
                         w_br_b[l].astype(BF16), w_br_c[l].astype(BF16), w_out[l].astype(BF16), row2(ln1_g[l]),
                         row2(ln1_b[l]), alpha, tm_merge)
        top_idx, top_w, rank, counts = _route(x1, w_router[l].T, router_bias[l].reshape(NE, 1), tm_route)
        counts = counts[:, 0]
        padded = (counts + bm - 1) // bm * bm
        pad_end = jnp.cumsum(padded)
        pad_start = pad_end - padded
        dest = _dest_tiles(top_idx, rank, pad_start, tm_moe)
        blk_e = jnp.minimum(jnp.searchsorted(pad_end, jnp.arange(n_blocks, dtype=I32) * bm, side='right'),
                            NE - 1).astype(I32)
        n_used = (pad_end[-1:] // bm).astype(I32)
        xs = _dispatch(counts, pad_start, dest, x1t, P, tm_moe, bm)
        ys = _experts(blk_e, n_used, xs, w_exp_gu, w_exp_down, l, bm)
        comb = functools.partial(_combine, dest, ys, x1, top_w.T, w_sh_gu[l].astype(BF16), w_sh_down[l].astype(BF16),
                                 row2(ln2_g[l]), row2(ln2_b[l]), alpha, tm_moe)
        if l + 1 < depth or split is None:
            xa = xb = comb()
            rows_a = T
        else:
            return comb(0, split // tm_moe), comb(split // tm_moe, (T - split) // tm_moe)
    return xa


def kernel(x_prompt, x_sample, w_in, b_in, sink, conv_w, w_br_a, w_br_b, w_br_c, w_out, ln1_g, ln1_b, w_router,
           router_bias, w_exp_gu, w_exp_down, w_sh_gu, w_sh_down, ln2_g, ln2_b):
    depth = w_in.shape[0]
    n1, s1, _ = x_prompt.shape
    n2, s2, _ = x_sample.shape
    span = QB * max(B_DILS)
    assert s1 % span == 0 and s2 % span == 0, "sequence lengths must be multiples of the widest dilated block"
    params = (w_in, b_in, sink, conv_w, w_br_a, w_br_b, w_br_c, w_out, ln1_g, ln1_b, w_router, router_bias,
              w_exp_gu, w_exp_down, w_sh_gu, w_sh_down, ln2_g, ln2_b)
    y1, y2 = _trunk(x_prompt.reshape(n1 * s1, D), x_sample.reshape(n2 * s2, D), [s1] * n1 + [s2] * n2, depth, params,
                    bm=1024, tm_proj=1024, tn_proj=1024, tm_merge=256, tm_route=256, tm_moe=512, split=n1 * s1)
    return y1.reshape(n1, s1, D), y2.reshape(n2, s2, D)
```

```python
import functools

import numpy as np
import jax
import jax.numpy as jnp
from jax import lax
from jax.experimental import pallas as pl
from jax.experimental.pallas import tpu as pltpu

F32, BF16, I32, U32 = jnp.float32, jnp.bfloat16, jnp.int32, jnp.uint32

LANES = 128
D = 1024
HD = 64
A_HK, A_G, A_BAND = 2, 4, 128
B_G, B_BAND = 2, 64
B_DILS = (1, 4, 16)
C_H, C_HD, CHUNK = 4, 128, 128
NE, TOPK, NGRP, TOPG, DE = 256, 8, 8, 4, 256
EPG = NE // NGRP
ROUTED_SCALE = 2.5
LN_EPS = 1e-5
NEG = -1e30
QB = 128
SLOT_ROWS = D // (2 * LANES)
COMBINE_CHUNK = 32

U_QA, U_KA, U_VA = 0, 4, 5
U_QB = (6, 8, 10)
U_KVB = (7, 9, 11)
U_QC, U_KC, U_VC, U_OC, U_GC = 12, 16, 20, 24, 28
N_UNITS = 32
NPAD = N_UNITS * LANES


def _edge_flags(seq_lens, rows):
    first, last = [], []
    for s in seq_lens:
        nb = s // rows
        first += [1] + [0] * (nb - 1)
        last += [0] * (nb - 1) + [1]
    return np.array([first, last], np.int32)


def _mm(a, b):
    return jnp.dot(a.astype(BF16), b.astype(BF16), preferred_element_type=F32)


def _mm_nt(a, b):
    return lax.dot_general(a.astype(BF16), b.astype(BF16), (((1,), (1,)), ((), ())), preferred_element_type=F32)


def _mm_tn(a, b):
    return lax.dot_general(a.astype(BF16), b.astype(BF16), (((0,), (0,)), ((), ())), preferred_element_type=F32)


def _split3(x):
    hi = x.astype(BF16)
    r1 = x - hi.astype(F32)
    mid = r1.astype(BF16)
    lo = (r1 - mid.astype(F32)).astype(BF16)
    return hi, mid, lo


def _sigmoid(x):
    return 1.0 / (1.0 + jnp.exp(-x))


def _log_sigmoid(x):
    return jnp.minimum(x, 0.0) - jnp.log(1.0 + jnp.exp(-jnp.abs(x)))


def _pack_tokens(ref, x):
    n = x.shape[0]
    u = lax.bitcast_convert_type(x, U32)
    u = u + (jnp.uint32(0x7FFF) + ((u >> 16) & jnp.uint32(1)))
    for j in range(SLOT_ROWS):
        lo = u[:, j * LANES:(j + 1) * LANES] >> 16
        hi = u[:, D // 2 + j * LANES:D // 2 + (j + 1) * LANES] & jnp.uint32(0xFFFF0000)
        ref[pl.ds(j, n, stride=SLOT_ROWS), :] = lo | hi


def _unpack_tokens(ref):
    n = ref.shape[0] // SLOT_ROWS
    words = [ref[pl.ds(j, n, stride=SLOT_ROWS), :] for j in range(SLOT_ROWS)]
    lo = [lax.bitcast_convert_type(w << 16, F32) for w in words]
    hi = [lax.bitcast_convert_type(w & jnp.uint32(0xFFFF0000), F32) for w in words]
    return jnp.concatenate(lo + hi, axis=1)


def _layer_norm(x, g, b):
    mu = jnp.mean(x, axis=-1, keepdims=True)
    xc = x - mu
    var = jnp.mean(xc * xc, axis=-1, keepdims=True)
    return xc * lax.rsqrt(var + LN_EPS) * g + b


def _two_part_specs(tm, n_a):
    return [pl.BlockSpec((tm, D), lambda i, *_: (jnp.minimum(i, n_a - 1), 0)),
            pl.BlockSpec((tm, D), lambda i, *_: (jnp.maximum(i - n_a, 0), 0))]


def _proj_kernel(xa_ref, xb_ref, w_ref, b_ref, o_ref, x16_ref, *, n_a):
    @pl.when(pl.program_id(1) == 0)
    def _():
        x16_ref[...] = jnp.where(pl.program_id(0) < n_a, xa_ref[...], xb_ref[...]).astype(BF16)

    o_ref[...] = jnp.dot(x16_ref[...], w_ref[...], preferred_element_type=F32) + b_ref[...]


def _proj(xa, xb, rows_a, T, w, b, tm, tn):
    N = w.shape[1]
    n_a = rows_a // tm
    return pl.pallas_call(
        functools.partial(_proj_kernel, n_a=n_a),
        grid=(T // tm, N // tn),
        in_specs=_two_part_specs(tm, n_a)
        + [pl.BlockSpec((D, tn), lambda i, j: (0, j)),
           pl.BlockSpec((1, tn), lambda i, j: (0, j))],
        out_specs=pl.BlockSpec((tm, tn), lambda i, j: (i, j)),
        out_shape=jax.ShapeDtypeStruct((T, N), F32),
        scratch_shapes=[pltpu.VMEM((tm, D), BF16)],
        compiler_params=pltpu.CompilerParams(dimension_semantics=("parallel", "arbitrary")),
        name="proj",
    )(xa, xb, w, b)


def _attn_bias(slopes, g, band, dist):
    rel = np.abs(np.arange(3 * QB)[None, :] - QB - np.arange(QB)[:, None])
    dist_f = (rel * dist).astype(np.float32)
    per_head = [np.where(rel <= band, -(np.float32(sl) * dist_f), np.float32(NEG)) for sl in slopes]
    return np.stack([np.concatenate(per_head[h * g:(h + 1) * g], axis=0) for h in range(len(slopes) // g)])


def _attn_tile(q, kwin, vwin, bias_ref, lo, hi, sinks, *, hk, g, packed):
    qb, W = q.shape[0], kwin.shape[0]
    rows = g * qb
    one_col = jnp.where(lax.broadcasted_iota(I32, (rows, HD), 1) == 0, 1.0, 0.0)
    key = lax.broadcasted_iota(I32, (W, HD), 0)
    edge = jnp.where((key >= lo) & (key < hi), 0.0, NEG)
    row_g = jnp.right_shift(lax.broadcasted_iota(I32, (rows, 1), 0), qb.bit_length() - 1)
    outs, lses = [], []
    for h in range(hk):
        qs = jnp.concatenate([q[:, (h * g + gg) * HD:(h * g + gg + 1) * HD] for gg in range(g)], axis=0)
        qs = jnp.concatenate([qs * (HD ** -0.5), one_col], axis=1)
        ko = h * HD
        vo = HD if packed else h * HD
        kh = jnp.concatenate([kwin[:, ko:ko + HD], edge], axis=1)
        vh = vwin[:, vo:vo + HD]
        logits = _mm_nt(qs, kh) + bias_ref[h]
        m = jnp.max(logits, axis=-1, keepdims=True)
        if sinks is not None:
            sink_col = jnp.zeros((rows, 1), F32)
            for gg in range(g):
                sink_col = jnp.where(row_g == gg, sinks[h * g + gg], sink_col)
            m = jnp.maximum(m, sink_col)
        p = jnp.exp(logits - m)
        den = jnp.sum(p, axis=-1, keepdims=True)
        if sinks is not None:
            den = den + jnp.exp(sink_col - m)
        o = _mm(p, vh) / den
        for gg in range(g):
            outs.append(o[gg * qb:(gg + 1) * qb])
        if packed:
            l = m + jnp.log(den)
            for gg in range(g):
                lses.append(jnp.broadcast_to(l[gg * qb:(gg + 1) * qb], (qb, HD)))
    return outs, lses


def _edge_cols(flags_ref, b):
    return jnp.where(flags_ref[0, b] == 0, 0, QB), jnp.where(flags_ref[1, b] == 0, 3 * QB, 2 * QB)


def _attn_ab0_kernel(flags_ref, sink_ref, qa_ref, kp_ref, ko_ref, kn_ref, vp_ref, vo_ref, vn_ref, qb_ref, kvp_ref,
                     kvo_ref, kvn_ref, bias_a_ref, bias_b_ref, oa_ref, ob_ref, lse_ref):
    lo, hi = _edge_cols(flags_ref, pl.program_id(0))
    sinks = [sink_ref[0, i] for i in range(A_HK * A_G)]
    kwin = jnp.concatenate([kp_ref[...], ko_ref[...], kn_ref[...]], axis=0)
    vwin = jnp.concatenate([vp_ref[...], vo_ref[...], vn_ref[...]], axis=0)
    outs, _ = _attn_tile(qa_ref[...], kwin, vwin, bias_a_ref, lo, hi, sinks, hk=A_HK, g=A_G, packed=False)
    oa_ref[...] = jnp.concatenate(outs, axis=1).astype(oa_ref.dtype)
    kv = jnp.concatenate([kvp_ref[...], kvo_ref[...], kvn_ref[...]], axis=0)
    outs, lses = _attn_tile(qb_ref[...], kv, kv, bias_b_ref, lo, hi, None, hk=1, g=B_G, packed=True)
    ob_ref[...] = jnp.concatenate(outs, axis=1)
    lse_ref[...] = jnp.concatenate(lses, axis=1)


def _attn_ab0(proj, flags, sink, slopes_a, slopes_b0):
    T = proj.shape[0]
    nb = T // QB
    prev = lambda b, f: jnp.maximum(b - 1, 0)
    nxt = lambda b, f: jnp.minimum(b + 1, nb - 1)
    kspecs = lambda u: [pl.BlockSpec((QB, LANES), lambda b, f: (prev(b, f), u)),
                        pl.BlockSpec((QB, LANES), lambda b, f: (b, u)),
                        pl.BlockSpec((QB, LANES), lambda b, f: (nxt(b, f), u))]
    bias_a = jnp.asarray(_attn_bias(slopes_a, A_G, A_BAND, 1))
    bias_b = jnp.asarray(_attn_bias(slopes_b0, B_G, B_BAND, B_DILS[0]))
    full = lambda a: pl.BlockSpec(a.shape, lambda b, f: (0,) * a.ndim)
    wa = A_HK * A_G * HD
    return pl.pallas_call(
        _attn_ab0_kernel,
        grid_spec=pltpu.PrefetchScalarGridSpec(
            num_scalar_prefetch=1, grid=(nb,),
            in_specs=[pl.BlockSpec(memory_space=pltpu.SMEM), pl.BlockSpec((QB, wa), lambda b, f: (b, U_QA // 4))]
            + kspecs(U_KA) + kspecs(U_VA) + [pl.BlockSpec((QB, LANES), lambda b, f: (b, U_QB[0]))] + kspecs(U_KVB[0])
            + [full(bias_a), full(bias_b)],
            out_specs=[pl.BlockSpec((QB, wa), lambda b, f: (b, 0)), pl.BlockSpec((QB, LANES), lambda b, f: (b, 0)),
                       pl.BlockSpec((QB, LANES), lambda b, f: (b, 0))]),
        out_shape=[jax.ShapeDtypeStruct((T, wa), BF16), jax.ShapeDtypeStruct((T, LANES), F32),
                   jax.ShapeDtypeStruct((T, LANES), F32)],
        compiler_params=pltpu.CompilerParams(dimension_semantics=("parallel",)),
        name="attn_ab0",
    )(flags, sink, *([proj] * 11), bias_a, bias_b)


def _attn_b_kernel(flags_ref, q_ref, kp_ref, ko_ref, kn_ref, bias_ref, o_ref, lse_ref, *, dil):
    lo, hi = _edge_cols(flags_ref, pl.program_id(0))

    def body(r, c):
        sub = lambda ref: ref[pl.ds(r, QB, stride=dil), :]
        kv = jnp.concatenate([sub(kp_ref), sub(ko_ref), sub(kn_ref)], axis=0)
        outs, lses = _attn_tile(sub(q_ref), kv, kv, bias_ref, lo, hi, None, hk=1, g=B_G, packed=True)
        o_ref[pl.ds(r, QB, stride=dil), :] = jnp.concatenate(outs, axis=1)
        lse_ref[pl.ds(r, QB, stride=dil), :] = jnp.concatenate(lses, axis=1)
        return c

    lax.fori_loop(0, dil, body, 0, unroll=2)


def _attn_b(proj, flags, gi, slopes):
    dil = B_DILS[gi]
    T = proj.shape[0]
    rows = QB * dil
    nb = T // rows
    uq, ukv = U_QB[gi], U_KVB[gi]
    bias = jnp.asarray(_attn_bias(slopes, B_G, B_BAND, dil))
    return pl.pallas_call(
        functools.partial(_attn_b_kernel, dil=dil),
        grid_spec=pltpu.PrefetchScalarGridSpec(
            num_scalar_prefetch=1, grid=(nb,),
            in_specs=[pl.BlockSpec((rows, LANES), lambda b, f: (b, uq)),
                      pl.BlockSpec((rows, LANES), lambda b, f: (jnp.maximum(b - 1, 0), ukv)),
                      pl.BlockSpec((rows, LANES), lambda b, f: (b, ukv)),
                      pl.BlockSpec((rows, LANES), lambda b, f: (jnp.minimum(b + 1, nb - 1), ukv)),
                      pl.BlockSpec(bias.shape, lambda b, f: (0, 0, 0))],
            out_specs=[pl.BlockSpec((rows, LANES), lambda b, f: (b, 0)),
                       pl.BlockSpec((rows, LANES), lambda b, f: (b, 0))]),
        out_shape=[jax.ShapeDtypeStruct((T, LANES), F32), jax.ShapeDtypeStruct((T, LANES), F32)],
        compiler_params=pltpu.CompilerParams(dimension_semantics=("parallel",)),
        name=f"attn_b{gi}",
    )(flags, proj, proj, proj, proj, bias)


def _conv_silu(x, xp8, xn8, w, keep_prev, keep_next):
    n = x.shape[0]
    rowi = lax.broadcasted_iota(I32, (n, 1), 0)
    x_prev = jnp.where(rowi == 0, xp8[7:8, :] * keep_prev, pltpu.roll(x, 1, 0))
    x_next = jnp.where(rowi == n - 1, xn8[0:1, :] * keep_next, pltpu.roll(x, n - 1, 0))
    y = w[0:1, :] * x_prev + w[1:2, :] * x + w[2:3, :] * x_next
    return y * _sigmoid(y)


def _mlstm_dir(rev, first, last, q_ref, qp_ref, qn_ref, k_ref, kp_ref, kn_ref, v_ref, g_ref, gt_ref, cw_ref,
               c_ref, n_ref, m_ref, h_ref):
    sd = 1 if rev else 0
    keep_prev = (1 - first).astype(F32)
    keep_next = (1 - last).astype(F32)

    @pl.when((last if rev else first) == 1)
    def _():
        c_ref[sd] = jnp.zeros(c_ref.shape[1:], F32)
        n_ref[sd] = jnp.zeros(n_ref.shape[1:], F32)
        m_ref[sd] = jnp.zeros(m_ref.shape[1:], F32)

    W = C_H * C_HD
    q = _conv_silu(q_ref[...], qp_ref[...], qn_ref[...], cw_ref[:, :W], keep_prev, keep_next)
    k = _conv_silu(k_ref[...], kp_ref[...], kn_ref[...], cw_ref[:, W:], keep_prev, keep_next) * (C_HD ** -0.5)
    v_all, q_all, k_all = v_ref[...], q, k
    for sub in ((1, 0) if rev else (0, 1)):
        rs = slice(sub * CHUNK, (sub + 1) * CHUNK)
        _mlstm_chunk(rev, sd, q_all[rs], k_all[rs], v_all[rs], g_ref[rs, :], gt_ref[:, rs], c_ref, n_ref, m_ref,
                     h_ref, rs)


def _mlstm_chunk(rev, sd, q, k, v, G, GT, c_ref, n_ref, m_ref, h_ref, rs):
    ti = lax.broadcasted_iota(I32, (CHUNK, CHUNK), 0)
    si = lax.broadcasted_iota(I32, (CHUNK, CHUNK), 1)
    tri = (si >= ti) if rev else (si <= ti)
    L = jnp.where(tri, 1.0, 0.0).astype(BF16)
    Lt = jnp.where((ti >= si) if rev else (ti <= si), 1.0, 0.0).astype(BF16)
    cum_col = sum(jnp.dot(L, p, preferred_element_type=F32) for p in _split3(_log_sigmoid(G)))
    cum_row = sum(jnp.dot(p, Lt, preferred_element_type=F32) for p in _split3(_log_sigmoid(GT)))
    e = 0 if rev else CHUNK - 1
    hs = []
    for h in range(C_H):
        ci = (2 * C_H if rev else 0) + h
        cf = ci + C_H
        cum_c, cum_r = cum_col[:, cf:cf + 1], cum_row[cf:cf + 1, :]
        li_c, li_r = G[:, ci:ci + 1], GT[ci:ci + 1, :]
        qh, kh, vh = (a[:, h * C_HD:(h + 1) * C_HD] for a in (q, k, v))
        m11 = m_ref[sd, h][:, :1]
        Dm = jnp.where(tri, cum_c - cum_r + li_r, NEG)
        m_inter = cum_c + m11
        mt = jnp.maximum(jnp.max(Dm, axis=-1, keepdims=True), m_inter)
        Wm = jnp.exp(Dm - mt) * _mm_nt(qh, kh)
        inter = jnp.exp(m_inter - mt)
        num = _mm(Wm, vh) + inter * _mm(qh, c_ref[sd, h])
        den = jnp.sum(Wm, axis=-1, keepdims=True) + inter * jnp.sum(qh * n_ref[sd, h], axis=-1, keepdims=True)
        hs.append(num / jnp.maximum(jnp.abs(den), jnp.exp(-mt)))
        total = cum_c[e:e + 1, :]
        gcol = total - cum_c + li_c
        m_new = jnp.maximum(total + m11, jnp.max(gcol, axis=0, keepdims=True))
        kw = kh * jnp.exp(gcol - m_new)
        decay = jnp.exp(total + m11 - m_new)
        c_ref[sd, h] = decay * c_ref[sd, h] + _mm_tn(kw, vh)
        n_ref[sd, h] = decay * n_ref[sd, h] + jnp.sum(kw, axis=0, keepdims=True)
        m_ref[sd, h] = jnp.broadcast_to(m_new, (1, LANES))
    h_ref[rs, :] = jnp.concatenate(hs, axis=1)


def _mlstm_kernel(flags_ref, *refs, nb):
    fwd, bwd, (cw_ref, hf_ref, hb_ref, c_ref, n_ref, m_ref) = refs[:9], refs[9:18], refs[18:]
    j = pl.program_id(0)
    jb = nb - 1 - j
    _mlstm_dir(False, flags_ref[0, j], flags_ref[1, j], *fwd, cw_ref, c_ref, n_ref, m_ref, hf_ref)
    _mlstm_dir(True, flags_ref[0, jb], flags_ref[1, jb], *bwd, cw_ref, c_ref, n_ref, m_ref, hb_ref)


def _mlstm(proj, gct, conv_w, flags):
    T = proj.shape[0]
    BLK = 2 * CHUNK
    nb = T // BLK
    W = C_H * C_HD
    r8 = BLK // 8

    def specs(blk):
        halo_p = lambda j, f: jnp.maximum(blk(j) * r8 - 1, 0)
        halo_n = lambda j, f: jnp.minimum(blk(j) * r8 + r8, nb * r8 - 1)
        out = []
        for u in (U_QC, U_KC):
            out += [pl.BlockSpec((BLK, W), lambda j, f, u=u: (blk(j), u // 4)),
                    pl.BlockSpec((8, W), lambda j, f, u=u: (halo_p(j, f), u // 4)),
                    pl.BlockSpec((8, W), lambda j, f, u=u: (halo_n(j, f), u // 4))]
        out += [pl.BlockSpec((BLK, W), lambda j, f: (blk(j), U_VC // 4)),
                pl.BlockSpec((BLK, LANES), lambda j, f: (blk(j), U_GC)),
                pl.BlockSpec((16, BLK), lambda j, f: (0, blk(j)))]
        return out

    fwd_blk = lambda j: j
    bwd_blk = lambda j: nb - 1 - j
    args = [proj] * 8 + [gct]
    return pl.pallas_call(
        functools.partial(_mlstm_kernel, nb=nb),
        grid_spec=pltpu.PrefetchScalarGridSpec(
            num_scalar_prefetch=1, grid=(nb,),
            in_specs=specs(fwd_blk) + specs(bwd_blk) + [pl.BlockSpec((3, 2 * W), lambda j, f: (0, 0))],
            out_specs=[pl.BlockSpec((BLK, W), lambda j, f: (j, 0)),
                       pl.BlockSpec((BLK, W), lambda j, f: (nb - 1 - j, 0))],
            scratch_shapes=[pltpu.VMEM((2, C_H, C_HD, C_HD), F32),
                            pltpu.VMEM((2, C_H, 1, C_HD), F32),
                            pltpu.VMEM((2, C_H, 1, LANES), F32)]),
        out_shape=[jax.ShapeDtypeStruct((T, W), F32), jax.ShapeDtypeStruct((T, W), F32)],
        compiler_params=pltpu.CompilerParams(dimension_semantics=("arbitrary",)),
        name="mlstm",
    )(flags, *args, *args, conv_w)


def _merge_kernel(xa_ref, xb_ref, oa_ref, ob0_ref, ob1_ref, ob2_ref, l0_ref, l1_ref, l2_ref, hf_ref, hb_ref, oc_ref,
                  wg_ref, bg_ref, wa_ref, wb_ref, wc_ref, wo_ref, lg_ref, lb_ref, o_ref, ot_ref, *, alpha, n_a):
    ya = jnp.dot(oa_ref[...], wa_ref[...], preferred_element_type=F32)
    l0, l1, l2 = l0_ref[...], l1_ref[...], l2_ref[...]
    lm = jnp.maximum(jnp.maximum(l0, l1), l2)
    e0, e1, e2 = jnp.exp(l0 - lm), jnp.exp(l1 - lm), jnp.exp(l2 - lm)
    es = e0 + e1 + e2
    outb = jnp.concatenate([ob0_ref[...] * (e0 / es), ob1_ref[...] * (e1 / es), ob2_ref[...] * (e2 / es)], axis=1)
    yb = _mm(outb, wb_ref[...])
    yc = _mm(_sigmoid(oc_ref[...]) * (hf_ref[...] + hb_ref[...]), wc_ref[...])
    x = jnp.where(pl.program_id(0) < n_a, xa_ref[...], xb_ref[...])
    gates = _sigmoid(_mm(x, wg_ref[...]) + bg_ref[...])
    merged = gates[:, 0:D] * ya + gates[:, D:2 * D] * yb + gates[:, 2 * D:3 * D] * yc
    mix = _mm(merged, wo_ref[...])
    x1 = _layer_norm(alpha * x + mix, lg_ref[...], lb_ref[...])
    o_ref[...] = x1
    _pack_tokens(ot_ref, x1)


def _merge(xa, xb, rows_a, proj, oa, obs, lses, hf, hb, wg, bg, wa, wb, wc, wo, lg, lb, alpha, tm):
    T = proj.shape[0]
    n_a = rows_a // tm
    row = lambda w: pl.BlockSpec((tm, w), lambda i: (i, 0))
    full = lambda a: pl.BlockSpec(a.shape, lambda i: (0,) * a.ndim)
    return pl.pallas_call(
        functools.partial(_merge_kernel, alpha=alpha, n_a=n_a),
        grid=(T // tm,),
        in_specs=_two_part_specs(tm, n_a) + [row(oa.shape[1])] + [row(LANES)] * 6 + [row(hf.shape[1])] * 2
        + [pl.BlockSpec((tm, 512), lambda i: (i, U_OC // 4))]
        + [full(a) for a in (wg, bg, wa, wb, wc, wo, lg, lb)],
        out_specs=[row(D), pl.BlockSpec((tm * SLOT_ROWS, LANES), lambda i: (i, 0))],
        out_shape=[jax.ShapeDtypeStruct((T, D), F32), jax.ShapeDtypeStruct((T * SLOT_ROWS, LANES), U32)],
        compiler_params=pltpu.CompilerParams(dimension_semantics=("parallel",)),
        name="merge",
    )(xa, xb, oa, *obs, *lses, hf, hb, proj, wg, bg, wa, wb, wc, wo, lg, lb)


def _stack_rows(rows, iota):
    out = jnp.zeros(iota.shape, F32)
    for k, r in enumerate(rows):
        out = jnp.where(iota == float(k), r, out)
    return out


def _route_kernel(x_ref, wrt_ref, bias_ref, idx_ref, w_ref, rank_ref, cnt_ref, run_ref, *, tm):
    @pl.when(pl.program_id(0) == 0)
    def _():
        run_ref[...] = jnp.zeros(run_ref.shape, F32)

    x = x_ref[...]
    xh, xm, xl = _split3(x)
    wh, wm, wl = _split3(wrt_ref[...])
    nt = lambda a, b: lax.dot_general(a, b, (((1,), (1,)), ((), ())), preferred_element_type=F32)
    logits = nt(wh, xh) + (nt(wh, xm) + nt(wm, xh)) + (nt(wh, xl) + nt(wl, xh) + nt(wm, xm))
    scores = _sigmoid(logits)
    biased = scores + bias_ref[...]
    ninf = -jnp.inf
    iota_g = lax.broadcasted_iota(I32, (EPG, tm), 0).astype(F32)
    gs = []
    for gi in range(NGRP):
        v = biased[gi * EPG:(gi + 1) * EPG]
        t1 = jnp.max(v, axis=0, keepdims=True)
        i1 = jnp.min(jnp.where(v == t1, iota_g, float(EPG)), axis=0, keepdims=True)
        t2 = jnp.max(jnp.where(iota_g == i1, ninf, v), axis=0, keepdims=True)
        gs.append(t1 + t2)
    iota8 = lax.broadcasted_iota(I32, (NGRP, tm), 0).astype(F32)
    gs = _stack_rows(gs, iota8)
    gsel = jnp.zeros((NGRP, tm), F32)
    for _ in range(TOPG):
        mx = jnp.max(gs, axis=0, keepdims=True)
        ix = jnp.min(jnp.where(gs == mx, iota8, float(NGRP)), axis=0, keepdims=True)
        hit = iota8 == ix
        gsel = jnp.where(hit, 1.0, gsel)
        gs = jnp.where(hit, ninf, gs)
    v = jnp.concatenate([jnp.where(gsel[gi:gi + 1] > 0.0, biased[gi * EPG:(gi + 1) * EPG], ninf)
                         for gi in range(NGRP)], axis=0)
    iota_e = lax.broadcasted_iota(I32, (NE, tm), 0).astype(F32)
    sel = jnp.zeros((NE, tm), F32)
    idxs, svals = [], []
    for _ in range(TOPK):
        mx = jnp.max(v, axis=0, keepdims=True)
        ix = jnp.min(jnp.where(v == mx, iota_e, float(NE)), axis=0, keepdims=True)
        hit = iota_e == ix
        idxs.append(ix)
        svals.append(jnp.sum(jnp.where(hit, scores, 0.0), axis=0, keepdims=True))
        sel = jnp.where(hit, 1.0, sel)
        v = jnp.where(hit, ninf, v)
    top_s = _stack_rows(svals, iota8)
    idx_ref[...] = _stack_rows(idxs, iota8).astype(I32)
    w_ref[...] = top_s / jnp.sum(top_s, axis=0, keepdims=True) * ROUTED_SCALE
    ti = lax.broadcasted_iota(I32, (tm, tm), 0)
    si = lax.broadcasted_iota(I32, (tm, tm), 1)
    before = jnp.where(ti < si, 1.0, 0.0).astype(BF16)
    selb = sel.astype(BF16)
    cnt = run_ref[...] + jnp.dot(selb, before, preferred_element_type=F32)
    rank_ref[...] = _stack_rows(
        [jnp.sum(jnp.where(iota_e == ix, cnt, 0.0), axis=0, keepdims=True) for ix in idxs], iota8).astype(I32)
    run = run_ref[...] + jnp.dot(selb, jnp.ones((tm, tm), BF16), preferred_element_type=F32)
    run_ref[...] = run
    cnt_ref[...] = run.astype(I32)


def _route(x1, wrt, bias_col, tm):
    T = x1.shape[0]
    slot = lambda dt: jax.ShapeDtypeStruct((TOPK, T), dt)
    return pl.pallas_call(
        functools.partial(_route_kernel, tm=tm),
        grid=(T // tm,),
        in_specs=[pl.BlockSpec((tm, D), lambda i: (i, 0)),
                  pl.BlockSpec((NE, D), lambda i: (0, 0)),
                  pl.BlockSpec((NE, 1), lambda i: (0, 0))],
        out_specs=[pl.BlockSpec((TOPK, tm), lambda i: (0, i))] * 3 + [pl.BlockSpec((NE, tm), lambda i: (0, 0))],
        out_shape=[slot(I32), slot(F32), slot(I32), jax.ShapeDtypeStruct((NE, tm), I32)],
        scratch_shapes=[pltpu.VMEM((NE, tm), F32)],
        compiler_params=pltpu.CompilerParams(dimension_semantics=("arbitrary",)),
        name="route",
    )(x1, wrt, bias_col)


def _dest_kernel(idx_ref, rank_ref, start_ref, o_ref, *, tm):
    iota_e = lax.broadcasted_iota(I32, (NE, tm), 0)
    starts = start_ref[...]
    rows = []
    for k in range(TOPK):
        base = jnp.sum(jnp.where(iota_e == idx_ref[k:k + 1, :], starts, 0.0), axis=0, keepdims=True)
        rows.append(base.astype(I32) + rank_ref[k:k + 1, :])
    o_ref[0] = jnp.concatenate(rows, axis=1)


def _dest_tiles(top_idx, rank, pad_start, tm):
    T = top_idx.shape[1]
    return pl.pallas_call(
        functools.partial(_dest_kernel, tm=tm),
        grid=(T // tm,),
        in_specs=[pl.BlockSpec((TOPK, tm), lambda i: (0, i)), pl.BlockSpec((TOPK, tm), lambda i: (0, i)),
                  pl.BlockSpec((NE, 1), lambda i: (0, 0))],
        out_specs=pl.BlockSpec((1, 1, TOPK * tm), lambda i: (i, 0, 0)),
        out_shape=jax.ShapeDtypeStruct((T // tm, 1, TOPK * tm), I32),
        compiler_params=pltpu.CompilerParams(dimension_semantics=("parallel",)),
        name="dest",
    )(top_idx, rank, pad_start.astype(F32).reshape(NE, 1))


def _tile_rows(ref, slot, n=1):
    return ref.at[pl.ds(pl.multiple_of(slot * SLOT_ROWS, SLOT_ROWS), n * SLOT_ROWS)]


def _dispatch_kernel(cnt_ref, start_ref, dest_hbm, x_ref, xs_hbm, dest_a, dest_b, zero_ref, sem_idx, sem, sem_z, *,
                     tm, bm, n_tiles):
    i = pl.program_id(0)
    pad_sizes = [1 << k for k in range(bm.bit_length() - 1)]
    half = bm // 2
    tables = (dest_a, dest_b)

    def table_copy(tile, s):
        return pltpu.make_async_copy(dest_hbm.at[tile, 0], tables[s], sem_idx.at[s])

    used = start_ref[NE - 1] + cnt_ref[NE - 1] + ((-cnt_ref[NE - 1]) & (bm - 1))
    n_tail = (xs_hbm.shape[0] // SLOT_ROWS - used) // half
    tail = lambda c: pltpu.make_async_copy(zero_ref, _tile_rows(xs_hbm, used + c * half, half), sem_z)

    def pad_copies(e, fn):
        n = (-cnt_ref[e]) & (bm - 1)
        at = start_ref[e] + cnt_ref[e]
        for sz in pad_sizes:
            @pl.when((n & sz) != 0)
            def _(at=at, sz=sz):
                fn(pltpu.make_async_copy(_tile_rows(zero_ref, 0, sz), _tile_rows(xs_hbm, at, sz), sem_z))
            at = at + (n & sz)

    @pl.when(i == 0)
    def _():
        table_copy(0, 0).start()
        zero_ref[...] = jnp.zeros(zero_ref.shape, U32)

        def start(e, c):
            pad_copies(e, lambda cp: cp.start())
            return c

        def tail_start(c, z):
            tail(c).start()
            return z

        lax.fori_loop(0, NE, start, 0)
        lax.fori_loop(0, n_tail, tail_start, 0)

    def step(cur, nxt):
        table_copy(i, cur).wait()
        table_copy(jnp.minimum(i + 1, n_tiles - 1), nxt).start()

        def body(r, c):
            for k in range(TOPK):
                pltpu.make_async_copy(_tile_rows(x_ref, r), _tile_rows(xs_hbm, tables[cur][k * tm + r]), sem).start(
                    priority=k % 2)
            return c

        lax.fori_loop(0, tm, body, 0)
        for k in range(TOPK):
            pltpu.make_async_copy(x_ref, _tile_rows(xs_hbm, 0, tm), sem).wait()

        @pl.when(i == n_tiles - 1)
        def _():
            table_copy(i, nxt).wait()

    for parity in range(2):
        pl.when(i % 2 == parity)(functools.partial(step, parity, 1 - parity))

    @pl.when(i == n_tiles - 1)
    def _():
        def wait(e, c):
            pad_copies(e, lambda cp: cp.wait())
            return c

        def tail_wait(c, z):
            tail(c).wait()
            return z

        lax.fori_loop(0, NE, wait, 0)
        lax.fori_loop(0, n_tail, tail_wait, 0)


def _dispatch(counts, pad_start, dest_tiles, x1t, P, tm, bm):
    T = x1t.shape[0] // SLOT_ROWS
    return pl.pallas_call(
        functools.partial(_dispatch_kernel, tm=tm, bm=bm, n_tiles=T // tm),
        grid_spec=pltpu.PrefetchScalarGridSpec(
            num_scalar_prefetch=2, grid=(T // tm,),
            in_specs=[pl.BlockSpec(memory_space=pl.ANY),
                      pl.BlockSpec((tm * SLOT_ROWS, LANES), lambda i, c, s: (i, 0))],
            out_specs=pl.BlockSpec(memory_space=pl.ANY),
            scratch_shapes=[pltpu.SMEM((TOPK * tm,), I32), pltpu.SMEM((TOPK * tm,), I32),
                            pltpu.VMEM((bm // 2 * SLOT_ROWS, LANES), U32),
                            pltpu.SemaphoreType.DMA((2,)), pltpu.SemaphoreType.DMA, pltpu.SemaphoreType.DMA]),
        out_shape=jax.ShapeDtypeStruct((P * SLOT_ROWS, LANES), U32),
        compiler_params=pltpu.CompilerParams(dimension_semantics=("arbitrary",)),
        name="dispatch",
    )(counts, pad_start, dest_tiles, x1t)


def _expert_kernel(blk_e_ref, n_used_ref, x_ref, wgu_ref, wdn_ref, y_ref, wgu_b, wdn_b):
    b = pl.program_id(0)
    live = b < n_used_ref[0]

    @pl.when(live & ((b == 0) | (blk_e_ref[b] != blk_e_ref[jnp.maximum(b - 1, 0)])))
    def _():
        wgu_b[...] = wgu_ref[0, 0].astype(BF16)
        wdn_b[...] = wdn_ref[0, 0].astype(BF16)

    @pl.when(live)
    def _():
        gu = _mm(_unpack_tokens(x_ref), wgu_b[...])
        gate, up = gu[:, :DE], gu[:, DE:]
        _pack_tokens(y_ref, _mm(gate * _sigmoid(gate) * up, wdn_b[...]))

    @pl.when(b >= n_used_ref[0])
    def _():
        y_ref[...] = jnp.zeros(y_ref.shape, U32)


def _experts(blk_e, n_used, xs, wgu, wdn, layer, bm):
    n_blocks = xs.shape[0] // (bm * SLOT_ROWS)
    live = lambda b, n: jnp.minimum(b, n[0] - 1)
    return pl.pallas_call(
        _expert_kernel,
        grid_spec=pltpu.PrefetchScalarGridSpec(
            num_scalar_prefetch=2, grid=(n_blocks,),
            in_specs=[pl.BlockSpec((bm * SLOT_ROWS, LANES), lambda b, e, n: (live(b, n), 0)),
                      pl.BlockSpec((1, 1, D, 2 * DE), lambda b, e, n: (layer, e[live(b, n)], 0, 0)),
                      pl.BlockSpec((1, 1, DE, D), lambda b, e, n: (layer, e[live(b, n)], 0, 0))],
            out_specs=pl.BlockSpec((bm * SLOT_ROWS, LANES), lambda b, e, n: (b, 0)),
            scratch_shapes=[pltpu.VMEM((D, 2 * DE), BF16), pltpu.VMEM((DE, D), BF16)]),
        out_shape=jax.ShapeDtypeStruct(xs.shape, U32),
        compiler_params=pltpu.CompilerParams(dimension_semantics=("arbitrary",)),
        name="experts",
    )(blk_e, n_used, xs, wgu, wdn)


def _combine_kernel(dest_hbm, y_hbm, x_ref, w_ref, wsg_ref, wsd_ref, lg_ref, lb_ref, o_ref, dest_a, dest_b, rows_ref,
                    routed_ref, sem_idx, sem, *, tm, alpha, tile0, n_tiles, chunk):
    g = pl.program_id(0)
    t_next = jnp.minimum(g + 1, n_tiles - 1) + tile0

    dest_smem = (dest_a, dest_b)

    def slots_copy(tile, s):
        return pltpu.make_async_copy(dest_hbm.at[tile, 0], dest_smem[s], sem_idx.at[s])

    def issue(s, r):
        for k in range(TOPK):
            pltpu.make_async_copy(_tile_rows(y_hbm, dest_smem[s][k * tm + r]), _tile_rows(rows_ref.at[s, k], r),
                                  sem.at[s]).start(priority=k % 2)

    def wait_rows(s):
        for k in range(TOPK):
            pltpu.make_async_copy(_tile_rows(y_hbm, 0, tm), rows_ref.at[s, k], sem.at[s]).wait()

    @pl.when(g == 0)
    def _():
        cp = slots_copy(tile0, 0)
        cp.start()
        cp.wait()

        def first(r, c):
            issue(0, r)
            return c

        lax.fori_loop(0, tm, first, 0)

    def step(cur, nxt):
        slots_next = slots_copy(t_next, nxt)
        slots_next.start()
        x = x_ref[...]
        gu = _mm(x, wsg_ref[...])
        gate, up = gu[:, :DE], gu[:, DE:]
        shared = _mm(gate * _sigmoid(gate) * up, wsd_ref[...])
        slots_next.wait()
        wait_rows(cur)

        def chunk_body(c, carry):
            r0 = pl.multiple_of(c * chunk, chunk)
            acc = None
            for k in range(TOPK):
                yk = _unpack_tokens(rows_ref.at[cur, k, pl.ds(r0 * SLOT_ROWS, chunk * SLOT_ROWS)])
                term = yk * w_ref[pl.ds(r0, chunk), k:k + 1]
                acc = term if acc is None else acc + term
            routed_ref[pl.ds(r0, chunk), :] = acc
            for rr in range(chunk):
                issue(nxt, r0 + rr)
            return carry

        lax.fori_loop(0, tm // chunk, chunk_body, 0)
        o_ref[...] = _layer_norm(alpha * x + (routed_ref[...] + shared), lg_ref[...], lb_ref[...])

        @pl.when(g == n_tiles - 1)
        def _():
            wait_rows(nxt)

    for parity in range(2):
        pl.when(g % 2 == parity)(functools.partial(step, parity, 1 - parity))


def _combine(dest_tiles, ys, x1, w_tok, wsg, wsd, lg, lb, alpha, tm, tile0=0, n_tiles=None):
    n_tiles = x1.shape[0] // tm if n_tiles is None else n_tiles
    full = lambda a: pl.BlockSpec(a.shape, lambda i: (0,) * a.ndim)
    return pl.pallas_call(
        functools.partial(_combine_kernel, tm=tm, alpha=alpha, tile0=tile0, n_tiles=n_tiles, chunk=COMBINE_CHUNK),
        grid=(n_tiles,),
        in_specs=[pl.BlockSpec(memory_space=pl.ANY), pl.BlockSpec(memory_space=pl.ANY),
                  pl.BlockSpec((tm, D), lambda i: (i + tile0, 0)), pl.BlockSpec((tm, TOPK), lambda i: (i + tile0, 0))]
        + [full(a) for a in (wsg, wsd, lg, lb)],
        out_specs=pl.BlockSpec((tm, D), lambda i: (i, 0)),
        out_shape=jax.ShapeDtypeStruct((n_tiles * tm, D), F32),
        scratch_shapes=[pltpu.SMEM((TOPK * tm,), I32), pltpu.SMEM((TOPK * tm,), I32),
                        pltpu.VMEM((2, TOPK, tm * SLOT_ROWS, LANES), U32),
                        pltpu.VMEM((tm, D), F32), pltpu.SemaphoreType.DMA((2,)), pltpu.SemaphoreType.DMA((2,))],
        compiler_params=pltpu.CompilerParams(dimension_semantics=("arbitrary",)),
        name="combine",
    )(dest_tiles, ys, x1, w_tok, wsg, wsd, lg, lb)


def _regroup_w_in(w, b):
    offs = np.cumsum([0, 512, 128, 128, 384, 192, 192, 512, 512, 512, 512, 16, 3 * D])
    qa, ka, va, qb, kb, vb, qc, kc, vc, oc, gc, gates = (int(o) for o in offs[:-1])
    pieces = [(qa, ka), (ka, va), (va, qb)]
    for gi in range(3):
        pieces += [(qb + gi * 128, qb + (gi + 1) * 128), (kb + gi * HD, kb + (gi + 1) * HD),
                   (vb + gi * HD, vb + (gi + 1) * HD)]
    pieces += [(qc, kc), (kc, vc), (vc, oc), (oc, gc), (gc, gates), None]
    gc_pad = (N_UNITS - U_GC) * LANES - (gates - gc)

    def build(a):
        parts = [jnp.zeros(a.shape[:-1] + (gc_pad,), a.dtype) if p is None else a[..., p[0]:p[1]] for p in pieces]
        return jnp.concatenate(parts, axis=-1)

    return (build(w).astype(BF16), build(b).reshape(1, NPAD), w[:, gates:].astype(BF16), b[gates:].reshape(1, 3 * D))


def _trunk(xa, xb, seq_lens, depth, params, *, bm, tm_proj, tn_proj, tm_merge, tm_route, tm_moe, split=None):
    (w_in, b_in, sink, conv_w, w_br_a, w_br_b, w_br_c, w_out, ln1_g, ln1_b, w_router, router_bias, w_exp_gu,
     w_exp_down, w_sh_gu, w_sh_down, ln2_g, ln2_b) = params
    rows_a = xa.shape[0]
    T = rows_a + xb.shape[0]
    alpha = float((2 * depth) ** 0.25)
    slopes_a = tuple(float(2.0 ** (-8.0 * i / 8)) for i in range(1, 9))
    slopes_b = tuple(float(2.0 ** (-8.0 * i / 6)) for i in range(1, 7))
    flags1 = jnp.asarray(_edge_flags(seq_lens, QB))
    flags_b = [jnp.asarray(_edge_flags(seq_lens, QB * d)) for d in B_DILS]
    A = T * TOPK
    n_blocks = A // bm + NE
    P = n_blocks * bm
    row2 = lambda a: a.reshape(1, -1)
    for l in range(depth):
        w_all, b_all, w_gates, b_gates = _regroup_w_in(w_in[l], b_in[l])
        proj = _proj(xa, xb, rows_a, T, w_all, b_all, tm_proj, tn_proj)
        oa, ob0, lse0 = _attn_ab0(proj, flags1, sink[l].reshape(1, A_HK * A_G), slopes_a, slopes_b[:B_G])
        obs, lses = [ob0], [lse0]
        for gi in (1, 2):
            o, s = _attn_b(proj, flags_b[gi], gi, slopes_b[gi * B_G:(gi + 1) * B_G])
            obs.append(o)
            lses.append(s)
        gct = proj[:, U_GC * LANES:U_GC * LANES + 16].T
        hf, hb = _mlstm(proj, gct, conv_w[l], jnp.asarray(_edge_flags(seq_lens, 2 * CHUNK)))
        x1, x1t = _merge(xa, xb, rows_a, proj, oa, obs, lses, hf, hb, w_gates, b_gates, w_br_a[l].astype(BF16),
                         w_br_b[l].astype(BF16), w_br_c[l].astype(BF16), w_out[l].astype(BF16), row2(ln1_g[l]),
                         row2(ln1_b[l]), alpha, tm_merge)
        top_idx, top_w, rank, counts = _route(x1, w_router[l].T, router_bias[l].reshape(NE, 1), tm_route)
        counts = counts[:, 0]
        padded = (counts + bm - 1) // bm * bm
        pad_end = jnp.cumsum(padded)
        pad_start = pad_end - padded
        dest = _dest_tiles(top_idx, rank, pad_start, tm_moe)
        blk_e = jnp.minimum(jnp.searchsorted(pad_end, jnp.arange(n_blocks, dtype=I32) * bm, side='right'),
                            NE - 1).astype(I32)
        n_used = (pad_end[-1:] // bm).astype(I32)
        xs = _dispatch(counts, pad_start, dest, x1t, P, tm_moe, bm)
        ys = _experts(blk_e, n_used, xs, w_exp_gu, w_exp_down, l, bm)
        comb = functools.partial(_combine, dest, ys, x1, top_w.T, w_sh_gu[l].astype(BF16), w_sh_down[l].astype(BF16),
                                 row2(ln2_g[l]), row2(ln2_b[l]), alpha, tm_moe)
        if l + 1 < depth or split is None:
            xa = xb = comb()
            rows_a = T
        else:
            return comb(0, split // tm_moe), comb(split // tm_moe, (T - split) // tm_moe)
    return xa


def kernel(x_prompt, x_sample, w_in, b_in, sink, conv_w, w_br_a, w_br_b, w_br_c, w_out, ln1_g, ln1_b, w_router,
           router_bias, w_exp_gu, w_exp_down, w_sh_gu, w_sh_down, ln2_g, ln2_b):
    depth = w_in.shape[0]
    n1, s1, _ = x_prompt.shape
    n2, s2, _ = x_sample.shape
    span = QB * max(B_DILS)
    assert s1 % span == 0 and s2 % span == 0, "sequence lengths must be multiples of the widest dilated block"
    params = (w_in, b_in, sink, conv_w, w_br_a, w_br_b, w_br_c, w_out, ln1_g, ln1_b, w_router, router_bias,
              w_exp_gu, w_exp_down, w_sh_gu, w_sh_down, ln2_g, ln2_b)
    y1, y2 = _trunk(x_prompt.reshape(n1 * s1, D), x_sample.reshape(n2 * s2, D), [s1] * n1 + [s2] * n2, depth, params,
                    bm=1024, tm_proj=1024, tn_proj=1024, tm_merge=256, tm_route=256, tm_moe=512, split=n1 * s1)
    return y1.reshape(n1, s1, D), y2.reshape(n2, s2, D)
```
